```python
import math
import jax
import jax.numpy as jnp
from jax import lax
import numpy as np

D_MODEL = 1024
BATCH = 32
SEQ = 2048
DEPTH = 2

N_A_LAYERS = (DEPTH + 1) // 2
N_B_LAYERS = DEPTH - N_A_LAYERS

GDN_HEADS = 8
GDN_HEAD_DIM = 128
GDN_WIDTH = GDN_HEADS * GDN_HEAD_DIM
GDN_CONV = 4
GDN_CHUNK = 64

NSA_HEADS = 16
NSA_GROUPS = 4
NSA_HPG = NSA_HEADS // NSA_GROUPS
NSA_HEAD_DIM = 64
NSA_WIDTH = NSA_HEADS * NSA_HEAD_DIM
CMP_BLOCK = 32
CMP_STRIDE = 16
CMP_HIDDEN = 2 * NSA_HEAD_DIM
SEL_BLOCK = 64
TOP_N = 16
WINDOW = 512
Q_BLOCK = 128
N_BRANCHES = 3
N_KV_PARTS = 2 * N_BRANCHES

REL_BUCKETS = 32
REL_MAX_DIST = 128

MOE_GROUPS = 4
MOE_EXPERTS_PER_GROUP = 4
MOE_EXPERTS = MOE_GROUPS * MOE_EXPERTS_PER_GROUP
MOE_TOP_K = 2
MOE_HIDDEN = 256

DEEPNORM_ALPHA = (2 * DEPTH) ** 0.25
DEEPNORM_BETA = (8 * DEPTH) ** -0.25
LN_EPS = 1e-5
NORM_EPS = 1e-6
NEG_INF = -1e30

kernel_name = 'hybrid_gdn_nsa_hmoe_deepnorm'


def layer_norm(x, g, b):
    xf = x.astype(jnp.float32)
    mu = jnp.mean(xf, -1, keepdims=True)
    var = jnp.mean(jnp.square(xf - mu), -1, keepdims=True)
    return ((xf - mu) * lax.rsqrt(var + LN_EPS) * g.astype(jnp.float32) + b.astype(jnp.float32)).astype(x.dtype)


def rms_norm(x, g):
    xf = x.astype(jnp.float32)
    return xf * lax.rsqrt(jnp.mean(xf * xf, -1, keepdims=True) + NORM_EPS) * g.astype(jnp.float32)


def l2_normalize(x):
    xf = x.astype(jnp.float32)
    return xf * lax.rsqrt(jnp.sum(xf * xf, -1, keepdims=True) + NORM_EPS)


def causal_depthwise_conv(x, w):
    k, c = w.shape
    return lax.conv_general_dilated(x, w[:, None, :].astype(x.dtype), window_strides=(1,),
                                    padding=[(k - 1, 0)], dimension_numbers=('NWC', 'WIO', 'NWC'),
                                    feature_group_count=c)


def masked_softmax(s, mask):
    return jax.nn.softmax(jnp.where(mask, s, NEG_INF), axis=-1) * mask


def gated_delta_rule_chunked(q, k, v, g, beta):
    b, t, h, dk = q.shape
    dv = v.shape[-1]
    c = GDN_CHUNK
    n = t // c

    def chunks(a):
        a = jnp.moveaxis(a.astype(jnp.float32), 1, 2)
        return a.reshape((b, h, n, c) + a.shape[3:])

    q, k, v, g, beta = chunks(q), chunks(k), chunks(v), chunks(g), chunks(beta)
    gc = jnp.cumsum(g, axis=-1)
    idx = jnp.arange(c)
    causal = idx[:, None] >= idx[None, :]
    decay = jnp.exp(jnp.where(causal, gc[..., :, None] - gc[..., None, :], -jnp.inf))
    k_beta = k * beta[..., None]
    a_strict = jnp.where(idx[:, None] > idx[None, :],
                         jnp.einsum('bhncd,bhnsd->bhncs', k_beta, k) * decay, 0.0)
    tri = a_strict + jnp.eye(c, dtype=jnp.float32)
    u = lax.linalg.triangular_solve(tri, v * beta[..., None], left_side=True, lower=True, unit_diagonal=True)
    w = lax.linalg.triangular_solve(tri, k_beta * jnp.exp(gc)[..., None], left_side=True, lower=True,
                                    unit_diagonal=True)
    qk = jnp.einsum('bhncd,bhnsd->bhncs', q, k) * decay
    g_last = gc[..., -1:]
    q_dec = q * jnp.exp(gc)[..., None]
    k_dec = k * jnp.exp(g_last - gc)[..., None]
    d_last = jnp.exp(g_last[..., 0])
    xs = (jnp.moveaxis(q_dec, 2, 0), jnp.moveaxis(qk, 2, 0), jnp.moveaxis(u, 2, 0),
          jnp.moveaxis(w, 2, 0), jnp.moveaxis(k_dec, 2, 0), jnp.moveaxis(d_last, 2, 0))

    def step(state, inp):
        q_c, qk_c, u_c, w_c, k_c, d_c = inp
        v_new = u_c - jnp.einsum('bhcd,bhde->bhce', w_c, state)
        o_c = jnp.einsum('bhcd,bhde->bhce', q_c, state) + jnp.einsum('bhcs,bhse->bhce', qk_c, v_new)
        state = state * d_c[..., None, None] + jnp.einsum('bhcd,bhce->bhde', k_c, v_new)
        return state, o_c

    _, o = lax.scan(step, jnp.zeros((b, h, dk, dv), jnp.float32), xs)
    return jnp.moveaxis(o, 0, 2).reshape(b, h, t, dv).transpose(0, 2, 1, 3)


def gdn_mixer(x, w_in, conv_w, a_log, dt_bias, norm_w, w_o):
    b, t, _ = x.shape
    hk = GDN_WIDTH
    proj = x @ w_in
    qkv = jax.nn.silu(causal_depthwise_conv(proj[..., :3 * hk], conv_w))
    shp = (b, t, GDN_HEADS, GDN_HEAD_DIM)
    q = l2_normalize(qkv[..., :hk].reshape(shp)) * (GDN_HEAD_DIM ** -0.5)
    k = l2_normalize(qkv[..., hk:2 * hk].reshape(shp))
    v = qkv[..., 2 * hk:3 * hk].reshape(shp)
    z = proj[..., 3 * hk:4 * hk].reshape(shp).astype(jnp.float32)
    a = proj[..., 4 * hk:4 * hk + GDN_HEADS].astype(jnp.float32)
    bt = proj[..., 4 * hk + GDN_HEADS:].astype(jnp.float32)
    g = -jnp.exp(a_log.astype(jnp.float32)) * jax.nn.softplus(a + dt_bias.astype(jnp.float32))
    beta = jax.nn.sigmoid(bt)
    o = gated_delta_rule_chunked(q, k, v, g, beta)
    o = rms_norm(o, norm_w) * jax.nn.silu(z)
    return o.reshape(b, t, hk).astype(x.dtype) @ w_o


def t5_bucket(dist):
    n = jnp.maximum(dist, 0)
    max_exact = REL_BUCKETS // 2
    nf = jnp.maximum(n, 1).astype(jnp.float32)
    large = max_exact + (jnp.log(nf / max_exact) / math.log(REL_MAX_DIST / max_exact)
                         * (REL_BUCKETS - max_exact)).astype(jnp.int32)
    large = jnp.minimum(large, REL_BUCKETS - 1)
    return jnp.where(n < max_exact, n, large)


def t5_bias(dist, rel_tbl):
    dist = jnp.broadcast_to(dist, (NSA_GROUPS,) + dist.shape[-2:])
    tbl = rel_tbl.astype(jnp.float32).reshape(REL_BUCKETS, NSA_GROUPS, NSA_HPG)
    bias = tbl[t5_bucket(dist), jnp.arange(NSA_GROUPS)[:, None, None]]
    return bias.transpose(0, 3, 1, 2)


def compress_blocks(x, pos, w1, w2):
    b, t, g, dk = x.shape
    nc = (t - CMP_BLOCK) // CMP_STRIDE + 1
    idx = jnp.arange(nc)[:, None] * CMP_STRIDE + jnp.arange(CMP_BLOCK)[None, :]
    blk = x[:, idx] + pos[None, None, :, None, :]
    blk = blk.transpose(0, 3, 1, 2, 4).reshape(b, g, nc, CMP_BLOCK * dk)
    return jax.nn.silu(blk @ w1) @ w2


def nsa_shared_kv(h, w_kv, cmp_k_pos, cmp_k_w1, cmp_k_w2, cmp_v_pos, cmp_v_w1, cmp_v_w2):
    b, t, _ = h.shape
    kv = (h @ w_kv).reshape(b, t, N_KV_PARTS, NSA_GROUPS, NSA_HEAD_DIM)
    kc_raw, vc_raw, ks, vs, kw, vw = jnp.moveaxis(kv, 2, 0)
    kc = compress_blocks(kc_raw, cmp_k_pos, cmp_k_w1, cmp_k_w2)
    vc = compress_blocks(vc_raw, cmp_v_pos, cmp_v_w1, cmp_v_w2)
    nb = t // SEL_BLOCK

    def to_blocks(a):
        return a.transpose(0, 2, 1, 3).reshape(b, NSA_GROUPS, nb, SEL_BLOCK, NSA_HEAD_DIM)

    def pad_front(a):
        return jnp.pad(a.transpose(0, 2, 1, 3), ((0, 0), (0, 0), (WINDOW, 0), (0, 0)))

    return (kc, vc, to_blocks(ks), to_blocks(vs), pad_front(kw), pad_front(vw))


def nsa_attention(q, gates, kc, vc, ks, vs, kw, vw, rel_tbl):
    b, t = q.shape[:2]
    nc = kc.shape[2]
    nb = ks.shape[2]
    nq = t // Q_BLOCK
    n_sel = min(TOP_N, nb)
    scale = NSA_HEAD_DIM ** -0.5
    cmp_end = jnp.arange(nc) * CMP_STRIDE + CMP_BLOCK - 1
    cmp_start = jnp.arange(nc) * CMP_STRIDE
    sel_start = jnp.arange(nb) * SEL_BLOCK
    overlap = ((cmp_start[:, None] < sel_start[None, :] + SEL_BLOCK)
               & (cmp_end[:, None] >= sel_start[None, :])).astype(jnp.float32)
    g_ix = jnp.arange(NSA_GROUPS)[:, None, None]

    def per_batch(args):
        q_b, gate_b, kc_b, vc_b, ks_b, vs_b, kw_b, vw_b = args
        kc_f, vc_f = kc_b.astype(jnp.float32), vc_b.astype(jnp.float32)
        qb = q_b.reshape(nq, Q_BLOCK, NSA_GROUPS, NSA_HPG, NSA_HEAD_DIM)
        gb = gate_b.reshape(nq, Q_BLOCK, NSA_GROUPS, NSA_HPG, N_BRANCHES)

        def per_block(blk):
            q_c, gate_c, c = blk
            tq = c * Q_BLOCK + jnp.arange(Q_BLOCK)
            qf = q_c.astype(jnp.float32)
            s = jnp.einsum('qghd,gnd->ghqn', qf, kc_f) * scale + t5_bias(tq[:, None] - cmp_end[None, :], rel_tbl)
            p_cmp = masked_softmax(s, cmp_end[None, :] <= tq[:, None])
            o_cmp = jnp.einsum('ghqn,gnd->qghd', p_cmp, vc_f)
            imp = jnp.einsum('ghqn,nj->gqj', p_cmp, overlap)
            jq = (tq // SEL_BLOCK)[:, None]
            jb = jnp.arange(nb)[None, :]
            forced = (jb == 0) | (jb == jq) | (jb == jq - 1)
            imp = jnp.where(forced, jnp.inf, jnp.where(jb <= jq, imp, -jnp.inf))
            _, sel = lax.top_k(imp, n_sel)
            k_sel = ks_b[g_ix, sel].reshape(NSA_GROUPS, Q_BLOCK, n_sel * SEL_BLOCK, NSA_HEAD_DIM)
            v_sel = vs_b[g_ix, sel].reshape(NSA_GROUPS, Q_BLOCK, n_sel * SEL_BLOCK, NSA_HEAD_DIM)
            kpos = (sel[..., None] * SEL_BLOCK + jnp.arange(SEL_BLOCK)).reshape(NSA_GROUPS, Q_BLOCK, -1)
            dist = tq[None, :, None] - kpos
            s = jnp.einsum('qghd,gqkd->ghqk', qf, k_sel.astype(jnp.float32)) * scale + t5_bias(dist, rel_tbl)
            p = masked_softmax(s, (dist >= 0)[:, None])
            o_sel = jnp.einsum('ghqk,gqkd->qghd', p, v_sel.astype(jnp.float32))
            k_win = lax.dynamic_slice_in_dim(kw_b, c * Q_BLOCK, WINDOW + Q_BLOCK, axis=1).astype(jnp.float32)
            v_win = lax.dynamic_slice_in_dim(vw_b, c * Q_BLOCK, WINDOW + Q_BLOCK, axis=1).astype(jnp.float32)
            kp = c * Q_BLOCK - WINDOW + jnp.arange(WINDOW + Q_BLOCK)
            dist = tq[:, None] - kp[None, :]
            mask = (dist >= 0) & (dist < WINDOW) & (kp[None, :] >= 0)
            s = jnp.einsum('qghd,gkd->ghqk', qf, k_win) * scale + t5_bias(dist, rel_tbl)
            p = masked_softmax(s, mask)
            o_win = jnp.einsum('ghqk,gkd->qghd', p, v_win)
            gt = jax.nn.sigmoid(gate_c.astype(jnp.float32))
            return gt[..., 0:1] * o_cmp + gt[..., 1:2] * o_sel + gt[..., 2:3] * o_win

        o = lax.map(per_block, (qb, gb, jnp.arange(nq)))
        return o.reshape(t, NSA_WIDTH)

    return lax.map(per_batch, (q, gates, kc, vc, ks, vs, kw, vw))


def nsa_mixer(h, shared, w_in, w_o, rel_tbl):
    b, t, _ = h.shape
    proj = h @ w_in
    q = proj[..., :NSA_WIDTH].reshape(b, t, NSA_GROUPS, NSA_HPG, NSA_HEAD_DIM)
    gates = proj[..., NSA_WIDTH:].reshape(b, t, NSA_GROUPS, NSA_HPG, N_BRANCHES)
    kc, vc, ks, vs, kw, vw = shared
    o = nsa_attention(q, gates, kc, vc, ks, vs, kw, vw, rel_tbl)
    return o.astype(h.dtype) @ w_o


def hier_moe(x, w_grp, b_grp, w_rt, b_rt, w_gate, w_up, w_down):
    b, t, d = x.shape
    xt = x.reshape(b * t, d)
    grp_p = jax.nn.softmax((xt @ w_grp + b_grp).astype(jnp.float32), axis=-1)
    grp_prob, grp_idx = lax.top_k(grp_p, 1)
    e_logits = (xt @ w_rt + b_rt).astype(jnp.float32).reshape(-1, MOE_GROUPS, MOE_EXPERTS_PER_GROUP)
    e_logits = jnp.take_along_axis(e_logits, grp_idx[:, :, None], axis=1)[:, 0]
    top_p, top_i = lax.top_k(jax.nn.softmax(e_logits, axis=-1), MOE_TOP_K)
    top_w = grp_prob * top_p / jnp.sum(top_p, -1, keepdims=True)
    expert_id = grp_idx * MOE_EXPERTS_PER_GROUP + top_i
    gate = jnp.sum(jax.nn.one_hot(expert_id, MOE_EXPERTS, dtype=jnp.float32) * top_w[..., None], axis=1)

    def step(acc, e_in):
        wg, wu, wd, ge = e_in
        hdn = jax.nn.silu(xt @ wg) * (xt @ wu)
        return acc + ge[:, None] * (hdn @ wd).astype(jnp.float32), None

    y, _ = lax.scan(step, jnp.zeros((b * t, d), jnp.float32), (w_gate, w_up, w_down, gate.T))
    return y.astype(x.dtype).reshape(b, t, d)


def setup_inputs(seed: int = 0) -> dict:
    key = jax.random.key(seed)
    keys = iter(jax.random.split(key, 40))

    def nrm(shape, scale):
        return jax.random.normal(next(keys), shape, jnp.float32) * scale

    na, nb_l, d = N_A_LAYERS, N_B_LAYERS, D_MODEL
    dt = jnp.exp(jax.random.uniform(next(keys), (na, GDN_HEADS), jnp.float32,
                                    minval=math.log(1e-3), maxval=math.log(0.1)))
    return {
        'x': nrm((BATCH, SEQ, d), 1.0),
        'gdn_w_in': nrm((na, d, 4 * GDN_WIDTH + 2 * GDN_HEADS), d ** -0.5),
        'gdn_conv_w': nrm((na, GDN_CONV, 3 * GDN_WIDTH), GDN_CONV ** -0.5),
        'gdn_a_log': jnp.log(jax.random.uniform(next(keys), (na, GDN_HEADS), jnp.float32, minval=1.0, maxval=16.0)),
        'gdn_dt_bias': dt + jnp.log(-jnp.expm1(-dt)),
        'gdn_norm_w': 1.0 + nrm((na, GDN_HEAD_DIM), 0.01),
        'gdn_w_o': nrm((na, GDN_WIDTH, d), GDN_WIDTH ** -0.5 * DEEPNORM_BETA),
        'nsa_w_kv': nrm((d, N_KV_PARTS * NSA_GROUPS * NSA_HEAD_DIM), d ** -0.5),
        'cmp_k_pos': nrm((CMP_BLOCK, NSA_HEAD_DIM), 0.1),
        'cmp_k_w1': nrm((CMP_BLOCK * NSA_HEAD_DIM, CMP_HIDDEN), (CMP_BLOCK * NSA_HEAD_DIM) ** -0.5),
        'cmp_k_w2': nrm((CMP_HIDDEN, NSA_HEAD_DIM), CMP_HIDDEN ** -0.5),
        'cmp_v_pos': nrm((CMP_BLOCK, NSA_HEAD_DIM), 0.1),
        'cmp_v_w1': nrm((CMP_BLOCK * NSA_HEAD_DIM, CMP_HIDDEN), (CMP_BLOCK * NSA_HEAD_DIM) ** -0.5),
        'cmp_v_w2': nrm((CMP_HIDDEN, NSA_HEAD_DIM), CMP_HIDDEN ** -0.5),
        'nsa_w_in': nrm((nb_l, d, NSA_WIDTH + N_BRANCHES * NSA_HEADS), d ** -0.5),
        'nsa_w_o': nrm((nb_l, NSA_WIDTH, d), NSA_WIDTH ** -0.5 * DEEPNORM_BETA),
        'rel_bias': nrm((REL_BUCKETS, NSA_HEADS), 0.5),
        'ln_mix_g': 1.0 + nrm((DEPTH, d), 0.01),
        'ln_mix_b': nrm((DEPTH, d), 0.01),
        'ln_ffn_g': 1.0 + nrm((DEPTH, d), 0.01),
        'ln_ffn_b': nrm((DEPTH, d), 0.01),
        'moe_w_grp': nrm((DEPTH, d, MOE_GROUPS), d ** -0.5),
        'moe_b_grp': nrm((DEPTH, MOE_GROUPS), 0.01),
        'moe_w_rt': nrm((DEPTH, d, MOE_EXPERTS), d ** -0.5),
        'moe_b_rt': nrm((DEPTH, MOE_EXPERTS), 0.01),
        'moe_w_gate': nrm((DEPTH, MOE_EXPERTS, d, MOE_HIDDEN), d ** -0.5),
        'moe_w_up': nrm((DEPTH, MOE_EXPERTS, d, MOE_HIDDEN), d ** -0.5),
        'moe_w_down': nrm((DEPTH, MOE_EXPERTS, MOE_HIDDEN, d), MOE_HIDDEN ** -0.5 * DEEPNORM_BETA),
    }


def reference(x, gdn_w_in, gdn_conv_w, gdn_a_log, gdn_dt_bias, gdn_norm_w, gdn_w_o,
              nsa_w_kv, cmp_k_pos, cmp_k_w1, cmp_k_w2, cmp_v_pos, cmp_v_w1, cmp_v_w2,
              nsa_w_in, nsa_w_o, rel_bias,
              ln_mix_g, ln_mix_b, ln_ffn_g, ln_ffn_b,
              moe_w_grp, moe_b_grp, moe_w_rt, moe_b_rt, moe_w_gate, moe_w_up, moe_w_down):
    h = x
    shared = None
    for layer in range(DEPTH):
        if layer < N_A_LAYERS:
            i = layer
            mix = gdn_mixer(h, gdn_w_in[i], gdn_conv_w[i], gdn_a_log[i], gdn_dt_bias[i], gdn_norm_w[i], gdn_w_o[i])
        else:
            i = layer - N_A_LAYERS
            mix = nsa_mixer(h, shared, nsa_w_in[i], nsa_w_o[i], rel_bias)
        h = layer_norm(DEEPNORM_ALPHA * h + mix, ln_mix_g[layer], ln_mix_b[layer])
        ffn = hier_moe(h, moe_w_grp[layer], moe_b_grp[layer], moe_w_rt[layer], moe_b_rt[layer],
                       moe_w_gate[layer], moe_w_up[layer], moe_w_down[layer])
        h = layer_norm(DEEPNORM_ALPHA * h + ffn, ln_ffn_g[layer], ln_ffn_b[layer])
        if layer == N_A_LAYERS - 1:
            shared = nsa_shared_kv(h, nsa_w_kv, cmp_k_pos, cmp_k_w1, cmp_k_w2, cmp_v_pos, cmp_v_w1, cmp_v_w2)
    return h
```

```python
import functools
import math

import jax
import jax.numpy as jnp
from jax import lax
from jax.experimental import pallas as pl
from jax.experimental.pallas import tpu as pltpu

F32 = jnp.float32
BF16 = jnp.bfloat16

D_MODEL = 1024
DEPTH = 2
GDN_HEADS = 8
GDN_HEAD_DIM = 128
GDN_WIDTH = GDN_HEADS * GDN_HEAD_DIM
GDN_CONV = 4
GDN_CHUNK = 64
NSA_HEADS = 16
NSA_GROUPS = 4
NSA_HPG = NSA_HEADS // NSA_GROUPS
NSA_HEAD_DIM = 64
NSA_WIDTH = NSA_HEADS * NSA_HEAD_DIM
CMP_BLOCK = 32
CMP_STRIDE = 16
CMP_HIDDEN = 2 * NSA_HEAD_DIM
SEL_BLOCK = 64
TOP_N = 16
WINDOW = 512
Q_BLOCK = 128
N_BRANCHES = 3
REL_BUCKETS = 32
REL_MAX_DIST = 128
MOE_GROUPS = 4
MOE_EPG = 4
MOE_EXPERTS = MOE_GROUPS * MOE_EPG
MOE_HIDDEN = 256
DEEPNORM_ALPHA = (2 * DEPTH) ** 0.25
LN_EPS = 1e-5
NORM_EPS = 1e-6
NEG = -1e30

LANES = 128
VMEM_LIMIT = 56 * 1024 * 1024


def _dot(a, b):
    return jnp.dot(a, b, preferred_element_type=F32)


def _dot_nt(a, b):
    return lax.dot_general(a, b, (((1,), (1,)), ((), ())), preferred_element_type=F32)


def _dot_tn(a, b):
    return lax.dot_general(a, b, (((0,), (0,)), ((), ())), preferred_element_type=F32)


def _split2(x):
    hi = x.astype(BF16)
    lo = (x - hi.astype(F32)).astype(BF16)
    return hi, lo


def _silu(x):
    return x * jax.nn.sigmoid(x)


def _params(sem):
    return pltpu.CompilerParams(dimension_semantics=sem, vmem_limit_bytes=VMEM_LIMIT)


def _layer_norm(x, g, b):
    mu = jnp.mean(x, axis=-1, keepdims=True)
    xc = x - mu
    var = jnp.mean(xc * xc, axis=-1, keepdims=True)
    return xc * lax.rsqrt(var + LN_EPS) * g + b


def _gdn_inproj_kernel(x_ref, w_ref, wabh_ref, wabl_ref, cw_ref, alog_ref, dtb_ref, ltri_ref,
                       q_ref, k_ref, v_ref, z_ref, gcb_ref, gcbT_ref, carry_ref, *, tm):
    @pl.when(pl.program_id(1) == 0)
    def _():
        carry_ref[...] = jnp.zeros_like(carry_ref)

    x = x_ref[0]
    xb, xlo = _split2(x)

    wh = wabh_ref[...]
    ab = _dot(xb, wh) + (_dot(xlo, wh) + _dot(xb, wabl_ref[...]))
    lane = lax.broadcasted_iota(jnp.int32, ab.shape, 1)
    sp_in = ab + dtb_ref[...]
    softplus = jnp.maximum(sp_in, 0.0) + jnp.log1p(jnp.exp(-jnp.abs(sp_in)))
    g = jnp.where(lane < GDN_HEADS, -jnp.exp(alog_ref[...]) * softplus, 0.0)
    beta = jax.nn.sigmoid(ab)
    g1 = g.astype(BF16)
    r1 = g - g1.astype(F32)
    g2 = r1.astype(BF16)
    g3 = (r1 - g2.astype(F32)).astype(BF16)
    ltri = ltri_ref[...]
    gc = _dot(ltri, g1) + (_dot(ltri, g2) + _dot(ltri, g3))
    gcb = jnp.where(lane < GDN_HEADS, gc, jnp.where(lane < 2 * GDN_HEADS, beta, 0.0))
    gcb_ref[0] = gcb
    gcbT_ref[0] = gcb.T[:2 * GDN_HEADS, :]

    row8 = lax.broadcasted_iota(jnp.int32, (8, 256), 0)
    outs = (q_ref, k_ref, v_ref)
    for s in range(3):
        for cc in range(4):
            col = s * GDN_WIDTH + cc * 256
            y = _dot(xb, w_ref[:, col:col + 256])
            prev = carry_ref[s * 4 + cc]
            carry_ref[s * 4 + cc] = y[tm - 8:, :]
            cw = cw_ref[:, col:col + 256]
            acc = y * cw[3:4, :]
            for kk in range(1, GDN_CONV):
                ry = pltpu.roll(y, kk, 0)
                rp = pltpu.roll(prev, kk, 0)
                head = jnp.where(row8 < kk, rp, ry[:8, :])
                shifted = jnp.concatenate([head, ry[8:, :]], axis=0)
                acc = acc + shifted * cw[3 - kk:4 - kk, :]
            a = _silu(acc)
            if s < 2:
                halves = []
                for hh in range(2):
                    ah = a[:, hh * LANES:(hh + 1) * LANES]
                    ss = jnp.sum(ah * ah, axis=-1, keepdims=True)
                    scale = lax.rsqrt(ss + NORM_EPS)
                    if s == 0:
                        scale = scale * (GDN_HEAD_DIM ** -0.5)
                    halves.append(ah * scale)
                a = jnp.concatenate(halves, axis=1)
            outs[s][0, :, cc * 256:(cc + 1) * 256] = a.astype(BF16)
    for cc in range(4):
        col = 3 * GDN_WIDTH + cc * 256
        z_ref[0, :, cc * 256:(cc + 1) * 256] = _dot(xb, w_ref[:, col:col + 256]).astype(BF16)


def gdn_inproj(x, w_in, conv_w, a_log, dt_bias, tm=256):
    b, t, d = x.shape
    hk = GDN_WIDTH
    w_main = w_in[:, :4 * hk].astype(BF16)
    w_ab = jnp.zeros((d, LANES), F32).at[:, :2 * GDN_HEADS].set(w_in[:, 4 * hk:])
    wabh = w_ab.astype(BF16)
    wabl = (w_ab - wabh.astype(F32)).astype(BF16)
    alog = jnp.zeros((1, LANES), F32).at[0, :GDN_HEADS].set(a_log)
    dtb = jnp.zeros((1, LANES), F32).at[0, :GDN_HEADS].set(dt_bias)
    r = jnp.arange(tm)
    ltri = ((r[:, None] // GDN_CHUNK == r[None, :] // GDN_CHUNK) & (r[:, None] >= r[None, :])).astype(BF16)
    tok = lambda bb, tt: (bb, tt, 0)
    const2 = lambda bb, tt: (0, 0)
    act = jax.ShapeDtypeStruct((b, t, hk), BF16)
    return pl.pallas_call(
        functools.partial(_gdn_inproj_kernel, tm=tm),
        grid=(b, t // tm),
        in_specs=[
            pl.BlockSpec((1, tm, d), tok),
            pl.BlockSpec((d, 4 * hk), const2),
            pl.BlockSpec((d, LANES), const2),
            pl.BlockSpec((d, LANES), const2),
            pl.BlockSpec((GDN_CONV, 3 * hk), const2),
            pl.BlockSpec((1, LANES), const2),
            pl.BlockSpec((1, LANES), const2),
            pl.BlockSpec((tm, tm), const2),
        ],
        out_specs=[
            pl.BlockSpec((1, tm, hk), tok),
            pl.BlockSpec((1, tm, hk), tok),
            pl.BlockSpec((1, tm, hk), tok),
            pl.BlockSpec((1, tm, hk), tok),
            pl.BlockSpec((1, tm, LANES), tok),
            pl.BlockSpec((1, 2 * GDN_HEADS, tm), lambda bb, tt: (bb, 0, tt)),
        ],
        out_shape=[act, act, act, act,
                   jax.ShapeDtypeStruct((b, t, LANES), F32),
                   jax.ShapeDtypeStruct((b, 2 * GDN_HEADS, t), F32)],
        scratch_shapes=[pltpu.VMEM((12, 8, 256), F32)],
        compiler_params=_params(("arbitrary", "arbitrary")),
        name="gdn_inproj",
    )(x, w_main, wabh, wabl, conv_w, alog, dtb, ltri)


def _gdn_rec_kernel(q_ref, k_ref, v_ref, z_ref, gcb_ref, gcT_ref, nw_ref, o_ref, s_ref, *, hb_n, n_chunks):
    c = GDN_CHUNK
    hb = pl.program_id(1)
    s_ref[...] = jnp.zeros_like(s_ref)
    row = lax.broadcasted_iota(jnp.int32, (c, c), 0)
    col = lax.broadcasted_iota(jnp.int32, (c, c), 1)
    lane = lax.broadcasted_iota(jnp.int32, (c, LANES), 1)
    sub = lax.broadcasted_iota(jnp.int32, (2 * GDN_HEADS, c), 0)
    eye = (row == col).astype(F32)
    nw = nw_ref[...]

    def body(i, carry):
        r0 = pl.multiple_of(i * c, c)
        gtok = gcb_ref[0, pl.ds(r0, c), :]
        gt = gcT_ref[0, i]
        for j in range(hb_n):
            h = hb * hb_n + j
            sl = slice(j * LANES, (j + 1) * LANES)
            q = q_ref[0, pl.ds(r0, c), sl]
            k = k_ref[0, pl.ds(r0, c), sl]
            v = v_ref[0, pl.ds(r0, c), sl].astype(F32)
            z = z_ref[0, pl.ds(r0, c), sl].astype(F32)
            gc_col = jnp.sum(jnp.where(lane == h, gtok, 0.0), axis=1, keepdims=True)
            beta = jnp.sum(jnp.where(lane == h + GDN_HEADS, gtok, 0.0), axis=1, keepdims=True)
            gc_row = jnp.sum(jnp.where(sub == h, gt, 0.0), axis=0, keepdims=True)
            g_last = gc_row[:, c - 1:c]
            decay = jnp.exp(jnp.where(row >= col, gc_col - gc_row, NEG))
            eg = jnp.exp(gc_col)
            kf = k.astype(F32)
            kb = kf * beta
            a = jnp.where(row > col, _dot_nt(kb.astype(BF16), k) * decay, 0.0)
            pw = -a
            tinv = eye + pw
            for _ in range(5):
                pwb = pw.astype(BF16)
                pw = _dot(pwb, pwb)
                tinv = tinv + _dot(tinv.astype(BF16), pw.astype(BF16))
            rhs = jnp.concatenate([v * beta, kb * eg], axis=1).astype(BF16)
            uw = _dot(tinv.astype(BF16), rhs)
            u = uw[:, :LANES]
            w = uw[:, LANES:]
            qk = _dot_nt(q, k) * decay
            s_old = s_ref[j]
            sb = s_old.astype(BF16)
            lhs = jnp.concatenate([w, q.astype(F32) * eg], axis=0).astype(BF16)
            r = _dot(lhs, sb)
            v_new = u - r[:c]
            v_newb = v_new.astype(BF16)
            o = r[c:] + _dot(qk.astype(BF16), v_newb)
            k_dec = (kf * jnp.exp(g_last - gc_col)).astype(BF16)
            s_ref[j] = s_old * jnp.exp(g_last) + _dot_tn(k_dec, v_newb)
            ms = jnp.mean(o * o, axis=-1, keepdims=True)
            on = o * lax.rsqrt(ms + NORM_EPS) * nw
            o_ref[0, pl.ds(r0, c), sl] = (on * _silu(z)).astype(BF16)
        return carry

    lax.fori_loop(0, n_chunks, body, 0)


def gdn_rec(q, k, v, z, gcb, gcbT, norm_w, hb_n=4):
    b, t, hk = q.shape
    n_chunks = t // GDN_CHUNK
    gct4 = gcbT.reshape(b, 2 * GDN_HEADS, n_chunks, GDN_CHUNK).transpose(0, 2, 1, 3)
    wblk = hb_n * LANES
    spec = pl.BlockSpec((1, t, wblk), lambda bb, hh: (bb, 0, hh))
    return pl.pallas_call(
        functools.partial(_gdn_rec_kernel, hb_n=hb_n, n_chunks=n_chunks),
        grid=(b, GDN_HEADS // hb_n),
        in_specs=[spec, spec, spec, spec,
                  pl.BlockSpec((1, t, LANES), lambda bb, hh: (bb, 0, 0)),
                  pl.BlockSpec((1, n_chunks, 2 * GDN_HEADS, GDN_CHUNK), lambda bb, hh: (bb, 0, 0, 0)),
                  pl.BlockSpec((1, LANES), lambda bb, hh: (0, 0))],
        out_specs=spec,
        out_shape=jax.ShapeDtypeStruct((b, t, hk), BF16),
        scratch_shapes=[pltpu.VMEM((hb_n, GDN_HEAD_DIM, GDN_HEAD_DIM), F32)],
        compiler_params=_params(("arbitrary", "arbitrary")),
        name="gdn_rec",
    )(q, k, v, z, gcb, gct4, norm_w.reshape(1, LANES).astype(F32))


def _outproj_ln_kernel(o_ref, w_ref, h_ref, g_ref, b_ref, out_ref):
    y = _dot(o_ref[...], w_ref[...])
    out_ref[...] = _layer_norm(DEEPNORM_ALPHA * h_ref[...] + y, g_ref[...], b_ref[...])


def outproj_ln(o, w, h, ln_g, ln_b, tm=512):
    n, d = h.shape
    kdim = o.shape[1]
    row = lambda i: (i, 0)
    const = lambda i: (0, 0)
    return pl.pallas_call(
        _outproj_ln_kernel,
        grid=(n // tm,),
        in_specs=[pl.BlockSpec((tm, kdim), row), pl.BlockSpec((kdim, d), const), pl.BlockSpec((tm, d), row),
                  pl.BlockSpec((1, d), const), pl.BlockSpec((1, d), const)],
        out_specs=pl.BlockSpec((tm, d), row),
        out_shape=jax.ShapeDtypeStruct((n, d), F32),
        compiler_params=_params(("arbitrary",)),
        name="outproj_ln",
    )(o, w.astype(BF16), h, ln_g.reshape(1, d), ln_b.reshape(1, d))


GRP_LANE0 = MOE_EXPERTS


def _route(logits):
    lane = lax.broadcasted_iota(jnp.int32, logits.shape, 1)
    lanef = lane.astype(F32)
    far = float(LANES)
    is_grp = (lane >= GRP_LANE0) & (lane < GRP_LANE0 + MOE_GROUPS)
    lg = jnp.where(is_grp, logits, NEG)
    eg = jnp.exp(lg - jnp.max(lg, axis=-1, keepdims=True))
    pg = eg / jnp.sum(eg, axis=-1, keepdims=True)
    gp = jnp.max(pg, axis=-1, keepdims=True)
    gidx = jnp.min(jnp.where(is_grp & (pg == gp), lanef, far), axis=-1, keepdims=True) - float(GRP_LANE0)
    in_grp = (lane < MOE_EXPERTS) & (jnp.floor(lanef * (1.0 / MOE_EPG)) == gidx)
    le = jnp.where(in_grp, logits, NEG)
    ee = jnp.exp(le - jnp.max(le, axis=-1, keepdims=True))
    pe = ee / jnp.sum(ee, axis=-1, keepdims=True)
    p1 = jnp.max(jnp.where(in_grp, pe, -1.0), axis=-1, keepdims=True)
    i1 = jnp.min(jnp.where(in_grp & (pe == p1), lanef, far), axis=-1, keepdims=True)
    rest = in_grp & (lanef != i1)
    p2 = jnp.max(jnp.where(rest, pe, -1.0), axis=-1, keepdims=True)
    i2 = jnp.min(jnp.where(rest & (pe == p2), lanef, far), axis=-1, keepdims=True)
    scale = gp / (p1 + p2)
    return jnp.where(lanef == i1, p1 * scale, jnp.where(lanef == i2, p2 * scale, 0.0))


def _moe_ln_kernel(h_ref, wrh_ref, wrl_ref, br_ref, wgu_ref, wd_ref, g_ref, b_ref, out_ref,
                   acc_ref, gate_ref, xb_ref):
    e = pl.program_id(1)

    @pl.when(e == 0)
    def _():
        xb, xlo = _split2(h_ref[...])
        wh = wrh_ref[...]
        logits = _dot(xb, wh) + (_dot(xlo, wh) + _dot(xb, wrl_ref[...])) + br_ref[...]
        gate_ref[...] = _route(logits)
        xb_ref[...] = xb
        acc_ref[...] = jnp.zeros_like(acc_ref)

    xb = xb_ref[...]
    gate = gate_ref[...]
    lane = lax.broadcasted_iota(jnp.int32, gate.shape, 1)
    ge = jnp.sum(jnp.where(lane == e, gate, 0.0), axis=-1, keepdims=True)
    hg = _dot(xb, wgu_ref[0, :, :MOE_HIDDEN])
    hu = _dot(xb, wgu_ref[0, :, MOE_HIDDEN:])
    hid = (_silu(hg) * hu * ge).astype(BF16)
    acc_ref[...] += _dot(hid, wd_ref[0])

    @pl.when(e == MOE_EXPERTS - 1)
    def _():
        out_ref[...] = _layer_norm(DEEPNORM_ALPHA * h_ref[...] + acc_ref[...], g_ref[...], b_ref[...])


def moe_ln(h, w_grp, b_grp, w_rt, b_rt, w_gate, w_up, w_down, ln_g, ln_b, tm=512):
    n, d = h.shape
    wr = jnp.zeros((d, LANES), F32).at[:, :MOE_EXPERTS].set(w_rt).at[:, GRP_LANE0:GRP_LANE0 + MOE_GROUPS].set(w_grp)
    br = jnp.zeros((1, LANES), F32).at[0, :MOE_EXPERTS].set(b_rt).at[0, GRP_LANE0:GRP_LANE0 + MOE_GROUPS].set(b_grp)
    wrh = wr.astype(BF16)
    wrl = (wr - wrh.astype(F32)).astype(BF16)
    wgu = jnp.concatenate([w_gate, w_up], axis=-1).astype(BF16)
    wd = w_down.astype(BF16)
    row = lambda i, e: (i, 0)
    const = lambda i, e: (0, 0)
    return pl.pallas_call(
        _moe_ln_kernel,
        grid=(n // tm, MOE_EXPERTS),
        in_specs=[pl.BlockSpec((tm, d), row),
                  pl.BlockSpec((d, LANES), const), pl.BlockSpec((d, LANES), const), pl.BlockSpec((1, LANES), const),
                  pl.BlockSpec((1, d, 2 * MOE_HIDDEN), lambda i, e: (e, 0, 0)),
                  pl.BlockSpec((1, MOE_HIDDEN, d), lambda i, e: (e, 0, 0)),
                  pl.BlockSpec((1, d), const), pl.BlockSpec((1, d), const)],
        out_specs=pl.BlockSpec((tm, d), row),
        out_shape=jax.ShapeDtypeStruct((n, d), F32),
        scratch_shapes=[pltpu.VMEM((tm, d), F32), pltpu.VMEM((tm, LANES), F32), pltpu.VMEM((tm, d), BF16)],
        compiler_params=_params(("arbitrary", "arbitrary")),
        name="moe_ln",
    )(h, wrh, wrl, br, wgu, wd, ln_g.reshape(1, d), ln_b.reshape(1, d))


KV_W = NSA_GROUPS * NSA_HEAD_DIM
GATE_ROWS = 16


def _nsa_proj_kernel(h_ref, wq_ref, wg_ref, wc_ref, wks_ref, wkw_ref, wvs_ref, wvw_ref,
                     qT_ref, gT_ref, cmp_ref, ksa_ref, kwa_ref, vsT_ref, vwT_ref, *, tm):
    t0 = pl.program_id(1) * tm
    hb = h_ref[0].astype(BF16)
    qT_ref[0] = _dot_nt(wq_ref[...], hb).astype(BF16)
    gT_ref[0] = jax.nn.sigmoid(_dot_nt(wg_ref[...], hb))
    cmp_ref[0] = _dot(hb, wc_ref[...]).astype(BF16)
    vsT_ref[0] = _dot_nt(wvs_ref[...], hb).astype(BF16)
    vwT_ref[0] = _dot_nt(wvw_ref[...], hb).astype(BF16)
    ks = _dot(hb, wks_ref[...])
    kw = _dot(hb, wkw_ref[...])
    lane = lax.broadcasted_iota(jnp.int32, (tm, LANES), 1)
    blk = (t0 + lax.broadcasted_iota(jnp.int32, (tm, LANES), 0)) // SEL_BLOCK
    onehot = jnp.where(lane - NSA_HEAD_DIM == blk, 1.0, 0.0)
    for g in range(NSA_GROUPS):
        ksg = ks[:, g * LANES:(g + 1) * LANES]
        ksa_ref[0, g] = jnp.where(lane < NSA_HEAD_DIM, ksg, onehot).astype(BF16)
        kwa_ref[0, g] = kw[:, g * LANES:(g + 1) * LANES].astype(BF16)


def _pad_group_cols(w):
    d = w.shape[0]
    w4 = w.reshape(d, NSA_GROUPS, NSA_HEAD_DIM)
    return jnp.concatenate([w4, jnp.zeros_like(w4)], axis=-1).reshape(d, NSA_GROUPS * LANES)


def nsa_proj(h, w_kv, w_in, tm=256):
    b, t, d = h.shape
    scale = NSA_HEAD_DIM ** -0.5
    wq = (w_in[:, :NSA_WIDTH] * scale).T.astype(BF16)
    wgate = w_in[:, NSA_WIDTH:].reshape(d, NSA_GROUPS, NSA_HPG, N_BRANCHES)
    wgate = wgate.transpose(0, 1, 3, 2).reshape(d, NSA_GROUPS, N_BRANCHES * NSA_HPG)
    wgate = jnp.concatenate([wgate, jnp.zeros((d, NSA_GROUPS, GATE_ROWS - N_BRANCHES * NSA_HPG), F32)], axis=-1)
    wg = wgate.reshape(d, NSA_GROUPS * GATE_ROWS).T.astype(BF16)
    part = lambda p: w_kv[:, p * KV_W:(p + 1) * KV_W]
    wc = w_kv[:, :2 * KV_W].astype(BF16)
    wks = _pad_group_cols(part(2)).astype(BF16)
    wvs = part(3).T.astype(BF16)
    wkw = _pad_group_cols(part(4)).astype(BF16)
    wvw = part(5).T.astype(BF16)
    const = lambda bb, tt: (0, 0)
    tok = lambda bb, tt: (bb, tt, 0)
    tr = lambda bb, tt: (bb, 0, tt)
    g4 = lambda bb, tt: (bb, 0, tt, 0)
    ng = NSA_GROUPS * GATE_ROWS
    return pl.pallas_call(
        functools.partial(_nsa_proj_kernel, tm=tm),
        grid=(b, t // tm),
        in_specs=[pl.BlockSpec((1, tm, d), tok),
                  pl.BlockSpec((NSA_WIDTH, d), const), pl.BlockSpec((ng, d), const),
                  pl.BlockSpec((d, 2 * KV_W), const),
                  pl.BlockSpec((d, NSA_GROUPS * LANES), const), pl.BlockSpec((d, NSA_GROUPS * LANES), const),
                  pl.BlockSpec((KV_W, d), const), pl.BlockSpec((KV_W, d), const)],
        out_specs=[pl.BlockSpec((1, NSA_WIDTH, tm), tr), pl.BlockSpec((1, ng, tm), tr),
                   pl.BlockSpec((1, tm, 2 * KV_W), tok),
                   pl.BlockSpec((1, NSA_GROUPS, tm, LANES), g4), pl.BlockSpec((1, NSA_GROUPS, tm, LANES), g4),
                   pl.BlockSpec((1, KV_W, tm), tr), pl.BlockSpec((1, KV_W, tm), tr)],
        out_shape=[jax.ShapeDtypeStruct((b, NSA_WIDTH, t), BF16), jax.ShapeDtypeStruct((b, ng, t), F32),
                   jax.ShapeDtypeStruct((b, t, 2 * KV_W), BF16),
                   jax.ShapeDtypeStruct((b, NSA_GROUPS, t, LANES), BF16),
                   jax.ShapeDtypeStruct((b, NSA_GROUPS, t, LANES), BF16),
                   jax.ShapeDtypeStruct((b, KV_W, t), BF16), jax.ShapeDtypeStruct((b, KV_W, t), BF16)],
        compiler_params=_params(("arbitrary", "arbitrary")),
        name="nsa_proj",
    )(h, wq, wg, wc, wks, wkw, wvs, wvw)


N_CMP_PAD = 128


def _nsa_compress_kernel(xk_ref, xv_ref, pk_ref, pv_ref, w1k_ref, w1v_ref, w2k_ref, w2vT_ref, kc_ref, vcT_ref):
    half = CMP_STRIDE * NSA_HEAD_DIM
    pbk = _dot(pk_ref[...], w1k_ref[...])[0:1, :]
    pbv = _dot(pv_ref[...], w1v_ref[...])[0:1, :]
    for g in range(NSA_GROUPS):
        xk = xk_ref[0, g]
        hk = _dot(xk, w1k_ref[:half, :]) + pltpu.roll(_dot(xk, w1k_ref[half:, :]), N_CMP_PAD - 1, 0) + pbk
        kc_ref[0, g] = _dot(_silu(hk).astype(BF16), w2k_ref[...]).astype(BF16)
        xv = xv_ref[0, g]
        hv = _dot(xv, w1v_ref[:half, :]) + pltpu.roll(_dot(xv, w1v_ref[half:, :]), N_CMP_PAD - 1, 0) + pbv
        vcT_ref[0, g] = _dot_nt(w2vT_ref[...], _silu(hv).astype(BF16)).astype(BF16)


def nsa_compress(cmp_raw, k_pos, k_w1, k_w2, v_pos, v_w1, v_w2):
    b, t, _ = cmp_raw.shape
    nrow = t // CMP_STRIDE
    assert nrow == N_CMP_PAD
    wide = CMP_STRIDE * NSA_HEAD_DIM
    x5 = cmp_raw.reshape(b, nrow, CMP_STRIDE, 2, NSA_GROUPS, NSA_HEAD_DIM)
    x5 = x5.transpose(3, 0, 4, 1, 2, 5).reshape(2, b, NSA_GROUPS, nrow, wide)
    pad_pos = lambda p: jnp.zeros((8, CMP_BLOCK * NSA_HEAD_DIM), F32).at[0].set(p.reshape(-1)).astype(BF16)
    spec_x = pl.BlockSpec((1, NSA_GROUPS, nrow, wide), lambda bb: (bb, 0, 0, 0))
    c2 = lambda bb: (0, 0)
    return pl.pallas_call(
        _nsa_compress_kernel,
        grid=(b,),
        in_specs=[spec_x, spec_x,
                  pl.BlockSpec((8, 2 * wide), c2), pl.BlockSpec((8, 2 * wide), c2),
                  pl.BlockSpec((2 * wide, CMP_HIDDEN), c2), pl.BlockSpec((2 * wide, CMP_HIDDEN), c2),
                  pl.BlockSpec((CMP_HIDDEN, NSA_HEAD_DIM), c2), pl.BlockSpec((NSA_HEAD_DIM, CMP_HIDDEN), c2)],
        out_specs=[pl.BlockSpec((1, NSA_GROUPS, N_CMP_PAD, NSA_HEAD_DIM), lambda bb: (bb, 0, 0, 0)),
                   pl.BlockSpec((1, NSA_GROUPS, NSA_HEAD_DIM, N_CMP_PAD), lambda bb: (bb, 0, 0, 0))],
        out_shape=[jax.ShapeDtypeStruct((b, NSA_GROUPS, N_CMP_PAD, NSA_HEAD_DIM), BF16),
                   jax.ShapeDtypeStruct((b, NSA_GROUPS, NSA_HEAD_DIM, N_CMP_PAD), BF16)],
        compiler_params=_params(("arbitrary",)),
        name="nsa_compress",
    )(x5[0], x5[1], pad_pos(k_pos), pad_pos(v_pos), k_w1.astype(BF16), v_w1.astype(BF16),
      k_w2.astype(BF16), v_w2.T.astype(BF16))


SEL_CHUNK = 256
WIN_CHUNK = 128
TB_ROWS = WINDOW + SEL_CHUNK
TBC_ROWS = 256
TBC_OFF = 120
N_SEL_BLOCKS = 32
QW = NSA_HPG * Q_BLOCK


def _t5_bucket(dist):
    n = jnp.maximum(dist, 0)
    max_exact = REL_BUCKETS // 2
    nf = jnp.maximum(n, 1).astype(F32)
    large = max_exact + (jnp.log(nf / max_exact) / math.log(REL_MAX_DIST / max_exact)
                         * (REL_BUCKETS - max_exact)).astype(jnp.int32)
    large = jnp.minimum(large, REL_BUCKETS - 1)
    return jnp.where(n < max_exact, n, large)


def _bias_tables(rel_bias):
    rel = rel_bias.astype(F32)
    far = rel[REL_BUCKETS - 1]
    ql = jnp.arange(Q_BLOCK)
    heads = jnp.arange(NSA_HEADS).reshape(NSA_GROUPS, NSA_HPG)

    def table(dist, valid):
        bias = rel[_t5_bucket(dist)] - far
        bias = jnp.where(valid[..., None], bias, NEG)
        bias = bias[:, :, heads]
        return bias.transpose(2, 0, 3, 1).reshape(NSA_GROUPS, dist.shape[0], QW)

    d = ql[None, :] - (jnp.arange(TB_ROWS)[:, None] - WINDOW)
    tb = table(d, (d >= 0) & (d < WINDOW))
    m = jnp.arange(TBC_ROWS)[:, None] - TBC_OFF
    dc = ql[None, :] - CMP_STRIDE * m - (CMP_BLOCK - 1)
    tbc = table(dc, dc >= 0)
    return tb, tbc


def _nsa_attn_kernel(qT_ref, gT_ref, kc_ref, vcT_ref, ksa_ref, vsT_ref, kwa_ref, vwT_ref, tb_ref, tbc_ref, ovl_ref,
                     o_ref, imp_ref, m_ref, l_ref, acc_ref, oT_ref):
    c = pl.program_id(1)
    nsel = N_SEL_BLOCKS
    jidx = lax.broadcasted_iota(jnp.int32, (nsel, Q_BLOCK), 0)
    qlane = lax.broadcasted_iota(jnp.int32, (nsel, Q_BLOCK), 1)
    jq = 2 * c + (qlane >= SEL_BLOCK).astype(jnp.int32)
    nrow = lax.broadcasted_iota(jnp.int32, (N_CMP_PAD, QW), 0)
    ovl = ovl_ref[...]

    def reset():
        m_ref[...] = jnp.full(m_ref.shape, NEG, F32)
        l_ref[...] = jnp.zeros_like(l_ref)
        acc_ref[...] = jnp.zeros_like(acc_ref)

    def update(s, vT):
        m_old = m_ref[...]
        m_new = jnp.maximum(m_old, jnp.max(s, axis=0, keepdims=True))
        alpha = jnp.exp(m_old - m_new)
        p = jnp.exp(s - m_new)
        l_ref[...] = l_ref[...] * alpha + jnp.sum(p, axis=0, keepdims=True)
        acc_ref[...] = acc_ref[...] * alpha + _dot(vT, p.astype(BF16))
        m_ref[...] = m_new

    for g in range(NSA_GROUPS):
        qTg = jnp.concatenate(
            [qT_ref[0, (g * NSA_HPG + hh) * NSA_HEAD_DIM:(g * NSA_HPG + hh + 1) * NSA_HEAD_DIM, :]
             for hh in range(NSA_HPG)], axis=1)
        gates = gT_ref[0, g * GATE_ROWS:(g + 1) * GATE_ROWS, :]
        gate_row = lambda br: jnp.concatenate(
            [gates[br * NSA_HPG + hh:br * NSA_HPG + hh + 1, :] for hh in range(NSA_HPG)], axis=1)

        start = pl.multiple_of(TBC_OFF - 8 * c, 8)
        tbc = tbc_ref[g, pl.ds(start, N_CMP_PAD), :]
        tbc = jnp.where(nrow < N_CMP_PAD - 1, tbc, NEG)
        s = _dot(kc_ref[0, g], qTg) + tbc
        valid = tbc > 0.5 * NEG
        e = jnp.where(valid, jnp.exp(s - jnp.max(s, axis=0, keepdims=True)), 0.0)
        l = jnp.sum(e, axis=0, keepdims=True)
        p = e / jnp.where(l > 0.0, l, 1.0)
        o_t = gate_row(0) * _dot(vcT_ref[0, g], p.astype(BF16))

        psum = p[:, 0:Q_BLOCK]
        for hh in range(1, NSA_HPG):
            psum = psum + p[:, hh * Q_BLOCK:(hh + 1) * Q_BLOCK]
        ph, plo = _split2(psum)
        imp = _dot(ovl, ph) + _dot(ovl, plo)
        forced = (jidx == 0) | (jidx == jq) | (jidx == jq - 1)
        imp = jnp.where(forced, -NEG, jnp.where(jidx <= jq, imp, NEG))
        imp_ref[...] = imp

        def rank_body(i, cnt):
            r = imp_ref[pl.ds(i, 1), :]
            ahead = (r > imp) | ((r == imp) & (i < jidx))
            return cnt + jnp.where(ahead, 1.0, 0.0)

        cnt = lax.fori_loop(0, nsel, rank_body, jnp.zeros((nsel, Q_BLOCK), F32))
        selm = jnp.where(cnt < float(min(TOP_N, nsel)), 0.0, NEG).astype(BF16)
        q_aug = jnp.concatenate([qTg, jnp.concatenate([selm] * NSA_HPG, axis=1),
                                 jnp.zeros((LANES - NSA_HEAD_DIM - nsel, QW), BF16)], axis=0)

        reset()
        n_far = jnp.maximum(c // 2 - 1, 0)

        def far_body(i, carry):
            k0 = pl.multiple_of(i * SEL_CHUNK, SEL_CHUNK)
            update(_dot(ksa_ref[0, g, pl.ds(k0, SEL_CHUNK), :], q_aug), vsT_ref[0, pl.ds(g * NSA_HEAD_DIM, NSA_HEAD_DIM), pl.ds(k0, SEL_CHUNK)])
            return carry

        lax.fori_loop(0, n_far, far_body, 0)

        def near(i):
            k0 = pl.multiple_of(i * SEL_CHUNK, SEL_CHUNK)
            off = c * Q_BLOCK - k0
            tstart = pl.multiple_of(WINDOW - off, Q_BLOCK)
            s_n = _dot(ksa_ref[0, g, pl.ds(k0, SEL_CHUNK), :], q_aug) + tb_ref[g, pl.ds(tstart, SEL_CHUNK), :]
            update(s_n, vsT_ref[0, pl.ds(g * NSA_HEAD_DIM, NSA_HEAD_DIM), pl.ds(k0, SEL_CHUNK)])

        @pl.when(c >= 2)
        def _():
            near(c // 2 - 1)

        near(c // 2)
        o_t = o_t + gate_row(1) * (acc_ref[...] / l_ref[...])

        reset()
        for i in range(WINDOW // WIN_CHUNK + 1):
            def win(i=i):
                k0 = pl.multiple_of((c - WINDOW // WIN_CHUNK + i) * WIN_CHUNK, WIN_CHUNK)
                s_w = _dot(kwa_ref[0, g, pl.ds(k0, WIN_CHUNK), :], q_aug)
                if i in (0, 3, 4):
                    s_w = s_w + tb_ref[g, i * WIN_CHUNK:(i + 1) * WIN_CHUNK, :]
                update(s_w, vwT_ref[0, pl.ds(g * NSA_HEAD_DIM, NSA_HEAD_DIM), pl.ds(k0, WIN_CHUNK)])

            if i == WINDOW // WIN_CHUNK:
                win()
            else:
                pl.when(c - WINDOW // WIN_CHUNK + i >= 0)(win)
        o_t = o_t + gate_row(2) * (acc_ref[...] / l_ref[...])

        for hh in range(NSA_HPG):
            oT_ref[hh, g * NSA_HEAD_DIM:(g + 1) * NSA_HEAD_DIM, :] = o_t[:, hh * Q_BLOCK:(hh + 1) * Q_BLOCK]

    for hh in range(NSA_HPG):
        o_ref[0, :, hh * KV_W:(hh + 1) * KV_W] = oT_ref[hh].T.astype(BF16)


def nsa_attn(qT, gT, kc, vcT, ksa, vsT, kwa, vwT, rel_bias):
    b, _, t = qT.shape
    nq = t // Q_BLOCK
    tb, tbc = _bias_tables(rel_bias)
    nc = (t - CMP_BLOCK) // CMP_STRIDE + 1
    cs = jnp.arange(N_CMP_PAD) * CMP_STRIDE
    ss = jnp.arange(N_SEL_BLOCKS) * SEL_BLOCK
    ovl = ((cs[None, :] < ss[:, None] + SEL_BLOCK) & (cs[None, :] + CMP_BLOCK - 1 >= ss[:, None])
           & (jnp.arange(N_CMP_PAD)[None, :] < nc)).astype(BF16)
    ng = NSA_GROUPS * GATE_ROWS
    per_b3 = lambda bb, cc: (bb, 0, 0)
    per_b4 = lambda bb, cc: (bb, 0, 0, 0)
    c3 = lambda bb, cc: (0, 0, 0)
    return pl.pallas_call(
        _nsa_attn_kernel,
        grid=(b, nq),
        in_specs=[pl.BlockSpec((1, NSA_WIDTH, Q_BLOCK), lambda bb, cc: (bb, 0, cc)),
                  pl.BlockSpec((1, ng, Q_BLOCK), lambda bb, cc: (bb, 0, cc)),
                  pl.BlockSpec((1, NSA_GROUPS, N_CMP_PAD, NSA_HEAD_DIM), per_b4),
                  pl.BlockSpec((1, NSA_GROUPS, NSA_HEAD_DIM, N_CMP_PAD), per_b4),
                  pl.BlockSpec((1, NSA_GROUPS, t, LANES), per_b4),
                  pl.BlockSpec((1, KV_W, t), per_b3),
                  pl.BlockSpec((1, NSA_GROUPS, t, LANES), per_b4),
                  pl.BlockSpec((1, KV_W, t), per_b3),
                  pl.BlockSpec((NSA_GROUPS, TB_ROWS, QW), c3),
                  pl.BlockSpec((NSA_GROUPS, TBC_ROWS, QW), c3),
                  pl.BlockSpec((N_SEL_BLOCKS, N_CMP_PAD), lambda bb, cc: (0, 0))],
        out_specs=pl.BlockSpec((1, Q_BLOCK, NSA_WIDTH), lambda bb, cc: (bb, cc, 0)),
        out_shape=jax.ShapeDtypeStruct((b, t, NSA_WIDTH), BF16),
        scratch_shapes=[pltpu.VMEM((N_SEL_BLOCKS, Q_BLOCK), F32),
                        pltpu.VMEM((1, QW), F32), pltpu.VMEM((1, QW), F32), pltpu.VMEM((NSA_HEAD_DIM, QW), F32),
                        pltpu.VMEM((NSA_HPG, KV_W, Q_BLOCK), F32)],
        compiler_params=_params(("arbitrary", "arbitrary")),
        name="nsa_attn",
    )(qT, gT, kc, vcT, ksa, vsT, kwa, vwT, tb, tbc, ovl)


def kernel(x, gdn_w_in, gdn_conv_w, gdn_a_log, gdn_dt_bias, gdn_norm_w, gdn_w_o, nsa_w_kv, cmp_k_pos, cmp_k_w1,
           cmp_k_w2, cmp_v_pos, cmp_v_w1, cmp_v_w2, nsa_w_in, nsa_w_o, rel_bias, ln_mix_g, ln_mix_b, ln_ffn_g,
           ln_ffn_b, moe_w_grp, moe_b_grp, moe_w_rt, moe_b_rt, moe_w_gate, moe_w_up, moe_w_down):
    b, t, d = x.shape
    n = b * t

    def ffn(h, layer):
        return moe_ln(h, moe_w_grp[layer], moe_b_grp[layer], moe_w_rt[layer], moe_b_rt[layer], moe_w_gate[layer],
                      moe_w_up[layer], moe_w_down[layer], ln_ffn_g[layer], ln_ffn_b[layer])

    q, k, v, z, gcb, gcbT = gdn_inproj(x, gdn_w_in[0], gdn_conv_w[0], gdn_a_log[0], gdn_dt_bias[0])
    o = gdn_rec(q, k, v, z, gcb, gcbT, gdn_norm_w[0])
    h = outproj_ln(o.reshape(n, GDN_WIDTH), gdn_w_o[0], x.reshape(n, d), ln_mix_g[0], ln_mix_b[0])
    h = ffn(h, 0)

    qT, gT, cmp_raw, ksa, kwa, vsT, vwT = nsa_proj(h.reshape(b, t, d), nsa_w_kv, nsa_w_in[0])
    kc, vcT = nsa_compress(cmp_raw, cmp_k_pos, cmp_k_w1, cmp_k_w2, cmp_v_pos, cmp_v_w1, cmp_v_w2)
    o = nsa_attn(qT, gT, kc, vcT, ksa, vsT, kwa, vwT, rel_bias)
    w_o = nsa_w_o[0].reshape(NSA_GROUPS, NSA_HPG, NSA_HEAD_DIM, d).transpose(1, 0, 2, 3).reshape(NSA_WIDTH, d)
    h = outproj_ln(o.reshape(n, NSA_WIDTH), w_o, h, ln_mix_g[1], ln_mix_b[1])
    h = ffn(h, 1)
    return h.reshape(b, t, d)
```

```python
import functools
import math

import jax
import jax.numpy as jnp
from jax import lax
from jax.experimental import pallas as pl
from jax.experimental.pallas import tpu as pltpu

F32 = jnp.float32
BF16 = jnp.bfloat16

D_MODEL = 1024
DEPTH = 2
GDN_HEADS = 8
GDN_HEAD_DIM = 128
GDN_WIDTH = GDN_HEADS * GDN_HEAD_DIM
GDN_CONV = 4
GDN_CHUNK = 64
NSA_HEADS = 16
NSA_GROUPS = 4
NSA_HPG = NSA_HEADS // NSA_GROUPS
NSA_HEAD_DIM = 64
NSA_WIDTH = NSA_HEADS * NSA_HEAD_DIM
CMP_BLOCK = 32
CMP_STRIDE = 16
CMP_HIDDEN = 2 * NSA_HEAD_DIM
SEL_BLOCK = 64
TOP_N = 16
WINDOW = 512
Q_BLOCK = 128
N_BRANCHES = 3
REL_BUCKETS = 32
REL_MAX_DIST = 128
MOE_GROUPS = 4
MOE_EPG = 4
MOE_EXPERTS = MOE_GROUPS * MOE_EPG
MOE_HIDDEN = 256
DEEPNORM_ALPHA = (2 * DEPTH) ** 0.25
LN_EPS = 1e-5
NORM_EPS = 1e-6
NEG = -1e30

LANES = 128
VMEM_LIMIT = 56 * 1024 * 1024


def _dot(a, b):
    return jnp.dot(a, b, preferred_element_type=F32)


def _dot_nt(a, b):
    return lax.dot_general(a, b, (((1,), (1,)), ((), ())), preferred_element_type=F32)


def _dot_tn(a, b):
    return lax.dot_general(a, b, (((0,), (0,)), ((), ())), preferred_element_type=F32)


def _split2(x):
    hi = x.astype(BF16)
    lo = (x - hi.astype(F32)).astype(BF16)
    return hi, lo


def _silu(x):
    return x * jax.nn.sigmoid(x)


def _params(sem):
    return pltpu.CompilerParams(dimension_semantics=sem, vmem_limit_bytes=VMEM_LIMIT)


def _layer_norm(x, g, b):
    mu = jnp.mean(x, axis=-1, keepdims=True)
    xc = x - mu
    var = jnp.mean(xc * xc, axis=-1, keepdims=True)
    return xc * lax.rsqrt(var + LN_EPS) * g + b


def _gdn_inproj_kernel(x_ref, w_ref, wabh_ref, wabl_ref, cw_ref, alog_ref, dtb_ref, ltri_ref,
                       q_ref, k_ref, v_ref, z_ref, gcb_ref, gcbT_ref, carry_ref, *, tm):
    @pl.when(pl.program_id(1) == 0)
    def _():
        carry_ref[...] = jnp.zeros_like(carry_ref)

    x = x_ref[0]
    xb, xlo = _split2(x)

    wh = wabh_ref[...]
    ab = _dot(xb, wh) + (_dot(xlo, wh) + _dot(xb, wabl_ref[...]))
    lane = lax.broadcasted_iota(jnp.int32, ab.shape, 1)
    sp_in = ab + dtb_ref[...]
    softplus = jnp.maximum(sp_in, 0.0) + jnp.log1p(jnp.exp(-jnp.abs(sp_in)))
    g = jnp.where(lane < GDN_HEADS, -jnp.exp(alog_ref[...]) * softplus, 0.0)
    beta = jax.nn.sigmoid(ab)
    g1 = g.astype(BF16)
    r1 = g - g1.astype(F32)
    g2 = r1.astype(BF16)
    g3 = (r1 - g2.astype(F32)).astype(BF16)
    ltri = ltri_ref[...]
    gc = _dot(ltri, g1) + (_dot(ltri, g2) + _dot(ltri, g3))
    gcb = jnp.where(lane < GDN_HEADS, gc, jnp.where(lane < 2 * GDN_HEADS, beta, 0.0))
    gcb_ref[0] = gcb
    gcbT_ref[0] = gcb.T[:2 * GDN_HEADS, :]

    row8 = lax.broadcasted_iota(jnp.int32, (8, 256), 0)
    outs = (q_ref, k_ref, v_ref)
    for s in range(3):
        for cc in range(4):
            col = s * GDN_WIDTH + cc * 256
            y = _dot(xb, w_ref[:, col:col + 256])
            prev = carry_ref[s * 4 + cc]
            carry_ref[s * 4 + cc] = y[tm - 8:, :]
            cw = cw_ref[:, col:col + 256]
            acc = y * cw[3:4, :]
            for kk in range(1, GDN_CONV):
                ry = pltpu.roll(y, kk, 0)
                rp = pltpu.roll(prev, kk, 0)
                head = jnp.where(row8 < kk, rp, ry[:8, :])
                shifted = jnp.concatenate([head, ry[8:, :]], axis=0)
                acc = acc + shifted * cw[3 - kk:4 - kk, :]
            a = _silu(acc)
            if s < 2:
                halves = []
                for hh in range(2):
                    ah = a[:, hh * LANES:(hh + 1) * LANES]
                    ss = jnp.sum(ah * ah, axis=-1, keepdims=True)
                    scale = lax.rsqrt(ss + NORM_EPS)
                    if s == 0:
                        scale = scale * (GDN_HEAD_DIM ** -0.5)
                    halves.append(ah * scale)
                a = jnp.concatenate(halves, axis=1)
            outs[s][0, :, cc * 256:(cc + 1) * 256] = a.astype(BF16)
    for cc in range(4):
        col = 3 * GDN_WIDTH + cc * 256
        z_ref[0, :, cc * 256:(cc + 1) * 256] = _dot(xb, w_ref[:, col:col + 256]).astype(BF16)


def gdn_inproj(x, w_in, conv_w, a_log, dt_bias, tm=256):
    b, t, d = x.shape
    hk = GDN_WIDTH
    w_main = w_in[:, :4 * hk].astype(BF16)
    w_ab = jnp.zeros((d, LANES), F32).at[:, :2 * GDN_HEADS].set(w_in[:, 4 * hk:])
    wabh = w_ab.astype(BF16)
    wabl = (w_ab - wabh.astype(F32)).astype(BF16)
    alog = jnp.zeros((1, LANES), F32).at[0, :GDN_HEADS].set(a_log)
    dtb = jnp.zeros((1, LANES), F32).at[0, :GDN_HEADS].set(dt_bias)
    r = jnp.arange(tm)
    ltri = ((r[:, None] // GDN_CHUNK == r[None, :] // GDN_CHUNK) & (r[:, None] >= r[None, :])).astype(BF16)
    tok = lambda bb, tt: (bb, tt, 0)
    const2 = lambda bb, tt: (0, 0)
    act = jax.ShapeDtypeStruct((b, t, hk), BF16)
    return pl.pallas_call(
        functools.partial(_gdn_inproj_kernel, tm=tm),
        grid=(b, t // tm),
        in_specs=[
            pl.BlockSpec((1, tm, d), tok),
            pl.BlockSpec((d, 4 * hk), const2),
            pl.BlockSpec((d, LANES), const2),
            pl.BlockSpec((d, LANES), const2),
            pl.BlockSpec((GDN_CONV, 3 * hk), const2),
            pl.BlockSpec((1, LANES), const2),
            pl.BlockSpec((1, LANES), const2),
            pl.BlockSpec((tm, tm), const2),
        ],
        out_specs=[
            pl.BlockSpec((1, tm, hk), tok),
            pl.BlockSpec((1, tm, hk), tok),
            pl.BlockSpec((1, tm, hk), tok),
            pl.BlockSpec((1, tm, hk), tok),
            pl.BlockSpec((1, tm, LANES), tok),
            pl.BlockSpec((1, 2 * GDN_HEADS, tm), lambda bb, tt: (bb, 0, tt)),
        ],
        out_shape=[act, act, act, act,
                   jax.ShapeDtypeStruct((b, t, LANES), F32),
                   jax.ShapeDtypeStruct((b, 2 * GDN_HEADS, t), F32)],
        scratch_shapes=[pltpu.VMEM((12, 8, 256), F32)],
        compiler_params=_params(("arbitrary", "arbitrary")),
        name="gdn_inproj",
    )(x, w_main, wabh, wabl, conv_w, alog, dtb, ltri)


def _bmm(a, b):
    return lax.dot_general(a, b, (((2,), (1,)), ((0,), (0,))), preferred_element_type=F32)


def _bmm_nt(a, b):
    return lax.dot_general(a, b, (((2,), (2,)), ((0,), (0,))), preferred_element_type=F32)


def _gdn_rec_kernel(q_ref, k_ref, v_ref, z_ref, gcb_ref, gcT_ref, nw_ref, o_ref, s_ref, *, nc):
    c = GDN_CHUNK
    nh = GDN_HEADS

    @pl.when(pl.program_id(1) == 0)
    def _():
        s_ref[...] = jnp.zeros_like(s_ref)

    row = lax.broadcasted_iota(jnp.int32, (nc, c, c), 1)
    col = lax.broadcasted_iota(jnp.int32, (nc, c, c), 2)
    eye = (row == col).astype(F32)
    nw = nw_ref[...]
    gcb = gcb_ref[0]
    gct = gcT_ref[0]

    u_h, wq_h, qkkd_h, dlast_h = [], [], [], []
    for h in range(nh):
        sl = slice(h * LANES, (h + 1) * LANES)
        q = q_ref[0, :, sl].reshape(nc, c, LANES)
        k = k_ref[0, :, sl].reshape(nc, c, LANES)
        v = v_ref[0, :, sl].astype(F32).reshape(nc, c, LANES)
        gc = gcb[:, h:h + 1].reshape(nc, c, 1)
        beta = gcb[:, nh + h:nh + h + 1].reshape(nc, c, 1)
        gc_row = gct[:, h:h + 1, :]
        g_last = gc[:, c - 1:c, :]
        decay = jnp.exp(jnp.where(row >= col, gc - gc_row, NEG))
        eg = jnp.exp(gc)
        kf = k.astype(F32)
        kb = kf * beta
        aq = _bmm_nt(jnp.concatenate([kb.astype(BF16), q], axis=1), k)
        a = jnp.where(row > col, aq[:, :c] * decay, 0.0)
        qk = aq[:, c:] * decay
        qpow = -a
        usum = eye + qpow
        qb = qpow.astype(BF16)
        qpow = _bmm(qb, qb)
        for lvl in range(5):
            qb = qpow.astype(BF16)
            if lvl < 4:
                prod = _bmm(jnp.concatenate([usum.astype(BF16), qb], axis=1), qb)
                usum = usum + prod[:, :c]
                qpow = prod[:, c:]
            else:
                usum = usum + _bmm(usum.astype(BF16), qb)
        rhs = jnp.concatenate([v * beta, kb * eg], axis=2).astype(BF16)
        uw = _bmm(usum.astype(BF16), rhs)
        u_h.append(uw[:, :, :LANES])
        wq_h.append(jnp.concatenate([uw[:, :, LANES:], q.astype(F32) * eg], axis=1).astype(BF16))
        k_dec = kf * jnp.exp(g_last - gc)
        k_dec_t = jnp.swapaxes(k_dec, 1, 2)
        qkkd_h.append(jnp.concatenate([qk, k_dec_t], axis=1).astype(BF16))
        dlast_h.append(jnp.exp(g_last))

    for i in range(nc):
        s_old = s_ref[...]
        lhs = jnp.stack([wq_h[h][i] for h in range(nh)])
        r = _bmm(lhs, s_old.astype(BF16))
        u = jnp.stack([u_h[h][i] for h in range(nh)])
        v_new = (u - r[:, :c]).astype(BF16)
        r2 = _bmm(jnp.stack([qkkd_h[h][i] for h in range(nh)]), v_new)
        o = r[:, c:] + r2[:, :c]
        dl = jnp.stack([dlast_h[h][i] for h in range(nh)])
        s_ref[...] = s_old * dl + r2[:, c:]
        ms = jnp.mean(o * o, axis=-1, keepdims=True)
        on = o * lax.rsqrt(ms + NORM_EPS) * nw
        for h in range(nh):
            z = z_ref[0, i * c:(i + 1) * c, h * LANES:(h + 1) * LANES].astype(F32)
            o_ref[0, i * c:(i + 1) * c, h * LANES:(h + 1) * LANES] = (on[h] * _silu(z)).astype(BF16)


def gdn_rec(q, k, v, z, gcb, gcbT, norm_w, nc=8):
    b, t, hk = q.shape
    n_chunks = t // GDN_CHUNK
    tb = nc * GDN_CHUNK
    gct4 = gcbT.reshape(b, 2 * GDN_HEADS, n_chunks, GDN_CHUNK).transpose(0, 2, 1, 3)
    spec = pl.BlockSpec((1, tb, hk), lambda bb, tt: (bb, tt, 0))
    return pl.pallas_call(
        functools.partial(_gdn_rec_kernel, nc=nc),
        grid=(b, t // tb),
        in_specs=[spec, spec, spec, spec,
                  pl.BlockSpec((1, tb, LANES), lambda bb, tt: (bb, tt, 0)),
                  pl.BlockSpec((1, nc, 2 * GDN_HEADS, GDN_CHUNK), lambda bb, tt: (bb, tt, 0, 0)),
                  pl.BlockSpec((1, LANES), lambda bb, tt: (0, 0))],
        out_specs=spec,
        out_shape=jax.ShapeDtypeStruct((b, t, hk), BF16),
        scratch_shapes=[pltpu.VMEM((GDN_HEADS, GDN_HEAD_DIM, GDN_HEAD_DIM), F32)],
        compiler_params=_params(("arbitrary", "arbitrary")),
        name="gdn_rec",
    )(q, k, v, z, gcb, gct4, norm_w.reshape(1, LANES).astype(F32))


def _outproj_ln_kernel(o_ref, w_ref, h_ref, g_ref, b_ref, out_ref):
    y = _dot(o_ref[...], w_ref[...])
    out_ref[...] = _layer_norm(DEEPNORM_ALPHA * h_ref[...] + y, g_ref[...], b_ref[...])


def outproj_ln(o, w, h, ln_g, ln_b, tm=512):
    n, d = h.shape
    kdim = o.shape[1]
    row = lambda i: (i, 0)
    const = lambda i: (0, 0)
    return pl.pallas_call(
        _outproj_ln_kernel,
        grid=(n // tm,),
        in_specs=[pl.BlockSpec((tm, kdim), row), pl.BlockSpec((kdim, d), const), pl.BlockSpec((tm, d), row),
                  pl.BlockSpec((1, d), const), pl.BlockSpec((1, d), const)],
        out_specs=pl.BlockSpec((tm, d), row),
        out_shape=jax.ShapeDtypeStruct((n, d), F32),
        compiler_params=_params(("arbitrary",)),
        name="outproj_ln",
    )(o, w.astype(BF16), h, ln_g.reshape(1, d), ln_b.reshape(1, d))


GRP_LANE0 = MOE_EXPERTS


def _route(logits):
    lane = lax.broadcasted_iota(jnp.int32, logits.shape, 1)
    lanef = lane.astype(F32)
    far = float(LANES)
    is_grp = (lane >= GRP_LANE0) & (lane < GRP_LANE0 + MOE_GROUPS)
    lg = jnp.where(is_grp, logits, NEG)
    eg = jnp.exp(lg - jnp.max(lg, axis=-1, keepdims=True))
    pg = eg / jnp.sum(eg, axis=-1, keepdims=True)
    gp = jnp.max(pg, axis=-1, keepdims=True)
    gidx = jnp.min(jnp.where(is_grp & (pg == gp), lanef, far), axis=-1, keepdims=True) - float(GRP_LANE0)
    in_grp = (lane < MOE_EXPERTS) & (jnp.floor(lanef * (1.0 / MOE_EPG)) == gidx)
    le = jnp.where(in_grp, logits, NEG)
    ee = jnp.exp(le - jnp.max(le, axis=-1, keepdims=True))
    pe = ee / jnp.sum(ee, axis=-1, keepdims=True)
    p1 = jnp.max(jnp.where(in_grp, pe, -1.0), axis=-1, keepdims=True)
    i1 = jnp.min(jnp.where(in_grp & (pe == p1), lanef, far), axis=-1, keepdims=True)
    rest = in_grp & (lanef != i1)
    p2 = jnp.max(jnp.where(rest, pe, -1.0), axis=-1, keepdims=True)
    i2 = jnp.min(jnp.where(rest & (pe == p2), lanef, far), axis=-1, keepdims=True)
    scale = gp / (p1 + p2)
    return jnp.where(lanef == i1, p1 * scale, jnp.where(lanef == i2, p2 * scale, 0.0))


def _moe_ln_kernel(h_ref, wrh_ref, wrl_ref, br_ref, wgu_ref, wd_ref, g_ref, b_ref, out_ref,
                   acc_ref, gate_ref, xb_ref):
    e = pl.program_id(1)

    @pl.when(e == 0)
    def _():
        xb, xlo = _split2(h_ref[...])
        wh = wrh_ref[...]
        logits = _dot(xb, wh) + (_dot(xlo, wh) + _dot(xb, wrl_ref[...])) + br_ref[...]
        gate_ref[...] = _route(logits)
        xb_ref[...] = xb
        acc_ref[...] = jnp.zeros_like(acc_ref)

    xb = xb_ref[...]
    gate = gate_ref[...]
    lane = lax.broadcasted_iota(jnp.int32, gate.shape, 1)
    ge = jnp.sum(jnp.where(lane == e, gate, 0.0), axis=-1, keepdims=True)
    hg = _dot(xb, wgu_ref[0, :, :MOE_HIDDEN])
    hu = _dot(xb, wgu_ref[0, :, MOE_HIDDEN:])
    hid = (_silu(hg) * hu * ge).astype(BF16)
    acc_ref[...] += _dot(hid, wd_ref[0])

    @pl.when(e == MOE_EXPERTS - 1)
    def _():
        out_ref[...] = _layer_norm(DEEPNORM_ALPHA * h_ref[...] + acc_ref[...], g_ref[...], b_ref[...])


def moe_ln(h, w_grp, b_grp, w_rt, b_rt, w_gate, w_up, w_down, ln_g, ln_b, tm=512):
    n, d = h.shape
    wr = jnp.zeros((d, LANES), F32).at[:, :MOE_EXPERTS].set(w_rt).at[:, GRP_LANE0:GRP_LANE0 + MOE_GROUPS].set(w_grp)
    br = jnp.zeros((1, LANES), F32).at[0, :MOE_EXPERTS].set(b_rt).at[0, GRP_LANE0:GRP_LANE0 + MOE_GROUPS].set(b_grp)
    wrh = wr.astype(BF16)
    wrl = (wr - wrh.astype(F32)).astype(BF16)
    wgu = jnp.concatenate([w_gate, w_up], axis=-1).astype(BF16)
    wd = w_down.astype(BF16)
    row = lambda i, e: (i, 0)
    const = lambda i, e: (0, 0)
    return pl.pallas_call(
        _moe_ln_kernel,
        grid=(n // tm, MOE_EXPERTS),
        in_specs=[pl.BlockSpec((tm, d), row),
                  pl.BlockSpec((d, LANES), const), pl.BlockSpec((d, LANES), const), pl.BlockSpec((1, LANES), const),
                  pl.BlockSpec((1, d, 2 * MOE_HIDDEN), lambda i, e: (e, 0, 0)),
                  pl.BlockSpec((1, MOE_HIDDEN, d), lambda i, e: (e, 0, 0)),
                  pl.BlockSpec((1, d), const), pl.BlockSpec((1, d), const)],
        out_specs=pl.BlockSpec((tm, d), row),
        out_shape=jax.ShapeDtypeStruct((n, d), F32),
        scratch_shapes=[pltpu.VMEM((tm, d), F32), pltpu.VMEM((tm, LANES), F32), pltpu.VMEM((tm, d), BF16)],
        compiler_params=_params(("arbitrary", "arbitrary")),
        name="moe_ln",
    )(h, wrh, wrl, br, wgu, wd, ln_g.reshape(1, d), ln_b.reshape(1, d))


KV_W = NSA_GROUPS * NSA_HEAD_DIM
GATE_ROWS = 16


def _nsa_proj_kernel(h_ref, wq_ref, wg_ref, wc_ref, wks_ref, wkw_ref, wvs_ref, wvw_ref,
                     qT_ref, gT_ref, cmp_ref, ksa_ref, kwa_ref, vsT_ref, vwT_ref, *, tm):
    t0 = pl.program_id(1) * tm
    hb = h_ref[0].astype(BF16)
    qT_ref[0] = _dot_nt(wq_ref[...], hb).astype(BF16)
    gT_ref[0] = jax.nn.sigmoid(_dot_nt(wg_ref[...], hb))
    cmp_ref[0] = _dot(hb, wc_ref[...]).astype(BF16)
    vsT_ref[0] = _dot_nt(wvs_ref[...], hb).astype(BF16)
    vwT_ref[0] = _dot_nt(wvw_ref[...], hb).astype(BF16)
    ks = _dot(hb, wks_ref[...])
    kw = _dot(hb, wkw_ref[...])
    lane = lax.broadcasted_iota(jnp.int32, (tm, LANES), 1)
    blk = (t0 + lax.broadcasted_iota(jnp.int32, (tm, LANES), 0)) // SEL_BLOCK
    onehot = jnp.where(lane - NSA_HEAD_DIM == blk, 1.0, 0.0)
    for g in range(NSA_GROUPS):
        ksg = ks[:, g * LANES:(g + 1) * LANES]
        ksa_ref[0, g] = jnp.where(lane < NSA_HEAD_DIM, ksg, onehot).astype(BF16)
        kwa_ref[0, g] = kw[:, g * LANES:(g + 1) * LANES].astype(BF16)


def _pad_group_cols(w):
    d = w.shape[0]
    w4 = w.reshape(d, NSA_GROUPS, NSA_HEAD_DIM)
    return jnp.concatenate([w4, jnp.zeros_like(w4)], axis=-1).reshape(d, NSA_GROUPS * LANES)


def nsa_proj(h, w_kv, w_in, tm=256):
    b, t, d = h.shape
    scale = NSA_HEAD_DIM ** -0.5
    wq = (w_in[:, :NSA_WIDTH] * scale).T.astype(BF16)
    wgate = w_in[:, NSA_WIDTH:].reshape(d, NSA_GROUPS, NSA_HPG, N_BRANCHES)
    wgate = wgate.transpose(0, 1, 3, 2).reshape(d, NSA_GROUPS, N_BRANCHES * NSA_HPG)
    wgate = jnp.concatenate([wgate, jnp.zeros((d, NSA_GROUPS, GATE_ROWS - N_BRANCHES * NSA_HPG), F32)], axis=-1)
    wg = wgate.reshape(d, NSA_GROUPS * GATE_ROWS).T.astype(BF16)
    part = lambda p: w_kv[:, p * KV_W:(p + 1) * KV_W]
    wc = w_kv[:, :2 * KV_W].astype(BF16)
    wks = _pad_group_cols(part(2)).astype(BF16)
    wvs = part(3).T.astype(BF16)
    wkw = _pad_group_cols(part(4)).astype(BF16)
    wvw = part(5).T.astype(BF16)
    const = lambda bb, tt: (0, 0)
    tok = lambda bb, tt: (bb, tt, 0)
    tr = lambda bb, tt: (bb, 0, tt)
    g4 = lambda bb, tt: (bb, 0, tt, 0)
    ng = NSA_GROUPS * GATE_ROWS
    return pl.pallas_call(
        functools.partial(_nsa_proj_kernel, tm=tm),
        grid=(b, t // tm),
        in_specs=[pl.BlockSpec((1, tm, d), tok),
                  pl.BlockSpec((NSA_WIDTH, d), const), pl.BlockSpec((ng, d), const),
                  pl.BlockSpec((d, 2 * KV_W), const),
                  pl.BlockSpec((d, NSA_GROUPS * LANES), const), pl.BlockSpec((d, NSA_GROUPS * LANES), const),
                  pl.BlockSpec((KV_W, d), const), pl.BlockSpec((KV_W, d), const)],
        out_specs=[pl.BlockSpec((1, NSA_WIDTH, tm), tr), pl.BlockSpec((1, ng, tm), tr),
                   pl.BlockSpec((1, tm, 2 * KV_W), tok),
                   pl.BlockSpec((1, NSA_GROUPS, tm, LANES), g4), pl.BlockSpec((1, NSA_GROUPS, tm, LANES), g4),
                   pl.BlockSpec((1, KV_W, tm), tr), pl.BlockSpec((1, KV_W, tm), tr)],
        out_shape=[jax.ShapeDtypeStruct((b, NSA_WIDTH, t), BF16), jax.ShapeDtypeStruct((b, ng, t), F32),
                   jax.ShapeDtypeStruct((b, t, 2 * KV_W), BF16),
                   jax.ShapeDtypeStruct((b, NSA_GROUPS, t, LANES), BF16),
                   jax.ShapeDtypeStruct((b, NSA_GROUPS, t, LANES), BF16),
                   jax.ShapeDtypeStruct((b, KV_W, t), BF16), jax.ShapeDtypeStruct((b, KV_W, t), BF16)],
        compiler_params=_params(("arbitrary", "arbitrary")),
        name="nsa_proj",
    )(h, wq, wg, wc, wks, wkw, wvs, wvw)


N_CMP_PAD = 128


def _nsa_compress_kernel(xk_ref, xv_ref, pk_ref, pv_ref, w1k_ref, w1v_ref, w2k_ref, w2vT_ref, kc_ref, vcT_ref):
    half = CMP_STRIDE * NSA_HEAD_DIM
    pbk = _dot(pk_ref[...], w1k_ref[...])[0:1, :]
    pbv = _dot(pv_ref[...], w1v_ref[...])[0:1, :]
    for g in range(NSA_GROUPS):
        xk = xk_ref[0, g]
        hk = _dot(xk, w1k_ref[:half, :]) + pltpu.roll(_dot(xk, w1k_ref[half:, :]), N_CMP_PAD - 1, 0) + pbk
        kc_ref[0, g] = _dot(_silu(hk).astype(BF16), w2k_ref[...]).astype(BF16)
        xv = xv_ref[0, g]
        hv = _dot(xv, w1v_ref[:half, :]) + pltpu.roll(_dot(xv, w1v_ref[half:, :]), N_CMP_PAD - 1, 0) + pbv
        vcT_ref[0, g] = _dot_nt(w2vT_ref[...], _silu(hv).astype(BF16)).astype(BF16)


def nsa_compress(cmp_raw, k_pos, k_w1, k_w2, v_pos, v_w1, v_w2):
    b, t, _ = cmp_raw.shape
    nrow = t // CMP_STRIDE
    assert nrow == N_CMP_PAD
    wide = CMP_STRIDE * NSA_HEAD_DIM
    x5 = cmp_raw.reshape(b, nrow, CMP_STRIDE, 2, NSA_GROUPS, NSA_HEAD_DIM)
    x5 = x5.transpose(3, 0, 4, 1, 2, 5).reshape(2, b, NSA_GROUPS, nrow, wide)
    pad_pos = lambda p: jnp.zeros((8, CMP_BLOCK * NSA_HEAD_DIM), F32).at[0].set(p.reshape(-1)).astype(BF16)
    spec_x = pl.BlockSpec((1, NSA_GROUPS, nrow, wide), lambda bb: (bb, 0, 0, 0))
    c2 = lambda bb: (0, 0)
    return pl.pallas_call(
        _nsa_compress_kernel,
        grid=(b,),
        in_specs=[spec_x, spec_x,
                  pl.BlockSpec((8, 2 * wide), c2), pl.BlockSpec((8, 2 * wide), c2),
                  pl.BlockSpec((2 * wide, CMP_HIDDEN), c2), pl.BlockSpec((2 * wide, CMP_HIDDEN), c2),
                  pl.BlockSpec((CMP_HIDDEN, NSA_HEAD_DIM), c2), pl.BlockSpec((NSA_HEAD_DIM, CMP_HIDDEN), c2)],
        out_specs=[pl.BlockSpec((1, NSA_GROUPS, N_CMP_PAD, NSA_HEAD_DIM), lambda bb: (bb, 0, 0, 0)),
                   pl.BlockSpec((1, NSA_GROUPS, NSA_HEAD_DIM, N_CMP_PAD), lambda bb: (bb, 0, 0, 0))],
        out_shape=[jax.ShapeDtypeStruct((b, NSA_GROUPS, N_CMP_PAD, NSA_HEAD_DIM), BF16),
                   jax.ShapeDtypeStruct((b, NSA_GROUPS, NSA_HEAD_DIM, N_CMP_PAD), BF16)],
        compiler_params=_params(("arbitrary",)),
        name="nsa_compress",
    )(x5[0], x5[1], pad_pos(k_pos), pad_pos(v_pos), k_w1.astype(BF16), v_w1.astype(BF16),
      k_w2.astype(BF16), v_w2.T.astype(BF16))


SEL_CHUNK = 256
WIN_CHUNK = 128
TB_ROWS = WINDOW + SEL_CHUNK
TBC_ROWS = 256
TBC_OFF = 120
N_SEL_BLOCKS = 32
QW = NSA_HPG * Q_BLOCK


def _t5_bucket(dist):
    n = jnp.maximum(dist, 0)
    max_exact = REL_BUCKETS // 2
    nf = jnp.maximum(n, 1).astype(F32)
    large = max_exact + (jnp.log(nf / max_exact) / math.log(REL_MAX_DIST / max_exact)
                         * (REL_BUCKETS - max_exact)).astype(jnp.int32)
    large = jnp.minimum(large, REL_BUCKETS - 1)
    return jnp.where(n < max_exact, n, large)


def _bias_tables(rel_bias):
    rel = rel_bias.astype(F32)
    far = rel[REL_BUCKETS - 1]
    ql = jnp.arange(Q_BLOCK)
    heads = jnp.arange(NSA_HEADS).reshape(NSA_GROUPS, NSA_HPG)

    def table(dist, valid):
        bias = rel[_t5_bucket(dist)] - far
        bias = jnp.where(valid[..., None], bias, NEG)
        bias = bias[:, :, heads]
        return bias.transpose(2, 0, 3, 1).reshape(NSA_GROUPS, dist.shape[0], QW)

    d = ql[None, :] - (jnp.arange(TB_ROWS)[:, None] - WINDOW)
    tb = table(d, (d >= 0) & (d < WINDOW))
    m = jnp.arange(TBC_ROWS)[:, None] - TBC_OFF
    dc = ql[None, :] - CMP_STRIDE * m - (CMP_BLOCK - 1)
    tbc = table(dc, dc >= 0)
    return tb, tbc


def _nsa_attn_kernel(qT_ref, gT_ref, kc_ref, vcT_ref, ksa_ref, vsT_ref, kwa_ref, vwT_ref, tb_ref, tbc_ref, ovl_ref,
                     o_ref, imp_ref, m_ref, l_ref, acc_ref, oT_ref):
    c = pl.program_id(1)
    nsel = N_SEL_BLOCKS
    jidx = lax.broadcasted_iota(jnp.int32, (nsel, Q_BLOCK), 0)
    qlane = lax.broadcasted_iota(jnp.int32, (nsel, Q_BLOCK), 1)
    jq = 2 * c + (qlane >= SEL_BLOCK).astype(jnp.int32)
    nrow = lax.broadcasted_iota(jnp.int32, (N_CMP_PAD, QW), 0)
    ovl = ovl_ref[...]

    def reset():
        m_ref[...] = jnp.full(m_ref.shape, NEG, F32)
        l_ref[...] = jnp.zeros_like(l_ref)
        acc_ref[...] = jnp.zeros_like(acc_ref)

    def update(s, vT):
        m_old = m_ref[...]
        m_new = jnp.maximum(m_old, jnp.max(s, axis=0, keepdims=True))
        alpha = jnp.exp(m_old - m_new)
        p = jnp.exp(s - m_new)
        l_ref[...] = l_ref[...] * alpha + jnp.sum(p, axis=0, keepdims=True)
        acc_ref[...] = acc_ref[...] * alpha + _dot(vT, p.astype(BF16))
        m_ref[...] = m_new

    for g in range(NSA_GROUPS):
        qTg = jnp.concatenate(
            [qT_ref[0, (g * NSA_HPG + hh) * NSA_HEAD_DIM:(g * NSA_HPG + hh + 1) * NSA_HEAD_DIM, :]
             for hh in range(NSA_HPG)], axis=1)
        gates = gT_ref[0, g * GATE_ROWS:(g + 1) * GATE_ROWS, :]
        gate_row = lambda br: jnp.concatenate(
            [gates[br * NSA_HPG + hh:br * NSA_HPG + hh + 1, :] for hh in range(NSA_HPG)], axis=1)

        start = pl.multiple_of(TBC_OFF - 8 * c, 8)
        tbc = tbc_ref[g, pl.ds(start, N_CMP_PAD), :]
        tbc = jnp.where(nrow < N_CMP_PAD - 1, tbc, NEG)
        s = _dot(kc_ref[0, g], qTg) + tbc
        valid = tbc > 0.5 * NEG
        e = jnp.where(valid, jnp.exp(s - jnp.max(s, axis=0, keepdims=True)), 0.0)
        l = jnp.sum(e, axis=0, keepdims=True)
        p = e / jnp.where(l > 0.0, l, 1.0)
        o_t = gate_row(0) * _dot(vcT_ref[0, g], p.astype(BF16))

        psum = p[:, 0:Q_BLOCK]
        for hh in range(1, NSA_HPG):
            psum = psum + p[:, hh * Q_BLOCK:(hh + 1) * Q_BLOCK]
        ph, plo = _split2(psum)
        imp = _dot(ovl, ph) + _dot(ovl, plo)
        forced = (jidx == 0) | (jidx == jq) | (jidx == jq - 1)
        imp = jnp.where(forced, -NEG, jnp.where(jidx <= jq, imp, NEG))
        imp_ref[...] = imp

        def rank_body(i, cnt):
            r = imp_ref[pl.ds(i, 1), :]
            ahead = (r > imp) | ((r == imp) & (i < jidx))
            return cnt + jnp.where(ahead, 1.0, 0.0)

        cnt = lax.fori_loop(0, nsel, rank_body, jnp.zeros((nsel, Q_BLOCK), F32))
        selm = jnp.where(cnt < float(min(TOP_N, nsel)), 0.0, NEG).astype(BF16)
        q_aug = jnp.concatenate([qTg, jnp.concatenate([selm] * NSA_HPG, axis=1),
                                 jnp.zeros((LANES - NSA_HEAD_DIM - nsel, QW), BF16)], axis=0)

        reset()
        n_far = jnp.maximum(c // 2 - 1, 0)

        def far_body(i, carry):
            k0 = pl.multiple_of(i * SEL_CHUNK, SEL_CHUNK)
            update(_dot(ksa_ref[0, g, pl.ds(k0, SEL_CHUNK), :], q_aug), vsT_ref[0, pl.ds(g * NSA_HEAD_DIM, NSA_HEAD_DIM), pl.ds(k0, SEL_CHUNK)])
            return carry

        lax.fori_loop(0, n_far, far_body, 0)

        def near(i):
            k0 = pl.multiple_of(i * SEL_CHUNK, SEL_CHUNK)
            off = c * Q_BLOCK - k0
            tstart = pl.multiple_of(WINDOW - off, Q_BLOCK)
            s_n = _dot(ksa_ref[0, g, pl.ds(k0, SEL_CHUNK), :], q_aug) + tb_ref[g, pl.ds(tstart, SEL_CHUNK), :]
            update(s_n, vsT_ref[0, pl.ds(g * NSA_HEAD_DIM, NSA_HEAD_DIM), pl.ds(k0, SEL_CHUNK)])

        @pl.when(c >= 2)
        def _():
            near(c // 2 - 1)

        near(c // 2)
        o_t = o_t + gate_row(1) * (acc_ref[...] / l_ref[...])

        reset()
        for i in range(WINDOW // WIN_CHUNK + 1):
            def win(i=i):
                k0 = pl.multiple_of((c - WINDOW // WIN_CHUNK + i) * WIN_CHUNK, WIN_CHUNK)
                s_w = _dot(kwa_ref[0, g, pl.ds(k0, WIN_CHUNK), :], q_aug)
                if i in (0, 3, 4):
                    s_w = s_w + tb_ref[g, i * WIN_CHUNK:(i + 1) * WIN_CHUNK, :]
                update(s_w, vwT_ref[0, pl.ds(g * NSA_HEAD_DIM, NSA_HEAD_DIM), pl.ds(k0, WIN_CHUNK)])

            if i == WINDOW // WIN_CHUNK:
                win()
            else:
                pl.when(c - WINDOW // WIN_CHUNK + i >= 0)(win)
        o_t = o_t + gate_row(2) * (acc_ref[...] / l_ref[...])

        for hh in range(NSA_HPG):
            oT_ref[hh, g * NSA_HEAD_DIM:(g + 1) * NSA_HEAD_DIM, :] = o_t[:, hh * Q_BLOCK:(hh + 1) * Q_BLOCK]

    for hh in range(NSA_HPG):
        o_ref[0, :, hh * KV_W:(hh + 1) * KV_W] = oT_ref[hh].T.astype(BF16)


def nsa_attn(qT, gT, kc, vcT, ksa, vsT, kwa, vwT, rel_bias):
    b, _, t = qT.shape
    nq = t // Q_BLOCK
    tb, tbc = _bias_tables(rel_bias)
    nc = (t - CMP_BLOCK) // CMP_STRIDE + 1
    cs = jnp.arange(N_CMP_PAD) * CMP_STRIDE
    ss = jnp.arange(N_SEL_BLOCKS) * SEL_BLOCK
    ovl = ((cs[None, :] < ss[:, None] + SEL_BLOCK) & (cs[None, :] + CMP_BLOCK - 1 >= ss[:, None])
           & (jnp.arange(N_CMP_PAD)[None, :] < nc)).astype(BF16)
    ng = NSA_GROUPS * GATE_ROWS
    per_b3 = lambda bb, cc: (bb, 0, 0)
    per_b4 = lambda bb, cc: (bb, 0, 0, 0)
    c3 = lambda bb, cc: (0, 0, 0)
    return pl.pallas_call(
        _nsa_attn_kernel,
        grid=(b, nq),
        in_specs=[pl.BlockSpec((1, NSA_WIDTH, Q_BLOCK), lambda bb, cc: (bb, 0, cc)),
                  pl.BlockSpec((1, ng, Q_BLOCK), lambda bb, cc: (bb, 0, cc)),
                  pl.BlockSpec((1, NSA_GROUPS, N_CMP_PAD, NSA_HEAD_DIM), per_b4),
                  pl.BlockSpec((1, NSA_GROUPS, NSA_HEAD_DIM, N_CMP_PAD), per_b4),
                  pl.BlockSpec((1, NSA_GROUPS, t, LANES), per_b4),
                  pl.BlockSpec((1, KV_W, t), per_b3),
                  pl.BlockSpec((1, NSA_GROUPS, t, LANES), per_b4),
                  pl.BlockSpec((1, KV_W, t), per_b3),
                  pl.BlockSpec((NSA_GROUPS, TB_ROWS, QW), c3),
                  pl.BlockSpec((NSA_GROUPS, TBC_ROWS, QW), c3),
                  pl.BlockSpec((N_SEL_BLOCKS, N_CMP_PAD), lambda bb, cc: (0, 0))],
        out_specs=pl.BlockSpec((1, Q_BLOCK, NSA_WIDTH), lambda bb, cc: (bb, cc, 0)),
        out_shape=jax.ShapeDtypeStruct((b, t, NSA_WIDTH), BF16),
        scratch_shapes=[pltpu.VMEM((N_SEL_BLOCKS, Q_BLOCK), F32),
                        pltpu.VMEM((1, QW), F32), pltpu.VMEM((1, QW), F32), pltpu.VMEM((NSA_HEAD_DIM, QW), F32),
                        pltpu.VMEM((NSA_HPG, KV_W, Q_BLOCK), F32)],
        compiler_params=_params(("arbitrary", "arbitrary")),
        name="nsa_attn",
    )(qT, gT, kc, vcT, ksa, vsT, kwa, vwT, tb, tbc, ovl)


def kernel(x, gdn_w_in, gdn_conv_w, gdn_a_log, gdn_dt_bias, gdn_norm_w, gdn_w_o, nsa_w_kv, cmp_k_pos, cmp_k_w1,
           cmp_k_w2, cmp_v_pos, cmp_v_w1, cmp_v_w2, nsa_w_in, nsa_w_o, rel_bias, ln_mix_g, ln_mix_b, ln_ffn_g,
           ln_ffn_b, moe_w_grp, moe_b_grp, moe_w_rt, moe_b_rt, moe_w_gate, moe_w_up, moe_w_down):
    b, t, d = x.shape
    n = b * t

    def ffn(h, layer):
        return moe_ln(h, moe_w_grp[layer], moe_b_grp[layer], moe_w_rt[layer], moe_b_rt[layer], moe_w_gate[layer],
                      moe_w_up[layer], moe_w_down[layer], ln_ffn_g[layer], ln_ffn_b[layer])

    q, k, v, z, gcb, gcbT = gdn_inproj(x, gdn_w_in[0], gdn_conv_w[0], gdn_a_log[0], gdn_dt_bias[0])
    o = gdn_rec(q, k, v, z, gcb, gcbT, gdn_norm_w[0])
    h = outproj_ln(o.reshape(n, GDN_WIDTH), gdn_w_o[0], x.reshape(n, d), ln_mix_g[0], ln_mix_b[0])
    h = ffn(h, 0)

    qT, gT, cmp_raw, ksa, kwa, vsT, vwT = nsa_proj(h.reshape(b, t, d), nsa_w_kv, nsa_w_in[0])
    kc, vcT = nsa_compress(cmp_raw, cmp_k_pos, cmp_k_w1, cmp_k_w2, cmp_v_pos, cmp_v_w1, cmp_v_w2)
    o = nsa_attn(qT, gT, kc, vcT, ksa, vsT, kwa, vwT, rel_bias)
    w_o = nsa_w_o[0].reshape(NSA_GROUPS, NSA_HPG, NSA_HEAD_DIM, d).transpose(1, 0, 2, 3).reshape(NSA_WIDTH, d)
    h = outproj_ln(o.reshape(n, NSA_WIDTH), w_o, h, ln_mix_g[1], ln_mix_b[1])
    h = ffn(h, 1)
    return h.reshape(b, t, d)
```

```python
import functools
import math

import jax
import jax.numpy as jnp
from jax import lax
from jax.experimental import pallas as pl
from jax.experimental.pallas import tpu as pltpu

F32 = jnp.float32
BF16 = jnp.bfloat16

D_MODEL = 1024
DEPTH = 2
GDN_HEADS = 8
GDN_HEAD_DIM = 128
GDN_WIDTH = GDN_HEADS * GDN_HEAD_DIM
GDN_CONV = 4
GDN_CHUNK = 64
NSA_HEADS = 16
NSA_GROUPS = 4
NSA_HPG = NSA_HEADS // NSA_GROUPS
NSA_HEAD_DIM = 64
NSA_WIDTH = NSA_HEADS * NSA_HEAD_DIM
CMP_BLOCK = 32
CMP_STRIDE = 16
CMP_HIDDEN = 2 * NSA_HEAD_DIM
SEL_BLOCK = 64
TOP_N = 16
WINDOW = 512
Q_BLOCK = 128
N_BRANCHES = 3
REL_BUCKETS = 32
REL_MAX_DIST = 128
MOE_GROUPS = 4
MOE_EPG = 4
MOE_EXPERTS = MOE_GROUPS * MOE_EPG
MOE_HIDDEN = 256
DEEPNORM_ALPHA = (2 * DEPTH) ** 0.25
LN_EPS = 1e-5
NORM_EPS = 1e-6
NEG = -1e30
LOG2E = math.log2(math.e)

LANES = 128
VMEM_LIMIT = 56 * 1024 * 1024


def _dot(a, b):
    return jnp.dot(a, b, preferred_element_type=F32)


def _dot_nt(a, b):
    return lax.dot_general(a, b, (((1,), (1,)), ((), ())), preferred_element_type=F32)


def _dot_tn(a, b):
    return lax.dot_general(a, b, (((0,), (0,)), ((), ())), preferred_element_type=F32)


def _split2(x):
    hi = x.astype(BF16)
    lo = (x - hi.astype(F32)).astype(BF16)
    return hi, lo


def _silu(x):
    return x * jax.nn.sigmoid(x)


def _params(sem):
    return pltpu.CompilerParams(dimension_semantics=sem, vmem_limit_bytes=VMEM_LIMIT)


def _layer_norm(x, g, b):
    mu = jnp.mean(x, axis=-1, keepdims=True)
    xc = x - mu
    var = jnp.mean(xc * xc, axis=-1, keepdims=True)
    return xc * lax.rsqrt(var + LN_EPS) * g + b


def _gdn_inproj_kernel(x_ref, w_ref, wabh_ref, wabl_ref, cw_ref, alog_ref, dtb_ref, ltri_ref,
                       q_ref, k_ref, v_ref, z_ref, gcb_ref, gcbT_ref, carry_ref, *, tm):
    @pl.when(pl.program_id(1) == 0)
    def _():
        carry_ref[...] = jnp.zeros_like(carry_ref)

    x = x_ref[0]
    xb, xlo = _split2(x)

    wh = wabh_ref[...]
    ab = _dot(xb, wh) + (_dot(xlo, wh) + _dot(xb, wabl_ref[...]))
    lane = lax.broadcasted_iota(jnp.int32, ab.shape, 1)
    sp_in = ab + dtb_ref[...]
    softplus = jnp.maximum(sp_in, 0.0) + jnp.log1p(jnp.exp(-jnp.abs(sp_in)))
    g = jnp.where(lane < GDN_HEADS, -jnp.exp(alog_ref[...]) * softplus, 0.0)
    beta = jax.nn.sigmoid(ab)
    g1 = g.astype(BF16)
    r1 = g - g1.astype(F32)
    g2 = r1.astype(BF16)
    g3 = (r1 - g2.astype(F32)).astype(BF16)
    ltri = ltri_ref[...]
    gc = _dot(ltri, g1) + (_dot(ltri, g2) + _dot(ltri, g3))
    gcb = jnp.where(lane < GDN_HEADS, gc, jnp.where(lane < 2 * GDN_HEADS, beta, 0.0))
    gcb_ref[0] = gcb
    gcbT_ref[0] = gcb.T[:2 * GDN_HEADS, :]

    row8 = lax.broadcasted_iota(jnp.int32, (8, 256), 0)
    outs = (q_ref, k_ref, v_ref)
    for s in range(3):
        for cc in range(4):
            col = s * GDN_WIDTH + cc * 256
            y = _dot(xb, w_ref[:, col:col + 256])
            prev = carry_ref[s * 4 + cc]
            carry_ref[s * 4 + cc] = y[tm - 8:, :]
            cw = cw_ref[:, col:col + 256]
            acc = y * cw[3:4, :]
            for kk in range(1, GDN_CONV):
                ry = pltpu.roll(y, kk, 0)
                rp = pltpu.roll(prev, kk, 0)
                head = jnp.where(row8 < kk, rp, ry[:8, :])
                shifted = jnp.concatenate([head, ry[8:, :]], axis=0)
                acc = acc + shifted * cw[3 - kk:4 - kk, :]
            a = _silu(acc)
            if s < 2:
                halves = []
                for hh in range(2):
                    ah = a[:, hh * LANES:(hh + 1) * LANES]
                    ss = jnp.sum(ah * ah, axis=-1, keepdims=True)
                    scale = lax.rsqrt(ss + NORM_EPS)
                    if s == 0:
                        scale = scale * (GDN_HEAD_DIM ** -0.5)
                    halves.append(ah * scale)
                a = jnp.concatenate(halves, axis=1)
            outs[s][0, :, cc * 256:(cc + 1) * 256] = a.astype(BF16)
    for cc in range(4):
        col = 3 * GDN_WIDTH + cc * 256
        z_ref[0, :, cc * 256:(cc + 1) * 256] = _dot(xb, w_ref[:, col:col + 256]).astype(BF16)


def gdn_inproj(x, w_in, conv_w, a_log, dt_bias, tm=256):
    b, t, d = x.shape
    hk = GDN_WIDTH
    w_main = w_in[:, :4 * hk].astype(BF16)
    w_ab = jnp.zeros((d, LANES), F32).at[:, :2 * GDN_HEADS].set(w_in[:, 4 * hk:])
    wabh = w_ab.astype(BF16)
    wabl = (w_ab - wabh.astype(F32)).astype(BF16)
    alog = jnp.zeros((1, LANES), F32).at[0, :GDN_HEADS].set(a_log)
    dtb = jnp.zeros((1, LANES), F32).at[0, :GDN_HEADS].set(dt_bias)
    r = jnp.arange(tm)
    ltri = ((r[:, None] // GDN_CHUNK == r[None, :] // GDN_CHUNK) & (r[:, None] >= r[None, :])).astype(BF16)
    tok = lambda bb, tt: (bb, tt, 0)
    const2 = lambda bb, tt: (0, 0)
    act = jax.ShapeDtypeStruct((b, t, hk), BF16)
    return pl.pallas_call(
        functools.partial(_gdn_inproj_kernel, tm=tm),
        grid=(b, t // tm),
        in_specs=[
            pl.BlockSpec((1, tm, d), tok),
            pl.BlockSpec((d, 4 * hk), const2),
            pl.BlockSpec((d, LANES), const2),
            pl.BlockSpec((d, LANES), const2),
            pl.BlockSpec((GDN_CONV, 3 * hk), const2),
            pl.BlockSpec((1, LANES), const2),
            pl.BlockSpec((1, LANES), const2),
            pl.BlockSpec((tm, tm), const2),
        ],
        out_specs=[
            pl.BlockSpec((1, tm, hk), tok),
            pl.BlockSpec((1, tm, hk), tok),
            pl.BlockSpec((1, tm, hk), tok),
            pl.BlockSpec((1, tm, hk), tok),
            pl.BlockSpec((1, tm, LANES), tok),
            pl.BlockSpec((1, 2 * GDN_HEADS, tm), lambda bb, tt: (bb, 0, tt)),
        ],
        out_shape=[act, act, act, act,
                   jax.ShapeDtypeStruct((b, t, LANES), F32),
                   jax.ShapeDtypeStruct((b, 2 * GDN_HEADS, t), F32)],
        scratch_shapes=[pltpu.VMEM((12, 8, 256), F32)],
        compiler_params=_params(("arbitrary", "arbitrary")),
        name="gdn_inproj",
    )(x, w_main, wabh, wabl, conv_w, alog, dtb, ltri)


def _bmm(a, b):
    return lax.dot_general(a, b, (((2,), (1,)), ((0,), (0,))), preferred_element_type=F32)


def _bmm_nt(a, b):
    return lax.dot_general(a, b, (((2,), (2,)), ((0,), (0,))), preferred_element_type=F32)


def _gdn_rec_kernel(q_ref, k_ref, v_ref, z_ref, gcb_ref, gcT_ref, nw_ref, o_ref, s_ref, *, nc):
    c = GDN_CHUNK
    nh = GDN_HEADS

    @pl.when(pl.program_id(1) == 0)
    def _():
        s_ref[...] = jnp.zeros_like(s_ref)

    row = lax.broadcasted_iota(jnp.int32, (nc, c, c), 1)
    col = lax.broadcasted_iota(jnp.int32, (nc, c, c), 2)
    eye = (row == col).astype(F32)
    nw = nw_ref[...]
    gcb = gcb_ref[0]
    gct = gcT_ref[0]

    u_h, wq_h, qkkd_h, dlast_h = [], [], [], []
    for h in range(nh):
        sl = slice(h * LANES, (h + 1) * LANES)
        q = q_ref[0, :, sl].reshape(nc, c, LANES)
        k = k_ref[0, :, sl].reshape(nc, c, LANES)
        v = v_ref[0, :, sl].astype(F32).reshape(nc, c, LANES)
        gc = gcb[:, h:h + 1].reshape(nc, c, 1)
        beta = gcb[:, nh + h:nh + h + 1].reshape(nc, c, 1)
        gc_row = gct[:, h:h + 1, :]
        g_last = gc[:, c - 1:c, :]
        decay = jnp.exp(jnp.where(row >= col, gc - gc_row, NEG))
        eg = jnp.exp(gc)
        kf = k.astype(F32)
        kb = kf * beta
        aq = _bmm_nt(jnp.concatenate([kb.astype(BF16), q], axis=1), k)
        a = jnp.where(row > col, aq[:, :c] * decay, 0.0)
        qk = aq[:, c:] * decay
        qpow = -a
        usum = eye + qpow
        qb = qpow.astype(BF16)
        qpow = _bmm(qb, qb)
        for lvl in range(5):
            qb = qpow.astype(BF16)
            if lvl < 4:
                prod = _bmm(jnp.concatenate([usum.astype(BF16), qb], axis=1), qb)
                usum = usum + prod[:, :c]
                qpow = prod[:, c:]
            else:
                usum = usum + _bmm(usum.astype(BF16), qb)
        rhs = jnp.concatenate([v * beta, kb * eg], axis=2).astype(BF16)
        uw = _bmm(usum.astype(BF16), rhs)
        u_h.append(uw[:, :, :LANES])
        wq_h.append(jnp.concatenate([uw[:, :, LANES:], q.astype(F32) * eg], axis=1).astype(BF16))
        k_dec = kf * jnp.exp(g_last - gc)
        k_dec_t = jnp.swapaxes(k_dec, 1, 2)
        qkkd_h.append(jnp.concatenate([qk, k_dec_t], axis=1).astype(BF16))
        dlast_h.append(jnp.exp(g_last))

    for i in range(nc):
        s_old = s_ref[...]
        lhs = jnp.stack([wq_h[h][i] for h in range(nh)])
        r = _bmm(lhs, s_old.astype(BF16))
        u = jnp.stack([u_h[h][i] for h in range(nh)])
        v_new = (u - r[:, :c]).astype(BF16)
        r2 = _bmm(jnp.stack([qkkd_h[h][i] for h in range(nh)]), v_new)
        o = r[:, c:] + r2[:, :c]
        dl = jnp.stack([dlast_h[h][i] for h in range(nh)])
        s_ref[...] = s_old * dl + r2[:, c:]
        ms = jnp.mean(o * o, axis=-1, keepdims=True)
        on = o * lax.rsqrt(ms + NORM_EPS) * nw
        for h in range(nh):
            z = z_ref[0, i * c:(i + 1) * c, h * LANES:(h + 1) * LANES].astype(F32)
            o_ref[0, i * c:(i + 1) * c, h * LANES:(h + 1) * LANES] = (on[h] * _silu(z)).astype(BF16)


def gdn_rec(q, k, v, z, gcb, gcbT, norm_w, nc=8):
    b, t, hk = q.shape
    n_chunks = t // GDN_CHUNK
    tb = nc * GDN_CHUNK
    gct4 = gcbT.reshape(b, 2 * GDN_HEADS, n_chunks, GDN_CHUNK).transpose(0, 2, 1, 3)
    spec = pl.BlockSpec((1, tb, hk), lambda bb, tt: (bb, tt, 0))
    return pl.pallas_call(
        functools.partial(_gdn_rec_kernel, nc=nc),
        grid=(b, t // tb),
        in_specs=[spec, spec, spec, spec,
                  pl.BlockSpec((1, tb, LANES), lambda bb, tt: (bb, tt, 0)),
                  pl.BlockSpec((1, nc, 2 * GDN_HEADS, GDN_CHUNK), lambda bb, tt: (bb, tt, 0, 0)),
                  pl.BlockSpec((1, LANES), lambda bb, tt: (0, 0))],
        out_specs=spec,
        out_shape=jax.ShapeDtypeStruct((b, t, hk), BF16),
        scratch_shapes=[pltpu.VMEM((GDN_HEADS, GDN_HEAD_DIM, GDN_HEAD_DIM), F32)],
        compiler_params=_params(("arbitrary", "arbitrary")),
        name="gdn_rec",
    )(q, k, v, z, gcb, gct4, norm_w.reshape(1, LANES).astype(F32))


def _outproj_ln_kernel(o_ref, w_ref, h_ref, g_ref, b_ref, out_ref):
    y = _dot(o_ref[...], w_ref[...])
    out_ref[...] = _layer_norm(DEEPNORM_ALPHA * h_ref[...] + y, g_ref[...], b_ref[...])


def outproj_ln(o, w, h, ln_g, ln_b, tm=512):
    n, d = h.shape
    kdim = o.shape[1]
    row = lambda i: (i, 0)
    const = lambda i: (0, 0)
    return pl.pallas_call(
        _outproj_ln_kernel,
        grid=(n // tm,),
        in_specs=[pl.BlockSpec((tm, kdim), row), pl.BlockSpec((kdim, d), const), pl.BlockSpec((tm, d), row),
                  pl.BlockSpec((1, d), const), pl.BlockSpec((1, d), const)],
        out_specs=pl.BlockSpec((tm, d), row),
        out_shape=jax.ShapeDtypeStruct((n, d), F32),
        compiler_params=_params(("arbitrary",)),
        name="outproj_ln",
    )(o, w.astype(BF16), h, ln_g.reshape(1, d), ln_b.reshape(1, d))


GRP_LANE0 = MOE_EXPERTS


def _route(logits):
    lane = lax.broadcasted_iota(jnp.int32, logits.shape, 1)
    lanef = lane.astype(F32)
    far = float(LANES)
    is_grp = (lane >= GRP_LANE0) & (lane < GRP_LANE0 + MOE_GROUPS)
    lg = jnp.where(is_grp, logits, NEG)
    eg = jnp.exp(lg - jnp.max(lg, axis=-1, keepdims=True))
    pg = eg / jnp.sum(eg, axis=-1, keepdims=True)
    gp = jnp.max(pg, axis=-1, keepdims=True)
    gidx = jnp.min(jnp.where(is_grp & (pg == gp), lanef, far), axis=-1, keepdims=True) - float(GRP_LANE0)
    in_grp = (lane < MOE_EXPERTS) & (jnp.floor(lanef * (1.0 / MOE_EPG)) == gidx)
    le = jnp.where(in_grp, logits, NEG)
    ee = jnp.exp(le - jnp.max(le, axis=-1, keepdims=True))
    pe = ee / jnp.sum(ee, axis=-1, keepdims=True)
    p1 = jnp.max(jnp.where(in_grp, pe, -1.0), axis=-1, keepdims=True)
    i1 = jnp.min(jnp.where(in_grp & (pe == p1), lanef, far), axis=-1, keepdims=True)
    rest = in_grp & (lanef != i1)
    p2 = jnp.max(jnp.where(rest, pe, -1.0), axis=-1, keepdims=True)
    i2 = jnp.min(jnp.where(rest & (pe == p2), lanef, far), axis=-1, keepdims=True)
    scale = gp / (p1 + p2)
    return jnp.where(lanef == i1, p1 * scale, jnp.where(lanef == i2, p2 * scale, 0.0))


def _moe_ln_kernel(h_ref, wrh_ref, wrl_ref, br_ref, wgu_ref, wd_ref, g_ref, b_ref, out_ref,
                   acc_ref, gate_ref, xb_ref):
    grp = pl.program_id(1)
    hid_w = MOE_EPG * MOE_HIDDEN

    @pl.when(grp == 0)
    def _():
        xb, xlo = _split2(h_ref[...])
        wh = wrh_ref[...]
        logits = _dot(xb, wh) + (_dot(xlo, wh) + _dot(xb, wrl_ref[...])) + br_ref[...]
        gate_ref[...] = _route(logits)
        xb_ref[...] = xb

    xb = xb_ref[...]
    gate = gate_ref[...]
    lane = lax.broadcasted_iota(jnp.int32, gate.shape, 1)
    hids = []
    for e in range(MOE_EPG):
        ge = jnp.sum(jnp.where(lane == grp * MOE_EPG + e, gate, 0.0), axis=-1, keepdims=True)
        hg = _dot(xb, wgu_ref[0, :, e * MOE_HIDDEN:(e + 1) * MOE_HIDDEN])
        hu = _dot(xb, wgu_ref[0, :, hid_w + e * MOE_HIDDEN:hid_w + (e + 1) * MOE_HIDDEN])
        hids.append((_silu(hg) * hu * ge).astype(BF16))
    y = _dot(jnp.concatenate(hids, axis=1), wd_ref[0])

    @pl.when(grp == 0)
    def _():
        acc_ref[...] = y

    @pl.when(grp > 0)
    def _():
        acc_ref[...] += y

    @pl.when(grp == MOE_GROUPS - 1)
    def _():
        out_ref[...] = _layer_norm(DEEPNORM_ALPHA * h_ref[...] + acc_ref[...], g_ref[...], b_ref[...])


def moe_ln(h, w_grp, b_grp, w_rt, b_rt, w_gate, w_up, w_down, ln_g, ln_b, tm=512):
    n, d = h.shape
    wr = jnp.zeros((d, LANES), F32).at[:, :MOE_EXPERTS].set(w_rt).at[:, GRP_LANE0:GRP_LANE0 + MOE_GROUPS].set(w_grp)
    br = jnp.zeros((1, LANES), F32).at[0, :MOE_EXPERTS].set(b_rt).at[0, GRP_LANE0:GRP_LANE0 + MOE_GROUPS].set(b_grp)
    wrh = wr.astype(BF16)
    wrl = (wr - wrh.astype(F32)).astype(BF16)
    hid_w = MOE_EPG * MOE_HIDDEN
    by_group = lambda w: w.reshape(MOE_GROUPS, MOE_EPG, d, MOE_HIDDEN).transpose(0, 2, 1, 3).reshape(MOE_GROUPS, d, hid_w)
    wgu = jnp.concatenate([by_group(w_gate), by_group(w_up)], axis=-1).astype(BF16)
    wd = w_down.reshape(MOE_GROUPS, hid_w, d).astype(BF16)
    row = lambda i, g: (i, 0)
    const = lambda i, g: (0, 0)
    return pl.pallas_call(
        _moe_ln_kernel,
        grid=(n // tm, MOE_GROUPS),
        in_specs=[pl.BlockSpec((tm, d), row),
                  pl.BlockSpec((d, LANES), const), pl.BlockSpec((d, LANES), const), pl.BlockSpec((1, LANES), const),
                  pl.BlockSpec((1, d, 2 * hid_w), lambda i, g: (g, 0, 0)),
                  pl.BlockSpec((1, hid_w, d), lambda i, g: (g, 0, 0)),
                  pl.BlockSpec((1, d), const), pl.BlockSpec((1, d), const)],
        out_specs=pl.BlockSpec((tm, d), row),
        out_shape=jax.ShapeDtypeStruct((n, d), F32),
        scratch_shapes=[pltpu.VMEM((tm, d), F32), pltpu.VMEM((tm, LANES), F32), pltpu.VMEM((tm, d), BF16)],
        compiler_params=_params(("arbitrary", "arbitrary")),
        name="moe_ln",
    )(h, wrh, wrl, br, wgu, wd, ln_g.reshape(1, d), ln_b.reshape(1, d))


KV_W = NSA_GROUPS * NSA_HEAD_DIM
GATE_ROWS = 16


def _nsa_proj_kernel(h_ref, wq_ref, wg_ref, wc_ref, wks_ref, wkw_ref, wvs_ref, wvw_ref,
                     qT_ref, gT_ref, cmp_ref, ksa_ref, kwa_ref, vsT_ref, vwT_ref, *, tm):
    t0 = pl.program_id(1) * tm
    hb = h_ref[0].astype(BF16)
    qT_ref[0] = _dot_nt(wq_ref[...], hb).astype(BF16)
    gT_ref[0] = jax.nn.sigmoid(_dot_nt(wg_ref[...], hb))
    cmp_ref[0] = _dot(hb, wc_ref[...]).astype(BF16)
    vsT_ref[0] = _dot_nt(wvs_ref[...], hb).astype(BF16)
    vwT_ref[0] = _dot_nt(wvw_ref[...], hb).astype(BF16)
    ks = _dot(hb, wks_ref[...])
    kw = _dot(hb, wkw_ref[...])
    lane = lax.broadcasted_iota(jnp.int32, (tm, LANES), 1)
    blk = (t0 + lax.broadcasted_iota(jnp.int32, (tm, LANES), 0)) // SEL_BLOCK
    onehot = jnp.where(lane - NSA_HEAD_DIM == blk, 1.0, 0.0)
    for g in range(NSA_GROUPS):
        ksg = ks[:, g * LANES:(g + 1) * LANES]
        ksa_ref[0, g] = jnp.where(lane < NSA_HEAD_DIM, ksg, onehot).astype(BF16)
        kwa_ref[0, g] = kw[:, g * LANES:(g + 1) * LANES].astype(BF16)


def _pad_group_cols(w):
    d = w.shape[0]
    w4 = w.reshape(d, NSA_GROUPS, NSA_HEAD_DIM)
    return jnp.concatenate([w4, jnp.zeros_like(w4)], axis=-1).reshape(d, NSA_GROUPS * LANES)


def nsa_proj(h, w_kv, w_in, tm=256):
    b, t, d = h.shape
    scale = NSA_HEAD_DIM ** -0.5 * LOG2E
    wq = (w_in[:, :NSA_WIDTH] * scale).T.astype(BF16)
    wgate = w_in[:, NSA_WIDTH:].reshape(d, NSA_GROUPS, NSA_HPG, N_BRANCHES)
    wgate = wgate.transpose(0, 1, 3, 2).reshape(d, NSA_GROUPS, N_BRANCHES * NSA_HPG)
    wgate = jnp.concatenate([wgate, jnp.zeros((d, NSA_GROUPS, GATE_ROWS - N_BRANCHES * NSA_HPG), F32)], axis=-1)
    wg = wgate.reshape(d, NSA_GROUPS * GATE_ROWS).T.astype(BF16)
    part = lambda p: w_kv[:, p * KV_W:(p + 1) * KV_W]
    wc = w_kv[:, :2 * KV_W].astype(BF16)
    wks = _pad_group_cols(part(2)).astype(BF16)
    wvs = part(3).T.astype(BF16)
    wkw = _pad_group_cols(part(4)).astype(BF16)
    wvw = part(5).T.astype(BF16)
    const = lambda bb, tt: (0, 0)
    tok = lambda bb, tt: (bb, tt, 0)
    tr = lambda bb, tt: (bb, 0, tt)
    g4 = lambda bb, tt: (bb, 0, tt, 0)
    ng = NSA_GROUPS * GATE_ROWS
    return pl.pallas_call(
        functools.partial(_nsa_proj_kernel, tm=tm),
        grid=(b, t // tm),
        in_specs=[pl.BlockSpec((1, tm, d), tok),
                  pl.BlockSpec((NSA_WIDTH, d), const), pl.BlockSpec((ng, d), const),
                  pl.BlockSpec((d, 2 * KV_W), const),
                  pl.BlockSpec((d, NSA_GROUPS * LANES), const), pl.BlockSpec((d, NSA_GROUPS * LANES), const),
                  pl.BlockSpec((KV_W, d), const), pl.BlockSpec((KV_W, d), const)],
        out_specs=[pl.BlockSpec((1, NSA_WIDTH, tm), tr), pl.BlockSpec((1, ng, tm), tr),
                   pl.BlockSpec((1, tm, 2 * KV_W), tok),
                   pl.BlockSpec((1, NSA_GROUPS, tm, LANES), g4), pl.BlockSpec((1, NSA_GROUPS, tm, LANES), g4),
                   pl.BlockSpec((1, KV_W, tm), tr), pl.BlockSpec((1, KV_W, tm), tr)],
        out_shape=[jax.ShapeDtypeStruct((b, NSA_WIDTH, t), BF16), jax.ShapeDtypeStruct((b, ng, t), F32),
                   jax.ShapeDtypeStruct((b, t, 2 * KV_W), BF16),
                   jax.ShapeDtypeStruct((b, NSA_GROUPS, t, LANES), BF16),
                   jax.ShapeDtypeStruct((b, NSA_GROUPS, t, LANES), BF16),
                   jax.ShapeDtypeStruct((b, KV_W, t), BF16), jax.ShapeDtypeStruct((b, KV_W, t), BF16)],
        compiler_params=_params(("arbitrary", "arbitrary")),
        name="nsa_proj",
    )(h, wq, wg, wc, wks, wkw, wvs, wvw)


N_CMP_PAD = 128


def _nsa_compress_kernel(xk_ref, xv_ref, pk_ref, pv_ref, w1k_ref, w1v_ref, w2k_ref, w2vT_ref, kc_ref, vcT_ref):
    half = CMP_STRIDE * NSA_HEAD_DIM
    pbk = _dot(pk_ref[...], w1k_ref[...])[0:1, :]
    pbv = _dot(pv_ref[...], w1v_ref[...])[0:1, :]
    for g in range(NSA_GROUPS):
        xk = xk_ref[0, g]
        hk = _dot(xk, w1k_ref[:half, :]) + pltpu.roll(_dot(xk, w1k_ref[half:, :]), N_CMP_PAD - 1, 0) + pbk
        kc_ref[0, g] = _dot(_silu(hk).astype(BF16), w2k_ref[...]).astype(BF16)
        xv = xv_ref[0, g]
        hv = _dot(xv, w1v_ref[:half, :]) + pltpu.roll(_dot(xv, w1v_ref[half:, :]), N_CMP_PAD - 1, 0) + pbv
        vcT_ref[0, g] = _dot_nt(w2vT_ref[...], _silu(hv).astype(BF16)).astype(BF16)


def nsa_compress(cmp_raw, k_pos, k_w1, k_w2, v_pos, v_w1, v_w2):
    b, t, _ = cmp_raw.shape
    nrow = t // CMP_STRIDE
    assert nrow == N_CMP_PAD
    wide = CMP_STRIDE * NSA_HEAD_DIM
    x5 = cmp_raw.reshape(b, nrow, CMP_STRIDE, 2, NSA_GROUPS, NSA_HEAD_DIM)
    x5 = x5.transpose(3, 0, 4, 1, 2, 5).reshape(2, b, NSA_GROUPS, nrow, wide)
    pad_pos = lambda p: jnp.zeros((8, CMP_BLOCK * NSA_HEAD_DIM), F32).at[0].set(p.reshape(-1)).astype(BF16)
    spec_x = pl.BlockSpec((1, NSA_GROUPS, nrow, wide), lambda bb: (bb, 0, 0, 0))
    c2 = lambda bb: (0, 0)
    return pl.pallas_call(
        _nsa_compress_kernel,
        grid=(b,),
        in_specs=[spec_x, spec_x,
                  pl.BlockSpec((8, 2 * wide), c2), pl.BlockSpec((8, 2 * wide), c2),
                  pl.BlockSpec((2 * wide, CMP_HIDDEN), c2), pl.BlockSpec((2 * wide, CMP_HIDDEN), c2),
                  pl.BlockSpec((CMP_HIDDEN, NSA_HEAD_DIM), c2), pl.BlockSpec((NSA_HEAD_DIM, CMP_HIDDEN), c2)],
        out_specs=[pl.BlockSpec((1, NSA_GROUPS, N_CMP_PAD, NSA_HEAD_DIM), lambda bb: (bb, 0, 0, 0)),
                   pl.BlockSpec((1, NSA_GROUPS, NSA_HEAD_DIM, N_CMP_PAD), lambda bb: (bb, 0, 0, 0))],
        out_shape=[jax.ShapeDtypeStruct((b, NSA_GROUPS, N_CMP_PAD, NSA_HEAD_DIM), BF16),
                   jax.ShapeDtypeStruct((b, NSA_GROUPS, NSA_HEAD_DIM, N_CMP_PAD), BF16)],
        compiler_params=_params(("arbitrary",)),
        name="nsa_compress",
    )(x5[0], x5[1], pad_pos(k_pos), pad_pos(v_pos), k_w1.astype(BF16), v_w1.astype(BF16),
      k_w2.astype(BF16), v_w2.T.astype(BF16))


SEL_CHUNK = 256
WIN_CHUNK = 128
TB_ROWS = WINDOW + SEL_CHUNK
TBC_ROWS = 256
TBC_OFF = 120
N_SEL_BLOCKS = 32
QW = NSA_HPG * Q_BLOCK


def _t5_bucket(dist):
    n = jnp.maximum(dist, 0)
    max_exact = REL_BUCKETS // 2
    nf = jnp.maximum(n, 1).astype(F32)
    large = max_exact + (jnp.log(nf / max_exact) / math.log(REL_MAX_DIST / max_exact)
                         * (REL_BUCKETS - max_exact)).astype(jnp.int32)
    large = jnp.minimum(large, REL_BUCKETS - 1)
    return jnp.where(n < max_exact, n, large)


def _bias_tables(rel_bias):
    rel = rel_bias.astype(F32)
    far = rel[REL_BUCKETS - 1]
    ql = jnp.arange(Q_BLOCK)
    heads = jnp.arange(NSA_HEADS).reshape(NSA_GROUPS, NSA_HPG)

    def table(dist, valid):
        bucket = _t5_bucket(dist)[None, :, None, :]
        relc = ((rel - far) * LOG2E)[:, heads][:, :, None, :, None]
        bias = jnp.zeros((NSA_GROUPS, dist.shape[0], NSA_HPG, Q_BLOCK), F32)
        for b in range(REL_BUCKETS):
            bias = jnp.where(bucket == b, relc[b], bias)
        bias = jnp.where(valid[None, :, None, :], bias, NEG)
        return bias.reshape(NSA_GROUPS, dist.shape[0], QW)

    d = ql[None, :] - (jnp.arange(TB_ROWS)[:, None] - WINDOW)
    tb = table(d, (d >= 0) & (d < WINDOW))
    m = jnp.arange(TBC_ROWS)[:, None] - TBC_OFF
    dc = ql[None, :] - CMP_STRIDE * m - (CMP_BLOCK - 1)
    tbc = table(dc, dc >= 0)
    return tb, tbc


def _nsa_attn_kernel(qT_ref, gT_ref, kc_ref, vcT_ref, ksa_ref, vsT_ref, kwa_ref, vwT_ref, tb_ref, tbc_ref, ovl_ref,
                     o_ref, m_ref, l_ref, acc_ref, ot_ref, oT_ref):
    c = pl.program_id(1)
    ng, hpg, hd = NSA_GROUPS, NSA_HPG, NSA_HEAD_DIM
    nsel = N_SEL_BLOCKS
    n_top = min(TOP_N, nsel)
    n_win = WINDOW // WIN_CHUNK

    q4 = jnp.stack([jnp.concatenate([qT_ref[0, (g * hpg + hh) * hd:(g * hpg + hh + 1) * hd, :]
                                     for hh in range(hpg)], axis=1) for g in range(ng)])
    gates = gT_ref[0]

    def gate3(br):
        return jnp.stack([jnp.concatenate(
            [gates[g * GATE_ROWS + br * hpg + hh:g * GATE_ROWS + br * hpg + hh + 1, :] for hh in range(hpg)], axis=1)
            for g in range(ng)])

    def fresh():
        return (jnp.full((ng, 1, QW), NEG, F32), jnp.zeros((ng, 1, QW), F32), jnp.zeros((ng, hd, QW), F32))

    def load_state():
        return m_ref[...], l_ref[...], acc_ref[...]

    def store_state(state):
        m_ref[...], l_ref[...], acc_ref[...] = state

    def upd(state, s, vT):
        m_old, l_old, acc_old = state
        m_new = jnp.maximum(m_old, jnp.max(s, axis=1, keepdims=True))
        alpha = jnp.exp2(m_old - m_new)
        p = jnp.exp2(s - m_new)
        return (m_new, l_old * alpha + jnp.sum(p, axis=1, keepdims=True),
                acc_old * alpha + _bmm(vT, p.astype(BF16)))

    def result(state, br):
        _, l_fin, acc_fin = state
        return gate3(br) * (acc_fin * (1.0 / l_fin))

    start = pl.multiple_of(TBC_OFF - 8 * c, 8)
    tbc = tbc_ref[:, pl.ds(start, N_CMP_PAD), :]
    nrow = lax.broadcasted_iota(jnp.int32, tbc.shape, 1)
    tbc = jnp.where(nrow < N_CMP_PAD - 1, tbc, NEG)
    s = _bmm(kc_ref[0], q4) + tbc
    valid = tbc > 0.5 * NEG
    e = jnp.where(valid, jnp.exp2(s - jnp.max(s, axis=1, keepdims=True)), 0.0)
    l = jnp.sum(e, axis=1, keepdims=True)
    p = e * (1.0 / jnp.where(l > 0.0, l, 1.0))
    o_cmp = gate3(0) * _bmm(vcT_ref[0], p.astype(BF16))

    psum = p[:, :, 0:Q_BLOCK]
    for hh in range(1, hpg):
        psum = psum + p[:, :, hh * Q_BLOCK:(hh + 1) * Q_BLOCK]
    psum = jnp.concatenate([psum[g] for g in range(ng)], axis=1)
    ph, plo = _split2(psum)
    ovl = ovl_ref[...]
    imp = _dot(ovl, ph) + _dot(ovl, plo)
    jidx = lax.broadcasted_iota(jnp.int32, imp.shape, 0)
    qlane = lax.broadcasted_iota(jnp.int32, imp.shape, 1) & (Q_BLOCK - 1)
    jq = 2 * c + (qlane >= SEL_BLOCK).astype(jnp.int32)
    forced = (jidx == 0) | (jidx == jq) | (jidx == jq - 1)
    imp = jnp.where(forced, -NEG, jnp.where(jidx <= jq, imp, NEG))
    sub8 = lax.broadcasted_iota(jnp.int32, (8, imp.shape[1]), 0)
    rows = [imp[8 * v:8 * v + 8, :] for v in range(nsel // 8)]
    cnts = [jnp.zeros_like(r) for r in rows]
    for i in range(nsel):
        r = imp[i:i + 1, :]
        for v in range(nsel // 8):
            ge = jnp.where(r >= rows[v], 1.0, 0.0)
            gt = jnp.where(r > rows[v], 1.0, 0.0)
            if i < 8 * v:
                ahead = ge
            elif i >= 8 * v + 8:
                ahead = gt
            else:
                ahead = jnp.where(sub8 > i - 8 * v, ge, gt)
            cnts[v] = cnts[v] + ahead
    cnt = jnp.concatenate(cnts, axis=0)
    selm = jnp.where(cnt < float(n_top), 0.0, NEG).astype(BF16)
    pad = jnp.zeros((LANES - hd - nsel, QW), BF16)
    q_aug = jnp.stack([jnp.concatenate(
        [q4[g], jnp.concatenate([selm[:, g * Q_BLOCK:(g + 1) * Q_BLOCK]] * hpg, axis=1), pad], axis=0)
        for g in range(ng)])

    def sel_scores(k0):
        return _bmm(ksa_ref[0, :, pl.ds(k0, SEL_CHUNK), :], q_aug)

    def sel_values(k0):
        return vsT_ref[0, :, pl.ds(k0, SEL_CHUNK)].reshape(ng, hd, SEL_CHUNK)

    def sel_near(i):
        k0 = pl.multiple_of(i * SEL_CHUNK, SEL_CHUNK)
        tstart = pl.multiple_of(WINDOW - (c * Q_BLOCK - k0), Q_BLOCK)
        return sel_scores(k0) + tb_ref[:, pl.ds(tstart, SEL_CHUNK), :], sel_values(k0)

    def win(first, n_chunks, tab_chunk):
        width = n_chunks * WIN_CHUNK
        k0 = pl.multiple_of((c - n_win + first) * WIN_CHUNK, WIN_CHUNK)
        s_w = _bmm(kwa_ref[0, :, pl.ds(k0, width), :], q_aug)
        if tab_chunk is not None:
            lo = (tab_chunk - first) * WIN_CHUNK
            part = s_w[:, lo:lo + WIN_CHUNK] + tb_ref[:, tab_chunk * WIN_CHUNK:(tab_chunk + 1) * WIN_CHUNK, :]
            pieces = ([s_w[:, :lo]] if lo else []) + [part] + ([s_w[:, lo + WIN_CHUNK:]] if lo + WIN_CHUNK < width else [])
            s_w = jnp.concatenate(pieces, axis=1) if len(pieces) > 1 else part
        return s_w, vwT_ref[0, :, pl.ds(k0, width)].reshape(ng, hd, width)

    store_state(fresh())

    def far_body(i, carry):
        k0 = pl.multiple_of(i * SEL_CHUNK, SEL_CHUNK)
        store_state(upd(load_state(), sel_scores(k0), sel_values(k0)))
        return carry

    lax.fori_loop(0, jnp.maximum(c // 2 - 1, 0), far_body, 0)

    @pl.when(c >= n_win)
    def _():
        st_s = upd(load_state(), *sel_near(c // 2 - 1))
        st_w = upd(fresh(), *win(0, 2, 0))
        st_s = upd(st_s, *sel_near(c // 2))
        st_w = upd(st_w, *win(2, 2, 3))
        st_w = upd(st_w, *win(4, 1, 4))
        ot_ref[...] = o_cmp + result(st_s, 1) + result(st_w, 2)

    @pl.when(c < n_win)
    def _():
        @pl.when(c >= 2)
        def _():
            store_state(upd(load_state(), *sel_near(c // 2 - 1)))

        ot_ref[...] = o_cmp + result(upd(load_state(), *sel_near(c // 2)), 1)
        store_state(fresh())
        pl.when(c == 3)(lambda: store_state(upd(load_state(), *win(1, 1, None))))
        pl.when(c >= 2)(lambda: store_state(upd(load_state(), *win(2, 2, 3))))
        pl.when(c == 1)(lambda: store_state(upd(load_state(), *win(3, 1, 3))))
        ot_ref[...] += result(upd(load_state(), *win(4, 1, 4)), 2)

    for g in range(ng):
        for hh in range(hpg):
            oT_ref[hh, g * hd:(g + 1) * hd, :] = ot_ref[g, :, hh * Q_BLOCK:(hh + 1) * Q_BLOCK]
    for hh in range(hpg):
        o_ref[0, :, hh * KV_W:(hh + 1) * KV_W] = oT_ref[hh].T.astype(BF16)


def nsa_attn(qT, gT, kc, vcT, ksa, vsT, kwa, vwT, rel_bias):
    b, _, t = qT.shape
    nq = t // Q_BLOCK
    tb, tbc = _bias_tables(rel_bias)
    nc = (t - CMP_BLOCK) // CMP_STRIDE + 1
    cs = jnp.arange(N_CMP_PAD) * CMP_STRIDE
    ss = jnp.arange(N_SEL_BLOCKS) * SEL_BLOCK
    ovl = ((cs[None, :] < ss[:, None] + SEL_BLOCK) & (cs[None, :] + CMP_BLOCK - 1 >= ss[:, None])
           & (jnp.arange(N_CMP_PAD)[None, :] < nc)).astype(BF16)
    ng = NSA_GROUPS * GATE_ROWS
    per_b3 = lambda bb, cc: (bb, 0, 0)
    per_b4 = lambda bb, cc: (bb, 0, 0, 0)
    c3 = lambda bb, cc: (0, 0, 0)
    return pl.pallas_call(
        _nsa_attn_kernel,
        grid=(b, nq),
        in_specs=[pl.BlockSpec((1, NSA_WIDTH, Q_BLOCK), lambda bb, cc: (bb, 0, cc)),
                  pl.BlockSpec((1, ng, Q_BLOCK), lambda bb, cc: (bb, 0, cc)),
                  pl.BlockSpec((1, NSA_GROUPS, N_CMP_PAD, NSA_HEAD_DIM), per_b4),
                  pl.BlockSpec((1, NSA_GROUPS, NSA_HEAD_DIM, N_CMP_PAD), per_b4),
                  pl.BlockSpec((1, NSA_GROUPS, t, LANES), per_b4),
                  pl.BlockSpec((1, KV_W, t), per_b3),
                  pl.BlockSpec((1, NSA_GROUPS, t, LANES), per_b4),
                  pl.BlockSpec((1, KV_W, t), per_b3),
                  pl.BlockSpec((NSA_GROUPS, TB_ROWS, QW), c3),
                  pl.BlockSpec((NSA_GROUPS, TBC_ROWS, QW), c3),
                  pl.BlockSpec((N_SEL_BLOCKS, N_CMP_PAD), lambda bb, cc: (0, 0))],
        out_specs=pl.BlockSpec((1, Q_BLOCK, NSA_WIDTH), lambda bb, cc: (bb, cc, 0)),
        out_shape=jax.ShapeDtypeStruct((b, t, NSA_WIDTH), BF16),
        scratch_shapes=[pltpu.VMEM((NSA_GROUPS, 1, QW), F32), pltpu.VMEM((NSA_GROUPS, 1, QW), F32),
                        pltpu.VMEM((NSA_GROUPS, NSA_HEAD_DIM, QW), F32),
                        pltpu.VMEM((NSA_GROUPS, NSA_HEAD_DIM, QW), F32),
                        pltpu.VMEM((NSA_HPG, KV_W, Q_BLOCK), F32)],
        compiler_params=_params(("arbitrary", "arbitrary")),
        name="nsa_attn",
    )(qT, gT, kc, vcT, ksa, vsT, kwa, vwT, tb, tbc, ovl)


def kernel(x, gdn_w_in, gdn_conv_w, gdn_a_log, gdn_dt_bias, gdn_norm_w, gdn_w_o, nsa_w_kv, cmp_k_pos, cmp_k_w1,
           cmp_k_w2, cmp_v_pos, cmp_v_w1, cmp_v_w2, nsa_w_in, nsa_w_o, rel_bias, ln_mix_g, ln_mix_b, ln_ffn_g,
           ln_ffn_b, moe_w_grp, moe_b_grp, moe_w_rt, moe_b_rt, moe_w_gate, moe_w_up, moe_w_down):
    b, t, d = x.shape
    n = b * t

    def ffn(h, layer):
        return moe_ln(h, moe_w_grp[layer], moe_b_grp[layer], moe_w_rt[layer], moe_b_rt[layer], moe_w_gate[layer],
                      moe_w_up[layer], moe_w_down[layer], ln_ffn_g[layer], ln_ffn_b[layer])

    q, k, v, z, gcb, gcbT = gdn_inproj(x, gdn_w_in[0], gdn_conv_w[0], gdn_a_log[0], gdn_dt_bias[0])
    o = gdn_rec(q, k, v, z, gcb, gcbT, gdn_norm_w[0])
    h = outproj_ln(o.reshape(n, GDN_WIDTH), gdn_w_o[0], x.reshape(n, d), ln_mix_g[0], ln_mix_b[0])
    h = ffn(h, 0)

    qT, gT, cmp_raw, ksa, kwa, vsT, vwT = nsa_proj(h.reshape(b, t, d), nsa_w_kv, nsa_w_in[0])
    kc, vcT = nsa_compress(cmp_raw, cmp_k_pos, cmp_k_w1, cmp_k_w2, cmp_v_pos, cmp_v_w1, cmp_v_w2)
    o = nsa_attn(qT, gT, kc, vcT, ksa, vsT, kwa, vwT, rel_bias)
    w_o = nsa_w_o[0].reshape(NSA_GROUPS, NSA_HPG, NSA_HEAD_DIM, d).transpose(1, 0, 2, 3).reshape(NSA_WIDTH, d)
    h = outproj_ln(o.reshape(n, NSA_WIDTH), w_o, h, ln_mix_g[1], ln_mix_b[1])
    h = ffn(h, 1)
    return h.reshape(b, t, d)
```

```python
import functools
import math

import jax
import jax.numpy as jnp
from jax import lax
from jax.experimental import pallas as pl
from jax.experimental.pallas import tpu as pltpu

F32 = jnp.float32
BF16 = jnp.bfloat16

D_MODEL = 1024
DEPTH = 2
GDN_HEADS = 8
GDN_HEAD_DIM = 128
GDN_WIDTH = GDN_HEADS * GDN_HEAD_DIM
GDN_CONV = 4
GDN_CHUNK = 64
NSA_HEADS = 16
NSA_GROUPS = 4
NSA_HPG = NSA_HEADS // NSA_GROUPS
NSA_HEAD_DIM = 64
NSA_WIDTH = NSA_HEADS * NSA_HEAD_DIM
CMP_BLOCK = 32
CMP_STRIDE = 16
CMP_HIDDEN = 2 * NSA_HEAD_DIM
SEL_BLOCK = 64
TOP_N = 16
WINDOW = 512
Q_BLOCK = 128
N_BRANCHES = 3
REL_BUCKETS = 32
REL_MAX_DIST = 128
MOE_GROUPS = 4
MOE_EPG = 4
MOE_EXPERTS = MOE_GROUPS * MOE_EPG
MOE_HIDDEN = 256
DEEPNORM_ALPHA = (2 * DEPTH) ** 0.25
LN_EPS = 1e-5
NORM_EPS = 1e-6
NEG = -1e30
LOG2E = math.log2(math.e)

LANES = 128
VMEM_LIMIT = 56 * 1024 * 1024


def _dot(a, b):
    return jnp.dot(a, b, preferred_element_type=F32)


def _dot_nt(a, b):
    return lax.dot_general(a, b, (((1,), (1,)), ((), ())), preferred_element_type=F32)


def _dot_tn(a, b):
    return lax.dot_general(a, b, (((0,), (0,)), ((), ())), preferred_element_type=F32)


def _split2(x):
    hi = x.astype(BF16)
    lo = (x - hi.astype(F32)).astype(BF16)
    return hi, lo


def _silu(x):
    return x * jax.nn.sigmoid(x)


def _params(sem):
    return pltpu.CompilerParams(dimension_semantics=sem, vmem_limit_bytes=VMEM_LIMIT)


def _layer_norm(x, g, b):
    mu = jnp.mean(x, axis=-1, keepdims=True)
    xc = x - mu
    var = jnp.mean(xc * xc, axis=-1, keepdims=True)
    return xc * lax.rsqrt(var + LN_EPS) * g + b


def _gdn_inproj_kernel(x_ref, w_ref, wabh_ref, wabl_ref, cw_ref, alog_ref, dtb_ref, ltri_ref,
                       q_ref, k_ref, v_ref, z_ref, gcb_ref, gcbT_ref, carry_ref, *, tm):
    @pl.when(pl.program_id(1) == 0)
    def _():
        carry_ref[...] = jnp.zeros_like(carry_ref)

    x = x_ref[0]
    xb, xlo = _split2(x)

    wh = wabh_ref[...]
    ab = _dot(xb, wh) + (_dot(xlo, wh) + _dot(xb, wabl_ref[...]))
    lane = lax.broadcasted_iota(jnp.int32, ab.shape, 1)
    sp_in = ab + dtb_ref[...]
    softplus = jnp.maximum(sp_in, 0.0) + jnp.log1p(jnp.exp(-jnp.abs(sp_in)))
    g = jnp.where(lane < GDN_HEADS, -jnp.exp(alog_ref[...]) * softplus, 0.0)
    beta = jax.nn.sigmoid(ab)
    g1 = g.astype(BF16)
    r1 = g - g1.astype(F32)
    g2 = r1.astype(BF16)
    g3 = (r1 - g2.astype(F32)).astype(BF16)
    ltri = ltri_ref[...]
    gc = _dot(ltri, g1) + (_dot(ltri, g2) + _dot(ltri, g3))
    gcb = jnp.where(lane < GDN_HEADS, gc, jnp.where(lane < 2 * GDN_HEADS, beta, 0.0))
    gcb_ref[0] = gcb
    gcbT_ref[0] = gcb.T[:2 * GDN_HEADS, :]

    row8 = lax.broadcasted_iota(jnp.int32, (8, 256), 0)
    outs = (q_ref, k_ref, v_ref)
    for s in range(3):
        for cc in range(4):
            col = s * GDN_WIDTH + cc * 256
            y = _dot(xb, w_ref[:, col:col + 256])
            prev = carry_ref[s * 4 + cc]
            carry_ref[s * 4 + cc] = y[tm - 8:, :]
            cw = cw_ref[:, col:col + 256]
            acc = y * cw[3:4, :]
            for kk in range(1, GDN_CONV):
                ry = pltpu.roll(y, kk, 0)
                rp = pltpu.roll(prev, kk, 0)
                head = jnp.where(row8 < kk, rp, ry[:8, :])
                shifted = jnp.concatenate([head, ry[8:, :]], axis=0)
                acc = acc + shifted * cw[3 - kk:4 - kk, :]
            a = _silu(acc)
            if s < 2:
                halves = []
                for hh in range(2):
                    ah = a[:, hh * LANES:(hh + 1) * LANES]
                    ss = jnp.sum(ah * ah, axis=-1, keepdims=True)
                    scale = lax.rsqrt(ss + NORM_EPS)
                    if s == 0:
                        scale = scale * (GDN_HEAD_DIM ** -0.5)
                    halves.append(ah * scale)
                a = jnp.concatenate(halves, axis=1)
            outs[s][0, :, cc * 256:(cc + 1) * 256] = a.astype(BF16)
    for cc in range(4):
        col = 3 * GDN_WIDTH + cc * 256
        z_ref[0, :, cc * 256:(cc + 1) * 256] = _dot(xb, w_ref[:, col:col + 256]).astype(BF16)


def gdn_inproj(x, w_in, conv_w, a_log, dt_bias, tm=256):
    b, t, d = x.shape
    hk = GDN_WIDTH
    w_main = w_in[:, :4 * hk].astype(BF16)
    w_ab = jnp.zeros((d, LANES), F32).at[:, :2 * GDN_HEADS].set(w_in[:, 4 * hk:])
    wabh = w_ab.astype(BF16)
    wabl = (w_ab - wabh.astype(F32)).astype(BF16)
    alog = jnp.zeros((1, LANES), F32).at[0, :GDN_HEADS].set(a_log)
    dtb = jnp.zeros((1, LANES), F32).at[0, :GDN_HEADS].set(dt_bias)
    r = jnp.arange(tm)
    ltri = ((r[:, None] // GDN_CHUNK == r[None, :] // GDN_CHUNK) & (r[:, None] >= r[None, :])).astype(BF16)
    tok = lambda bb, tt: (bb, tt, 0)
    const2 = lambda bb, tt: (0, 0)
    act = jax.ShapeDtypeStruct((b, t, hk), BF16)
    return pl.pallas_call(
        functools.partial(_gdn_inproj_kernel, tm=tm),
        grid=(b, t // tm),
        in_specs=[
            pl.BlockSpec((1, tm, d), tok),
            pl.BlockSpec((d, 4 * hk), const2),
            pl.BlockSpec((d, LANES), const2),
            pl.BlockSpec((d, LANES), const2),
            pl.BlockSpec((GDN_CONV, 3 * hk), const2),
            pl.BlockSpec((1, LANES), const2),
            pl.BlockSpec((1, LANES), const2),
            pl.BlockSpec((tm, tm), const2),
        ],
        out_specs=[
            pl.BlockSpec((1, tm, hk), tok),
            pl.BlockSpec((1, tm, hk), tok),
            pl.BlockSpec((1, tm, hk), tok),
            pl.BlockSpec((1, tm, hk), tok),
            pl.BlockSpec((1, tm, LANES), tok),
            pl.BlockSpec((1, 2 * GDN_HEADS, tm), lambda bb, tt: (bb, 0, tt)),
        ],
        out_shape=[act, act, act, act,
                   jax.ShapeDtypeStruct((b, t, LANES), F32),
                   jax.ShapeDtypeStruct((b, 2 * GDN_HEADS, t), F32)],
        scratch_shapes=[pltpu.VMEM((12, 8, 256), F32)],
        compiler_params=_params(("arbitrary", "arbitrary")),
        name="gdn_inproj",
    )(x, w_main, wabh, wabl, conv_w, alog, dtb, ltri)


def _bmm(a, b):
    return lax.dot_general(a, b, (((2,), (1,)), ((0,), (0,))), preferred_element_type=F32)


def _bmm_nt(a, b):
    return lax.dot_general(a, b, (((2,), (2,)), ((0,), (0,))), preferred_element_type=F32)


def _gdn_rec_kernel(q_ref, k_ref, v_ref, z_ref, gcb_ref, gcT_ref, nw_ref, o_ref, s_ref, *, nc):
    c = GDN_CHUNK
    nh = GDN_HEADS

    @pl.when(pl.program_id(1) == 0)
    def _():
        s_ref[...] = jnp.zeros_like(s_ref)

    row = lax.broadcasted_iota(jnp.int32, (nc, c, c), 1)
    col = lax.broadcasted_iota(jnp.int32, (nc, c, c), 2)
    eye = (row == col).astype(F32)
    nw = nw_ref[...]
    gcb = gcb_ref[0]
    gct = gcT_ref[0]

    u_h, wq_h, qkkd_h, dlast_h = [], [], [], []
    for h in range(nh):
        sl = slice(h * LANES, (h + 1) * LANES)
        q = q_ref[0, :, sl].reshape(nc, c, LANES)
        k = k_ref[0, :, sl].reshape(nc, c, LANES)
        v = v_ref[0, :, sl].astype(F32).reshape(nc, c, LANES)
        gc = gcb[:, h:h + 1].reshape(nc, c, 1)
        beta = gcb[:, nh + h:nh + h + 1].reshape(nc, c, 1)
        gc_row = gct[:, h:h + 1, :]
        g_last = gc[:, c - 1:c, :]
        decay = jnp.exp(jnp.where(row >= col, gc - gc_row, NEG))
        eg = jnp.exp(gc)
        kf = k.astype(F32)
        kb = kf * beta
        aq = _bmm_nt(jnp.concatenate([kb.astype(BF16), q], axis=1), k)
        a = jnp.where(row > col, aq[:, :c] * decay, 0.0)
        qk = aq[:, c:] * decay
        qpow = -a
        usum = eye + qpow
        qb = qpow.astype(BF16)
        qpow = _bmm(qb, qb)
        for lvl in range(5):
            qb = qpow.astype(BF16)
            if lvl < 4:
                prod = _bmm(jnp.concatenate([usum.astype(BF16), qb], axis=1), qb)
                usum = usum + prod[:, :c]
                qpow = prod[:, c:]
            else:
                usum = usum + _bmm(usum.astype(BF16), qb)
        rhs = jnp.concatenate([v * beta, kb * eg], axis=2).astype(BF16)
        uw = _bmm(usum.astype(BF16), rhs)
        u_h.append(uw[:, :, :LANES])
        wq_h.append(jnp.concatenate([uw[:, :, LANES:], q.astype(F32) * eg], axis=1).astype(BF16))
        k_dec = kf * jnp.exp(g_last - gc)
        k_dec_t = jnp.swapaxes(k_dec, 1, 2)
        qkkd_h.append(jnp.concatenate([qk, k_dec_t], axis=1).astype(BF16))
        dlast_h.append(jnp.exp(g_last))

    for i in range(nc):
        s_old = s_ref[...]
        lhs = jnp.stack([wq_h[h][i] for h in range(nh)])
        r = _bmm(lhs, s_old.astype(BF16))
        u = jnp.stack([u_h[h][i] for h in range(nh)])
        v_new = (u - r[:, :c]).astype(BF16)
        r2 = _bmm(jnp.stack([qkkd_h[h][i] for h in range(nh)]), v_new)
        o = r[:, c:] + r2[:, :c]
        dl = jnp.stack([dlast_h[h][i] for h in range(nh)])
        s_ref[...] = s_old * dl + r2[:, c:]
        ms = jnp.mean(o * o, axis=-1, keepdims=True)
        on = o * lax.rsqrt(ms + NORM_EPS) * nw
        for h in range(nh):
            z = z_ref[0, i * c:(i + 1) * c, h * LANES:(h + 1) * LANES].astype(F32)
            o_ref[0, i * c:(i + 1) * c, h * LANES:(h + 1) * LANES] = (on[h] * _silu(z)).astype(BF16)


def gdn_rec(q, k, v, z, gcb, gcbT, norm_w, nc=8):
    b, t, hk = q.shape
    n_chunks = t // GDN_CHUNK
    tb = nc * GDN_CHUNK
    gct4 = gcbT.reshape(b, 2 * GDN_HEADS, n_chunks, GDN_CHUNK).transpose(0, 2, 1, 3)
    spec = pl.BlockSpec((1, tb, hk), lambda bb, tt: (bb, tt, 0))
    return pl.pallas_call(
        functools.partial(_gdn_rec_kernel, nc=nc),
        grid=(b, t // tb),
        in_specs=[spec, spec, spec, spec,
                  pl.BlockSpec((1, tb, LANES), lambda bb, tt: (bb, tt, 0)),
                  pl.BlockSpec((1, nc, 2 * GDN_HEADS, GDN_CHUNK), lambda bb, tt: (bb, tt, 0, 0)),
                  pl.BlockSpec((1, LANES), lambda bb, tt: (0, 0))],
        out_specs=spec,
        out_shape=jax.ShapeDtypeStruct((b, t, hk), BF16),
        scratch_shapes=[pltpu.VMEM((GDN_HEADS, GDN_HEAD_DIM, GDN_HEAD_DIM), F32)],
        compiler_params=_params(("arbitrary", "arbitrary")),
        name="gdn_rec",
    )(q, k, v, z, gcb, gct4, norm_w.reshape(1, LANES).astype(F32))


GRP_LANE0 = MOE_EXPERTS


def _route(logits):
    lane = lax.broadcasted_iota(jnp.int32, logits.shape, 1)
    lanef = lane.astype(F32)
    far = float(LANES)
    is_grp = (lane >= GRP_LANE0) & (lane < GRP_LANE0 + MOE_GROUPS)
    lg = jnp.where(is_grp, logits, NEG)
    eg = jnp.exp(lg - jnp.max(lg, axis=-1, keepdims=True))
    pg = eg / jnp.sum(eg, axis=-1, keepdims=True)
    gp = jnp.max(pg, axis=-1, keepdims=True)
    gidx = jnp.min(jnp.where(is_grp & (pg == gp), lanef, far), axis=-1, keepdims=True) - float(GRP_LANE0)
    in_grp = (lane < MOE_EXPERTS) & (jnp.floor(lanef * (1.0 / MOE_EPG)) == gidx)
    le = jnp.where(in_grp, logits, NEG)
    ee = jnp.exp(le - jnp.max(le, axis=-1, keepdims=True))
    pe = ee / jnp.sum(ee, axis=-1, keepdims=True)
    p1 = jnp.max(jnp.where(in_grp, pe, -1.0), axis=-1, keepdims=True)
    i1 = jnp.min(jnp.where(in_grp & (pe == p1), lanef, far), axis=-1, keepdims=True)
    rest = in_grp & (lanef != i1)
    p2 = jnp.max(jnp.where(rest, pe, -1.0), axis=-1, keepdims=True)
    i2 = jnp.min(jnp.where(rest & (pe == p2), lanef, far), axis=-1, keepdims=True)
    scale = gp / (p1 + p2)
    return jnp.where(lanef == i1, p1 * scale, jnp.where(lanef == i2, p2 * scale, 0.0)), gidx


MOE_SUB = 128


def _mix_moe_ln_kernel(o_ref, wo_ref, res_ref, mg_ref, mb_ref, wrh_ref, wrl_ref, br_ref, ltri_ref, wgu_ref, wd_ref,
                       g_ref, b_ref, out_ref, h_ref, xs_ref, gs_ref, ys_ref, dest_ref, seg_ref, *, tm, slots):
    grp = pl.program_id(1)
    hid_w = MOE_EPG * MOE_HIDDEN
    lane = lax.broadcasted_iota(jnp.int32, (tm, LANES), 1)
    shift = MOE_SUB.bit_length() - 1

    @pl.when(grp == 0)
    def _():
        h = _layer_norm(DEEPNORM_ALPHA * res_ref[...] + _dot(o_ref[...], wo_ref[...]), mg_ref[...], mb_ref[...])
        h_ref[...] = h
        xb, xlo = _split2(h)
        wh = wrh_ref[...]
        logits = _dot(xb, wh) + (_dot(xlo, wh) + _dot(xb, wrl_ref[...])) + br_ref[...]
        gate, gidx = _route(logits)
        onehot = jnp.where(lane.astype(F32) == gidx, 1.0, 0.0)
        pos = _dot(ltri_ref[...], onehot.astype(BF16))
        cnt = jnp.sum(onehot, axis=0, keepdims=True).astype(jnp.int32)
        offs, off = [], 0
        for g in range(MOE_GROUPS):
            n_sub = lax.shift_right_logical(cnt[0, g] + (MOE_SUB - 1), shift)
            seg_ref[g] = off
            seg_ref[MOE_GROUPS + g] = n_sub
            offs.append(off)
            off = off + lax.shift_left(n_sub, shift)
        off_row = jnp.zeros((1, LANES), jnp.int32)
        for g in range(1, MOE_GROUPS):
            off_row = jnp.where(lane[0:1, :] == g, offs[g], off_row)
        dest = jnp.sum(onehot * (pos + off_row.astype(F32)), axis=-1, keepdims=True)
        dest_ref[...] = jnp.broadcast_to(dest, (tm, LANES))
        hi = jnp.floor(dest * (1.0 / 32.0))
        parts = jnp.where(lane == 0, hi, jnp.where(lane == 1, dest - 32.0 * hi, 0.0)).astype(BF16)
        pick = (lax.broadcasted_iota(jnp.int32, (8, LANES), 0) == lax.broadcasted_iota(jnp.int32, (8, LANES), 1))
        rows = _dot_nt(pick.astype(BF16), parts)
        dest_row = 32.0 * rows[0:1, :] + rows[1:2, :]
        slot = lax.broadcasted_iota(jnp.int32, (slots, tm), 0).astype(F32)
        perm = jnp.where(slot == dest_row, 1.0, 0.0).astype(BF16)
        xs_ref[...] = _dot(perm, xb).astype(BF16)
        gs_ref[...] = _dot(perm, gate.astype(BF16))
        ys_ref[...] = jnp.zeros_like(ys_ref)

    base = seg_ref[grp]
    lane_s = lax.broadcasted_iota(jnp.int32, (MOE_SUB, LANES), 1)

    def sub_tile(j, carry):
        r0 = pl.multiple_of(base + j * MOE_SUB, MOE_SUB)
        xb = xs_ref[pl.ds(r0, MOE_SUB), :]
        gate = gs_ref[pl.ds(r0, MOE_SUB), :]
        hids = []
        for e in range(MOE_EPG):
            ge = jnp.sum(jnp.where(lane_s == grp * MOE_EPG + e, gate, 0.0), axis=-1, keepdims=True)
            hg = _dot(xb, wgu_ref[0, :, e * MOE_HIDDEN:(e + 1) * MOE_HIDDEN])
            hu = _dot(xb, wgu_ref[0, :, hid_w + e * MOE_HIDDEN:hid_w + (e + 1) * MOE_HIDDEN])
            hids.append((_silu(hg) * hu * ge).astype(BF16))
        ys_ref[pl.ds(r0, MOE_SUB), :] = _dot(jnp.concatenate(hids, axis=1), wd_ref[0]).astype(BF16)
        return carry

    lax.fori_loop(0, seg_ref[MOE_GROUPS + grp], sub_tile, 0)

    @pl.when(grp == MOE_GROUPS - 1)
    def _():
        slot_l = lax.broadcasted_iota(jnp.int32, (tm, slots), 1).astype(F32)
        unperm = jnp.where(slot_l == dest_ref[:, 0:1], 1.0, 0.0).astype(BF16)
        y = _dot(unperm, ys_ref[...])
        out_ref[...] = _layer_norm(DEEPNORM_ALPHA * h_ref[...] + y, g_ref[...], b_ref[...])


def mix_moe_ln(o, w_o, res, mix_g, mix_b, w_grp, b_grp, w_rt, b_rt, w_gate, w_up, w_down, ln_g, ln_b, tm=512):
    n, d = res.shape
    kdim = o.shape[1]
    slots = tm + MOE_GROUPS * MOE_SUB
    wr = jnp.zeros((d, LANES), F32).at[:, :MOE_EXPERTS].set(w_rt).at[:, GRP_LANE0:GRP_LANE0 + MOE_GROUPS].set(w_grp)
    br = jnp.zeros((1, LANES), F32).at[0, :MOE_EXPERTS].set(b_rt).at[0, GRP_LANE0:GRP_LANE0 + MOE_GROUPS].set(b_grp)
    wrh = wr.astype(BF16)
    wrl = (wr - wrh.astype(F32)).astype(BF16)
    hid_w = MOE_EPG * MOE_HIDDEN
    by_group = lambda w: w.reshape(MOE_GROUPS, MOE_EPG, d, MOE_HIDDEN).transpose(0, 2, 1, 3).reshape(MOE_GROUPS, d, hid_w)
    wgu = jnp.concatenate([by_group(w_gate), by_group(w_up)], axis=-1).astype(BF16)
    wd = w_down.reshape(MOE_GROUPS, hid_w, d).astype(BF16)
    r = jnp.arange(tm)
    ltri = (r[:, None] > r[None, :]).astype(BF16)
    row = lambda i, g: (i, 0)
    const = lambda i, g: (0, 0)
    return pl.pallas_call(
        functools.partial(_mix_moe_ln_kernel, tm=tm, slots=slots),
        grid=(n // tm, MOE_GROUPS),
        in_specs=[pl.BlockSpec((tm, kdim), row), pl.BlockSpec((kdim, d), const), pl.BlockSpec((tm, d), row),
                  pl.BlockSpec((1, d), const), pl.BlockSpec((1, d), const),
                  pl.BlockSpec((d, LANES), const), pl.BlockSpec((d, LANES), const), pl.BlockSpec((1, LANES), const),
                  pl.BlockSpec((tm, tm), const),
                  pl.BlockSpec((1, d, 2 * hid_w), lambda i, g: (g, 0, 0)),
                  pl.BlockSpec((1, hid_w, d), lambda i, g: (g, 0, 0)),
                  pl.BlockSpec((1, d), const), pl.BlockSpec((1, d), const)],
        out_specs=pl.BlockSpec((tm, d), row),
        out_shape=jax.ShapeDtypeStruct((n, d), F32),
        scratch_shapes=[pltpu.VMEM((tm, d), F32),
                        pltpu.VMEM((slots, d), BF16), pltpu.VMEM((slots, LANES), F32), pltpu.VMEM((slots, d), BF16),
                        pltpu.VMEM((tm, LANES), F32), pltpu.SMEM((2 * MOE_GROUPS,), jnp.int32)],
        compiler_params=_params(("arbitrary", "arbitrary")),
        name="mix_moe_ln",
    )(o, w_o.astype(BF16), res, mix_g.reshape(1, d), mix_b.reshape(1, d), wrh, wrl, br, ltri, wgu, wd,
      ln_g.reshape(1, d), ln_b.reshape(1, d))


KV_W = NSA_GROUPS * NSA_HEAD_DIM
GATE_ROWS = 16


def _nsa_proj_kernel(h_ref, wq_ref, wg_ref, wc_ref, wks_ref, wkw_ref, wvs_ref, wvw_ref,
                     qT_ref, gT_ref, xk_ref, xv_ref, ksa_ref, kwa_ref, vsT_ref, vwT_ref, cmp_scr, *, tm):
    t0 = pl.program_id(1) * tm
    hb = h_ref[0].astype(BF16)
    qT_ref[0] = _dot_nt(wq_ref[...], hb).astype(BF16)
    gT_ref[0] = jax.nn.sigmoid(_dot_nt(wg_ref[...], hb))
    craw = _dot(hb, wc_ref[...])
    n_cb = 2 * KV_W // LANES
    for cb in range(n_cb):
        cmp_scr[cb] = craw[:, cb * LANES:(cb + 1) * LANES]
    nrow = tm // CMP_STRIDE
    lane_r = lax.broadcasted_iota(jnp.int32, (nrow, LANES), 1)
    for j in range(CMP_STRIDE // 2):
        for cb in range(n_cb):
            a = cmp_scr[cb, pl.ds(2 * j, nrow, stride=CMP_STRIDE), :]
            bm = cmp_scr[cb, pl.ds(2 * j + 1, nrow, stride=CMP_STRIDE), :]
            pieces = (jnp.where(lane_r < NSA_HEAD_DIM, a, pltpu.roll(bm, NSA_HEAD_DIM, 1)),
                      jnp.where(lane_r < NSA_HEAD_DIM, pltpu.roll(a, NSA_HEAD_DIM, 1), bm))
            for k, val in enumerate(pieces):
                pg = 2 * cb + k
                ref = xk_ref if pg < NSA_GROUPS else xv_ref
                ref[0, pg % NSA_GROUPS, :, j * LANES:(j + 1) * LANES] = val.astype(BF16)
    vsT_ref[0] = _dot_nt(wvs_ref[...], hb).astype(BF16)
    vwT_ref[0] = _dot_nt(wvw_ref[...], hb).astype(BF16)
    ks = _dot(hb, wks_ref[...])
    kw = _dot(hb, wkw_ref[...])
    lane = lax.broadcasted_iota(jnp.int32, (tm, LANES), 1)
    blk = (t0 + lax.broadcasted_iota(jnp.int32, (tm, LANES), 0)) // SEL_BLOCK
    onehot = jnp.where(lane - NSA_HEAD_DIM == blk, 1.0, 0.0)
    for g in range(NSA_GROUPS):
        ksg = ks[:, g * LANES:(g + 1) * LANES]
        ksa_ref[0, g] = jnp.where(lane < NSA_HEAD_DIM, ksg, onehot).astype(BF16)
        kwa_ref[0, g] = kw[:, g * LANES:(g + 1) * LANES].astype(BF16)


def _pad_group_cols(w):
    d = w.shape[0]
    w4 = w.reshape(d, NSA_GROUPS, NSA_HEAD_DIM)
    return jnp.concatenate([w4, jnp.zeros_like(w4)], axis=-1).reshape(d, NSA_GROUPS * LANES)


def nsa_proj(h, w_kv, w_in, tm=256):
    b, t, d = h.shape
    scale = NSA_HEAD_DIM ** -0.5 * LOG2E
    wq = (w_in[:, :NSA_WIDTH] * scale).T.astype(BF16)
    wgate = w_in[:, NSA_WIDTH:].reshape(d, NSA_GROUPS, NSA_HPG, N_BRANCHES)
    wgate = wgate.transpose(0, 1, 3, 2).reshape(d, NSA_GROUPS, N_BRANCHES * NSA_HPG)
    wgate = jnp.concatenate([wgate, jnp.zeros((d, NSA_GROUPS, GATE_ROWS - N_BRANCHES * NSA_HPG), F32)], axis=-1)
    wg = wgate.reshape(d, NSA_GROUPS * GATE_ROWS).T.astype(BF16)
    part = lambda p: w_kv[:, p * KV_W:(p + 1) * KV_W]
    wc = w_kv[:, :2 * KV_W].astype(BF16)
    wks = _pad_group_cols(part(2)).astype(BF16)
    wvs = part(3).T.astype(BF16)
    wkw = _pad_group_cols(part(4)).astype(BF16)
    wvw = part(5).T.astype(BF16)
    const = lambda bb, tt: (0, 0)
    tok = lambda bb, tt: (bb, tt, 0)
    tr = lambda bb, tt: (bb, 0, tt)
    g4 = lambda bb, tt: (bb, 0, tt, 0)
    ng = NSA_GROUPS * GATE_ROWS
    wide = CMP_STRIDE * NSA_HEAD_DIM
    return pl.pallas_call(
        functools.partial(_nsa_proj_kernel, tm=tm),
        grid=(b, t // tm),
        in_specs=[pl.BlockSpec((1, tm, d), tok),
                  pl.BlockSpec((NSA_WIDTH, d), const), pl.BlockSpec((ng, d), const),
                  pl.BlockSpec((d, 2 * KV_W), const),
                  pl.BlockSpec((d, NSA_GROUPS * LANES), const), pl.BlockSpec((d, NSA_GROUPS * LANES), const),
                  pl.BlockSpec((KV_W, d), const), pl.BlockSpec((KV_W, d), const)],
        out_specs=[pl.BlockSpec((1, NSA_WIDTH, tm), tr), pl.BlockSpec((1, ng, tm), tr),
                   pl.BlockSpec((1, NSA_GROUPS, tm // CMP_STRIDE, wide), g4),
                   pl.BlockSpec((1, NSA_GROUPS, tm // CMP_STRIDE, wide), g4),
                   pl.BlockSpec((1, NSA_GROUPS, tm, LANES), g4), pl.BlockSpec((1, NSA_GROUPS, tm, LANES), g4),
                   pl.BlockSpec((1, KV_W, tm), tr), pl.BlockSpec((1, KV_W, tm), tr)],
        out_shape=[jax.ShapeDtypeStruct((b, NSA_WIDTH, t), BF16), jax.ShapeDtypeStruct((b, ng, t), F32),
                   jax.ShapeDtypeStruct((b, NSA_GROUPS, t // CMP_STRIDE, wide), BF16),
                   jax.ShapeDtypeStruct((b, NSA_GROUPS, t // CMP_STRIDE, wide), BF16),
                   jax.ShapeDtypeStruct((b, NSA_GROUPS, t, LANES), BF16),
                   jax.ShapeDtypeStruct((b, NSA_GROUPS, t, LANES), BF16),
                   jax.ShapeDtypeStruct((b, KV_W, t), BF16), jax.ShapeDtypeStruct((b, KV_W, t), BF16)],
        scratch_shapes=[pltpu.VMEM((2 * KV_W // LANES, tm, LANES), F32)],
        compiler_params=_params(("arbitrary", "arbitrary")),
        name="nsa_proj",
    )(h, wq, wg, wc, wks, wkw, wvs, wvw)


N_CMP_PAD = 128


def _nsa_compress_kernel(xk_ref, xv_ref, pk_ref, pv_ref, w1k_ref, w1v_ref, w2k_ref, w2vT_ref, kc_ref, vcT_ref):
    half = CMP_STRIDE * NSA_HEAD_DIM
    pbk = _dot(pk_ref[...], w1k_ref[...])[0:1, :]
    pbv = _dot(pv_ref[...], w1v_ref[...])[0:1, :]
    for g in range(NSA_GROUPS):
        xk = xk_ref[0, g]
        hk = _dot(xk, w1k_ref[:half, :]) + pltpu.roll(_dot(xk, w1k_ref[half:, :]), N_CMP_PAD - 1, 0) + pbk
        kc_ref[0, g] = _dot(_silu(hk).astype(BF16), w2k_ref[...]).astype(BF16)
        xv = xv_ref[0, g]
        hv = _dot(xv, w1v_ref[:half, :]) + pltpu.roll(_dot(xv, w1v_ref[half:, :]), N_CMP_PAD - 1, 0) + pbv
        vcT_ref[0, g] = _dot_nt(w2vT_ref[...], _silu(hv).astype(BF16)).astype(BF16)


def nsa_compress(xk, xv, k_pos, k_w1, k_w2, v_pos, v_w1, v_w2):
    b, _, nrow, wide = xk.shape
    assert nrow == N_CMP_PAD
    pad_pos = lambda p: jnp.zeros((8, CMP_BLOCK * NSA_HEAD_DIM), F32).at[0].set(p.reshape(-1)).astype(BF16)
    spec_x = pl.BlockSpec((1, NSA_GROUPS, nrow, wide), lambda bb: (bb, 0, 0, 0))
    c2 = lambda bb: (0, 0)
    return pl.pallas_call(
        _nsa_compress_kernel,
        grid=(b,),
        in_specs=[spec_x, spec_x,
                  pl.BlockSpec((8, 2 * wide), c2), pl.BlockSpec((8, 2 * wide), c2),
                  pl.BlockSpec((2 * wide, CMP_HIDDEN), c2), pl.BlockSpec((2 * wide, CMP_HIDDEN), c2),
                  pl.BlockSpec((CMP_HIDDEN, NSA_HEAD_DIM), c2), pl.BlockSpec((NSA_HEAD_DIM, CMP_HIDDEN), c2)],
        out_specs=[pl.BlockSpec((1, NSA_GROUPS, N_CMP_PAD, NSA_HEAD_DIM), lambda bb: (bb, 0, 0, 0)),
                   pl.BlockSpec((1, NSA_GROUPS, NSA_HEAD_DIM, N_CMP_PAD), lambda bb: (bb, 0, 0, 0))],
        out_shape=[jax.ShapeDtypeStruct((b, NSA_GROUPS, N_CMP_PAD, NSA_HEAD_DIM), BF16),
                   jax.ShapeDtypeStruct((b, NSA_GROUPS, NSA_HEAD_DIM, N_CMP_PAD), BF16)],
        compiler_params=_params(("arbitrary",)),
        name="nsa_compress",
    )(xk, xv, pad_pos(k_pos), pad_pos(v_pos), k_w1.astype(BF16), v_w1.astype(BF16),
      k_w2.astype(BF16), v_w2.T.astype(BF16))


SEL_CHUNK = 256
WIN_CHUNK = 128
TB_ROWS = WINDOW + SEL_CHUNK
TBC_ROWS = 256
TBC_OFF = 120
N_SEL_BLOCKS = 32
QW = NSA_HPG * Q_BLOCK


def _t5_bucket(dist):
    n = jnp.maximum(dist, 0)
    max_exact = REL_BUCKETS // 2
    nf = jnp.maximum(n, 1).astype(F32)
    large = max_exact + (jnp.log(nf / max_exact) / math.log(REL_MAX_DIST / max_exact)
                         * (REL_BUCKETS - max_exact)).astype(jnp.int32)
    large = jnp.minimum(large, REL_BUCKETS - 1)
    return jnp.where(n < max_exact, n, large)


def _bias_tables(rel_bias):
    rel = rel_bias.astype(F32)
    far = rel[REL_BUCKETS - 1]
    ql = jnp.arange(Q_BLOCK)
    heads = jnp.arange(NSA_HEADS).reshape(NSA_GROUPS, NSA_HPG)

    def table(dist, valid):
        bucket = _t5_bucket(dist)[None, :, None, :]
        relc = ((rel - far) * LOG2E)[:, heads][:, :, None, :, None]
        bias = jnp.zeros((NSA_GROUPS, dist.shape[0], NSA_HPG, Q_BLOCK), F32)
        for b in range(REL_BUCKETS):
            bias = jnp.where(bucket == b, relc[b], bias)
        bias = jnp.where(valid[None, :, None, :], bias, NEG)
        return bias.reshape(NSA_GROUPS, dist.shape[0], QW)

    d = ql[None, :] - (jnp.arange(TB_ROWS)[:, None] - WINDOW)
    tb = table(d, (d >= 0) & (d < WINDOW))
    m = jnp.arange(TBC_ROWS)[:, None] - TBC_OFF
    dc = ql[None, :] - CMP_STRIDE * m - (CMP_BLOCK - 1)
    tbc = table(dc, dc >= 0)
    return tb, tbc


def _nsa_attn_kernel(qT_ref, gT_ref, kc_ref, vcT_ref, ksa_ref, vsT_ref, kwa_ref, vwT_ref, tb_ref, tbc_ref, ovl_ref,
                     o_ref, m_ref, l_ref, acc_ref, ot_ref, oT_ref):
    c = pl.program_id(1)
    ng, hpg, hd = NSA_GROUPS, NSA_HPG, NSA_HEAD_DIM
    nsel = N_SEL_BLOCKS
    n_top = min(TOP_N, nsel)
    n_win = WINDOW // WIN_CHUNK

    q4 = jnp.stack([jnp.concatenate([qT_ref[0, (g * hpg + hh) * hd:(g * hpg + hh + 1) * hd, :]
                                     for hh in range(hpg)], axis=1) for g in range(ng)])
    gates = gT_ref[0]

    def gate3(br):
        return jnp.stack([jnp.concatenate(
            [gates[g * GATE_ROWS + br * hpg + hh:g * GATE_ROWS + br * hpg + hh + 1, :] for hh in range(hpg)], axis=1)
            for g in range(ng)])

    def fresh():
        return (jnp.full((ng, 1, QW), NEG, F32), jnp.zeros((ng, 1, QW), F32), jnp.zeros((ng, hd, QW), F32))

    def load_state():
        return m_ref[...], l_ref[...], acc_ref[...]

    def store_state(state):
        m_ref[...], l_ref[...], acc_ref[...] = state

    def upd(state, s, vT):
        m_old, l_old, acc_old = state
        m_new = jnp.maximum(m_old, jnp.max(s, axis=1, keepdims=True))
        alpha = jnp.exp2(m_old - m_new)
        p = jnp.exp2(s - m_new)
        return (m_new, l_old * alpha + jnp.sum(p, axis=1, keepdims=True),
                acc_old * alpha + _bmm(vT, p.astype(BF16)))

    def result(state, br):
        _, l_fin, acc_fin = state
        return gate3(br) * (acc_fin * (1.0 / l_fin))

    start = pl.multiple_of(TBC_OFF - 8 * c, 8)
    tbc = tbc_ref[:, pl.ds(start, N_CMP_PAD), :]
    nrow = lax.broadcasted_iota(jnp.int32, tbc.shape, 1)
    tbc = jnp.where(nrow < N_CMP_PAD - 1, tbc, NEG)
    s = _bmm(kc_ref[0], q4) + tbc
    valid = tbc > 0.5 * NEG
    e = jnp.where(valid, jnp.exp2(s - jnp.max(s, axis=1, keepdims=True)), 0.0)
    l = jnp.sum(e, axis=1, keepdims=True)
    p = e * (1.0 / jnp.where(l > 0.0, l, 1.0))
    o_cmp = gate3(0) * _bmm(vcT_ref[0], p.astype(BF16))

    psum = p[:, :, 0:Q_BLOCK]
    for hh in range(1, hpg):
        psum = psum + p[:, :, hh * Q_BLOCK:(hh + 1) * Q_BLOCK]
    psum = jnp.concatenate([psum[g] for g in range(ng)], axis=1)
    ph, plo = _split2(psum)
    ovl = ovl_ref[...]
    imp = _dot(ovl, ph) + _dot(ovl, plo)
    jidx = lax.broadcasted_iota(jnp.int32, imp.shape, 0)
    qlane = lax.broadcasted_iota(jnp.int32, imp.shape, 1) & (Q_BLOCK - 1)
    jq = 2 * c + (qlane >= SEL_BLOCK).astype(jnp.int32)
    forced = (jidx == 0) | (jidx == jq) | (jidx == jq - 1)
    imp = jnp.where(forced, -NEG, jnp.where(jidx <= jq, imp, NEG))
    sub8 = lax.broadcasted_iota(jnp.int32, (8, imp.shape[1]), 0)
    rows = [imp[8 * v:8 * v + 8, :] for v in range(nsel // 8)]
    cnts = [jnp.zeros_like(r) for r in rows]
    for i in range(nsel):
        r = imp[i:i + 1, :]
        for v in range(nsel // 8):
            ge = jnp.where(r >= rows[v], 1.0, 0.0)
            gt = jnp.where(r > rows[v], 1.0, 0.0)
            if i < 8 * v:
                ahead = ge
            elif i >= 8 * v + 8:
                ahead = gt
            else:
                ahead = jnp.where(sub8 > i - 8 * v, ge, gt)
            cnts[v] = cnts[v] + ahead
    cnt = jnp.concatenate(cnts, axis=0)
    selm = jnp.where(cnt < float(n_top), 0.0, NEG).astype(BF16)
    pad = jnp.zeros((LANES - hd - nsel, QW), BF16)
    q_aug = jnp.stack([jnp.concatenate(
        [q4[g], jnp.concatenate([selm[:, g * Q_BLOCK:(g + 1) * Q_BLOCK]] * hpg, axis=1), pad], axis=0)
        for g in range(ng)])

    def sel_scores(k0):
        return _bmm(ksa_ref[0, :, pl.ds(k0, SEL_CHUNK), :], q_aug)

    def sel_values(k0):
        return vsT_ref[0, :, pl.ds(k0, SEL_CHUNK)].reshape(ng, hd, SEL_CHUNK)

    def sel_near(i):
        k0 = pl.multiple_of(i * SEL_CHUNK, SEL_CHUNK)
        tstart = pl.multiple_of(WINDOW - (c * Q_BLOCK - k0), Q_BLOCK)
        return sel_scores(k0) + tb_ref[:, pl.ds(tstart, SEL_CHUNK), :], sel_values(k0)

    def win(first, n_chunks, tab_chunk):
        width = n_chunks * WIN_CHUNK
        k0 = pl.multiple_of((c - n_win + first) * WIN_CHUNK, WIN_CHUNK)
        s_w = _bmm(kwa_ref[0, :, pl.ds(k0, width), :], q_aug)
        if tab_chunk is not None:
            lo = (tab_chunk - first) * WIN_CHUNK
            part = s_w[:, lo:lo + WIN_CHUNK] + tb_ref[:, tab_chunk * WIN_CHUNK:(tab_chunk + 1) * WIN_CHUNK, :]
            pieces = ([s_w[:, :lo]] if lo else []) + [part] + ([s_w[:, lo + WIN_CHUNK:]] if lo + WIN_CHUNK < width else [])
            s_w = jnp.concatenate(pieces, axis=1) if len(pieces) > 1 else part
        return s_w, vwT_ref[0, :, pl.ds(k0, width)].reshape(ng, hd, width)

    store_state(fresh())

    def far_body(i, carry):
        k0 = pl.multiple_of(i * SEL_CHUNK, SEL_CHUNK)
        store_state(upd(load_state(), sel_scores(k0), sel_values(k0)))
        return carry

    lax.fori_loop(0, jnp.maximum(c // 2 - 1, 0), far_body, 0)

    @pl.when(c >= n_win)
    def _():
        st_s = upd(load_state(), *sel_near(c // 2 - 1))
        st_w = upd(fresh(), *win(0, 2, 0))
        st_s = upd(st_s, *sel_near(c // 2))
        st_w = upd(st_w, *win(2, 2, 3))
        st_w = upd(st_w, *win(4, 1, 4))
        ot_ref[...] = o_cmp + result(st_s, 1) + result(st_w, 2)

    @pl.when(c < n_win)
    def _():
        @pl.when(c >= 2)
        def _():
            store_state(upd(load_state(), *sel_near(c // 2 - 1)))

        ot_ref[...] = o_cmp + result(upd(load_state(), *sel_near(c // 2)), 1)
        store_state(fresh())
        pl.when(c == 3)(lambda: store_state(upd(load_state(), *win(1, 1, None))))
        pl.when(c >= 2)(lambda: store_state(upd(load_state(), *win(2, 2, 3))))
        pl.when(c == 1)(lambda: store_state(upd(load_state(), *win(3, 1, 3))))
        ot_ref[...] += result(upd(load_state(), *win(4, 1, 4)), 2)

    for g in range(ng):
        for hh in range(hpg):
            oT_ref[hh, g * hd:(g + 1) * hd, :] = ot_ref[g, :, hh * Q_BLOCK:(hh + 1) * Q_BLOCK]
    for hh in range(hpg):
        o_ref[0, :, hh * KV_W:(hh + 1) * KV_W] = oT_ref[hh].T.astype(BF16)


def nsa_attn(qT, gT, kc, vcT, ksa, vsT, kwa, vwT, rel_bias):
    b, _, t = qT.shape
    nq = t // Q_BLOCK
    tb, tbc = _bias_tables(rel_bias)
    nc = (t - CMP_BLOCK) // CMP_STRIDE + 1
    cs = jnp.arange(N_CMP_PAD) * CMP_STRIDE
    ss = jnp.arange(N_SEL_BLOCKS) * SEL_BLOCK
    ovl = ((cs[None, :] < ss[:, None] + SEL_BLOCK) & (cs[None, :] + CMP_BLOCK - 1 >= ss[:, None])
           & (jnp.arange(N_CMP_PAD)[None, :] < nc)).astype(BF16)
    ng = NSA_GROUPS * GATE_ROWS
    per_b3 = lambda bb, cc: (bb, 0, 0)
    per_b4 = lambda bb, cc: (bb, 0, 0, 0)
    c3 = lambda bb, cc: (0, 0, 0)
    return pl.pallas_call(
        _nsa_attn_kernel,
        grid=(b, nq),
        in_specs=[pl.BlockSpec((1, NSA_WIDTH, Q_BLOCK), lambda bb, cc: (bb, 0, cc)),
                  pl.BlockSpec((1, ng, Q_BLOCK), lambda bb, cc: (bb, 0, cc)),
                  pl.BlockSpec((1, NSA_GROUPS, N_CMP_PAD, NSA_HEAD_DIM), per_b4),
                  pl.BlockSpec((1, NSA_GROUPS, NSA_HEAD_DIM, N_CMP_PAD), per_b4),
                  pl.BlockSpec((1, NSA_GROUPS, t, LANES), per_b4),
                  pl.BlockSpec((1, KV_W, t), per_b3),
                  pl.BlockSpec((1, NSA_GROUPS, t, LANES), per_b4),
                  pl.BlockSpec((1, KV_W, t), per_b3),
                  pl.BlockSpec((NSA_GROUPS, TB_ROWS, QW), c3),
                  pl.BlockSpec((NSA_GROUPS, TBC_ROWS, QW), c3),
                  pl.BlockSpec((N_SEL_BLOCKS, N_CMP_PAD), lambda bb, cc: (0, 0))],
        out_specs=pl.BlockSpec((1, Q_BLOCK, NSA_WIDTH), lambda bb, cc: (bb, cc, 0)),
        out_shape=jax.ShapeDtypeStruct((b, t, NSA_WIDTH), BF16),
        scratch_shapes=[pltpu.VMEM((NSA_GROUPS, 1, QW), F32), pltpu.VMEM((NSA_GROUPS, 1, QW), F32),
                        pltpu.VMEM((NSA_GROUPS, NSA_HEAD_DIM, QW), F32),
                        pltpu.VMEM((NSA_GROUPS, NSA_HEAD_DIM, QW), F32),
                        pltpu.VMEM((NSA_HPG, KV_W, Q_BLOCK), F32)],
        compiler_params=_params(("arbitrary", "arbitrary")),
        name="nsa_attn",
    )(qT, gT, kc, vcT, ksa, vsT, kwa, vwT, tb, tbc, ovl)


def kernel(x, gdn_w_in, gdn_conv_w, gdn_a_log, gdn_dt_bias, gdn_norm_w, gdn_w_o, nsa_w_kv, cmp_k_pos, cmp_k_w1,
           cmp_k_w2, cmp_v_pos, cmp_v_w1, cmp_v_w2, nsa_w_in, nsa_w_o, rel_bias, ln_mix_g, ln_mix_b, ln_ffn_g,
           ln_ffn_b, moe_w_grp, moe_b_grp, moe_w_rt, moe_b_rt, moe_w_gate, moe_w_up, moe_w_down):
    b, t, d = x.shape
    n = b * t

    def mix_ffn(o, w_o, res, layer):
        return mix_moe_ln(o, w_o, res, ln_mix_g[layer], ln_mix_b[layer], moe_w_grp[layer], moe_b_grp[layer],
                          moe_w_rt[layer], moe_b_rt[layer], moe_w_gate[layer], moe_w_up[layer], moe_w_down[layer],
                          ln_ffn_g[layer], ln_ffn_b[layer])

    q, k, v, z, gcb, gcbT = gdn_inproj(x, gdn_w_in[0], gdn_conv_w[0], gdn_a_log[0], gdn_dt_bias[0])
    o = gdn_rec(q, k, v, z, gcb, gcbT, gdn_norm_w[0])
    h = mix_ffn(o.reshape(n, GDN_WIDTH), gdn_w_o[0], x.reshape(n, d), 0)

    qT, gT, xk, xv, ksa, kwa, vsT, vwT = nsa_proj(h.reshape(b, t, d), nsa_w_kv, nsa_w_in[0])
    kc, vcT = nsa_compress(xk, xv, cmp_k_pos, cmp_k_w1, cmp_k_w2, cmp_v_pos, cmp_v_w1, cmp_v_w2)
    o = nsa_attn(qT, gT, kc, vcT, ksa, vsT, kwa, vwT, rel_bias)
    w_o = nsa_w_o[0].reshape(NSA_GROUPS, NSA_HPG, NSA_HEAD_DIM, d).transpose(1, 0, 2, 3).reshape(NSA_WIDTH, d)
    h = mix_ffn(o.reshape(n, NSA_WIDTH), w_o, h, 1)
    return h.reshape(b, t, d)
```

```python
import functools
import math

import jax
import jax.numpy as jnp
from jax import lax
from jax.experimental import pallas as pl
from jax.experimental.pallas import tpu as pltpu

F32 = jnp.float32
BF16 = jnp.bfloat16

D_MODEL = 1024
DEPTH = 2
GDN_HEADS = 8
GDN_HEAD_DIM = 128
GDN_WIDTH = GDN_HEADS * GDN_HEAD_DIM
GDN_CONV = 4
GDN_CHUNK = 64
NSA_HEADS = 16
NSA_GROUPS = 4
NSA_HPG = NSA_HEADS // NSA_GROUPS
NSA_HEAD_DIM = 64
NSA_WIDTH = NSA_HEADS * NSA_HEAD_DIM
CMP_BLOCK = 32
CMP_STRIDE = 16
CMP_HIDDEN = 2 * NSA_HEAD_DIM
SEL_BLOCK = 64
TOP_N = 16
WINDOW = 512
Q_BLOCK = 128
N_BRANCHES = 3
REL_BUCKETS = 32
REL_MAX_DIST = 128
MOE_GROUPS = 4
MOE_EPG = 4
MOE_EXPERTS = MOE_GROUPS * MOE_EPG
MOE_HIDDEN = 256
DEEPNORM_ALPHA = (2 * DEPTH) ** 0.25
LN_EPS = 1e-5
NORM_EPS = 1e-6
NEG = -1e30
LOG2E = math.log2(math.e)

LANES = 128
VMEM_LIMIT = 56 * 1024 * 1024


def _dot(a, b):
    return jnp.dot(a, b, preferred_element_type=F32)


def _dot_nt(a, b):
    return lax.dot_general(a, b, (((1,), (1,)), ((), ())), preferred_element_type=F32)


def _dot_tn(a, b):
    return lax.dot_general(a, b, (((0,), (0,)), ((), ())), preferred_element_type=F32)


def _split2(x):
    hi = x.astype(BF16)
    lo = (x - hi.astype(F32)).astype(BF16)
    return hi, lo


def _silu(x):
    return x * jax.nn.sigmoid(x)


def _params(sem):
    return pltpu.CompilerParams(dimension_semantics=sem, vmem_limit_bytes=VMEM_LIMIT)


def _layer_norm(x, g, b):
    mu = jnp.mean(x, axis=-1, keepdims=True)
    xc = x - mu
    var = jnp.mean(xc * xc, axis=-1, keepdims=True)
    return xc * lax.rsqrt(var + LN_EPS) * g + b


def _gdn_inproj_kernel(x_ref, w_ref, wabh_ref, wabl_ref, cw_ref, alog_ref, dtb_ref, ltri_ref,
                       q_ref, k_ref, v_ref, z_ref, gcb_ref, gcbT_ref, carry_ref, *, tm):
    @pl.when(pl.program_id(1) == 0)
    def _():
        carry_ref[...] = jnp.zeros_like(carry_ref)

    x = x_ref[0]
    xb, xlo = _split2(x)

    wh = wabh_ref[...]
    ab = _dot(xb, wh) + (_dot(xlo, wh) + _dot(xb, wabl_ref[...]))
    lane = lax.broadcasted_iota(jnp.int32, ab.shape, 1)
    sp_in = ab + dtb_ref[...]
    softplus = jnp.maximum(sp_in, 0.0) + jnp.log1p(jnp.exp(-jnp.abs(sp_in)))
    g = jnp.where(lane < GDN_HEADS, -jnp.exp(alog_ref[...]) * softplus, 0.0)
    beta = jax.nn.sigmoid(ab)
    g1 = g.astype(BF16)
    r1 = g - g1.astype(F32)
    g2 = r1.astype(BF16)
    g3 = (r1 - g2.astype(F32)).astype(BF16)
    ltri = ltri_ref[...]
    gc = _dot(ltri, g1) + (_dot(ltri, g2) + _dot(ltri, g3))
    gcb = jnp.where(lane < GDN_HEADS, gc, jnp.where(lane < 2 * GDN_HEADS, beta, 0.0))
    gcb_ref[0] = gcb
    gcbT_ref[0] = gcb.T[:2 * GDN_HEADS, :]

    row8 = lax.broadcasted_iota(jnp.int32, (8, 256), 0)
    outs = (q_ref, k_ref, v_ref)
    for s in range(3):
        for cc in range(4):
            col = s * GDN_WIDTH + cc * 256
            y = _dot(xb, w_ref[:, col:col + 256])
            prev = carry_ref[s * 4 + cc]
            carry_ref[s * 4 + cc] = y[tm - 8:, :]
            cw = cw_ref[:, col:col + 256]
            acc = y * cw[3:4, :]
            for kk in range(1, GDN_CONV):
                ry = pltpu.roll(y, kk, 0)
                rp = pltpu.roll(prev, kk, 0)
                head = jnp.where(row8 < kk, rp, ry[:8, :])
                shifted = jnp.concatenate([head, ry[8:, :]], axis=0)
                acc = acc + shifted * cw[3 - kk:4 - kk, :]
            a = _silu(acc)
            if s < 2:
                halves = []
                for hh in range(2):
                    ah = a[:, hh * LANES:(hh + 1) * LANES]
                    ss = jnp.sum(ah * ah, axis=-1, keepdims=True)
                    scale = lax.rsqrt(ss + NORM_EPS)
                    if s == 0:
                        scale = scale * (GDN_HEAD_DIM ** -0.5)
                    halves.append(ah * scale)
                a = jnp.concatenate(halves, axis=1)
            outs[s][0, :, cc * 256:(cc + 1) * 256] = a.astype(BF16)
    for cc in range(4):
        col = 3 * GDN_WIDTH + cc * 256
        z_ref[0, :, cc * 256:(cc + 1) * 256] = _dot(xb, w_ref[:, col:col + 256]).astype(BF16)


def gdn_inproj(x, w_in, conv_w, a_log, dt_bias, tm=256):
    b, t, d = x.shape
    hk = GDN_WIDTH
    w_main = w_in[:, :4 * hk].astype(BF16)
    w_ab = jnp.zeros((d, LANES), F32).at[:, :2 * GDN_HEADS].set(w_in[:, 4 * hk:])
    wabh = w_ab.astype(BF16)
    wabl = (w_ab - wabh.astype(F32)).astype(BF16)
    alog = jnp.zeros((1, LANES), F32).at[0, :GDN_HEADS].set(a_log)
    dtb = jnp.zeros((1, LANES), F32).at[0, :GDN_HEADS].set(dt_bias)
    r = jnp.arange(tm)
    ltri = ((r[:, None] // GDN_CHUNK == r[None, :] // GDN_CHUNK) & (r[:, None] >= r[None, :])).astype(BF16)
    tok = lambda bb, tt: (bb, tt, 0)
    const2 = lambda bb, tt: (0, 0)
    act = jax.ShapeDtypeStruct((b, t, hk), BF16)
    return pl.pallas_call(
        functools.partial(_gdn_inproj_kernel, tm=tm),
        grid=(b, t // tm),
        in_specs=[
            pl.BlockSpec((1, tm, d), tok),
            pl.BlockSpec((d, 4 * hk), const2),
            pl.BlockSpec((d, LANES), const2),
            pl.BlockSpec((d, LANES), const2),
            pl.BlockSpec((GDN_CONV, 3 * hk), const2),
            pl.BlockSpec((1, LANES), const2),
            pl.BlockSpec((1, LANES), const2),
            pl.BlockSpec((tm, tm), const2),
        ],
        out_specs=[
            pl.BlockSpec((1, tm, hk), tok),
            pl.BlockSpec((1, tm, hk), tok),
            pl.BlockSpec((1, tm, hk), tok),
            pl.BlockSpec((1, tm, hk), tok),
            pl.BlockSpec((1, tm, LANES), tok),
            pl.BlockSpec((1, 2 * GDN_HEADS, tm), lambda bb, tt: (bb, 0, tt)),
        ],
        out_shape=[act, act, act, act,
                   jax.ShapeDtypeStruct((b, t, LANES), F32),
                   jax.ShapeDtypeStruct((b, 2 * GDN_HEADS, t), F32)],
        scratch_shapes=[pltpu.VMEM((12, 8, 256), F32)],
        compiler_params=_params(("arbitrary", "arbitrary")),
        name="gdn_inproj",
    )(x, w_main, wabh, wabl, conv_w, alog, dtb, ltri)


def _bmm(a, b):
    return lax.dot_general(a, b, (((2,), (1,)), ((0,), (0,))), preferred_element_type=F32)


def _bmm_nt(a, b):
    return lax.dot_general(a, b, (((2,), (2,)), ((0,), (0,))), preferred_element_type=F32)


def _gdn_rec_kernel(q_ref, k_ref, v_ref, z_ref, gcb_ref, gcT_ref, nw_ref, o_ref, s_ref, *, nc):
    c = GDN_CHUNK
    nh = GDN_HEADS

    @pl.when(pl.program_id(1) == 0)
    def _():
        s_ref[...] = jnp.zeros_like(s_ref)

    row = lax.broadcasted_iota(jnp.int32, (nc, c, c), 1)
    col = lax.broadcasted_iota(jnp.int32, (nc, c, c), 2)
    eye = (row == col).astype(F32)
    nw = nw_ref[...]
    gcb = gcb_ref[0]
    gct = gcT_ref[0]

    u_h, wq_h, qkkd_h, dlast_h = [], [], [], []
    for h in range(nh):
        sl = slice(h * LANES, (h + 1) * LANES)
        q = q_ref[0, :, sl].reshape(nc, c, LANES)
        k = k_ref[0, :, sl].reshape(nc, c, LANES)
        v = v_ref[0, :, sl].astype(F32).reshape(nc, c, LANES)
        gc = gcb[:, h:h + 1].reshape(nc, c, 1)
        beta = gcb[:, nh + h:nh + h + 1].reshape(nc, c, 1)
        gc_row = gct[:, h:h + 1, :]
        g_last = gc[:, c - 1:c, :]
        decay = jnp.exp(jnp.where(row >= col, gc - gc_row, NEG))
        eg = jnp.exp(gc)
        kf = k.astype(F32)
        kb = kf * beta
        aq = _bmm_nt(jnp.concatenate([kb.astype(BF16), q], axis=1), k)
        a = jnp.where(row > col, aq[:, :c] * decay, 0.0)
        qk = aq[:, c:] * decay
        qpow = -a
        usum = eye + qpow
        qb = qpow.astype(BF16)
        qpow = _bmm(qb, qb)
        for lvl in range(5):
            qb = qpow.astype(BF16)
            if lvl < 4:
                prod = _bmm(jnp.concatenate([usum.astype(BF16), qb], axis=1), qb)
                usum = usum + prod[:, :c]
                qpow = prod[:, c:]
            else:
                usum = usum + _bmm(usum.astype(BF16), qb)
        rhs = jnp.concatenate([v * beta, kb * eg], axis=2).astype(BF16)
        uw = _bmm(usum.astype(BF16), rhs)
        u_h.append(uw[:, :, :LANES])
        wq_h.append(jnp.concatenate([uw[:, :, LANES:], q.astype(F32) * eg], axis=1).astype(BF16))
        k_dec = kf * jnp.exp(g_last - gc)
        k_dec_t = jnp.swapaxes(k_dec, 1, 2)
        qkkd_h.append(jnp.concatenate([qk, k_dec_t], axis=1).astype(BF16))
        dlast_h.append(jnp.exp(g_last))

    for i in range(nc):
        s_old = s_ref[...]
        lhs = jnp.stack([wq_h[h][i] for h in range(nh)])
        r = _bmm(lhs, s_old.astype(BF16))
        u = jnp.stack([u_h[h][i] for h in range(nh)])
        v_new = (u - r[:, :c]).astype(BF16)
        r2 = _bmm(jnp.stack([qkkd_h[h][i] for h in range(nh)]), v_new)
        o = r[:, c:] + r2[:, :c]
        dl = jnp.stack([dlast_h[h][i] for h in range(nh)])
        s_ref[...] = s_old * dl + r2[:, c:]
        ms = jnp.mean(o * o, axis=-1, keepdims=True)
        on = o * lax.rsqrt(ms + NORM_EPS) * nw
        for h in range(nh):
            z = z_ref[0, i * c:(i + 1) * c, h * LANES:(h + 1) * LANES].astype(F32)
            o_ref[0, i * c:(i + 1) * c, h * LANES:(h + 1) * LANES] = (on[h] * _silu(z)).astype(BF16)


def gdn_rec(q, k, v, z, gcb, gcbT, norm_w, nc=16):
    b, t, hk = q.shape
    n_chunks = t // GDN_CHUNK
    tb = nc * GDN_CHUNK
    gct4 = gcbT.reshape(b, 2 * GDN_HEADS, n_chunks, GDN_CHUNK).transpose(0, 2, 1, 3)
    spec = pl.BlockSpec((1, tb, hk), lambda bb, tt: (bb, tt, 0))
    return pl.pallas_call(
        functools.partial(_gdn_rec_kernel, nc=nc),
        grid=(b, t // tb),
        in_specs=[spec, spec, spec, spec,
                  pl.BlockSpec((1, tb, LANES), lambda bb, tt: (bb, tt, 0)),
                  pl.BlockSpec((1, nc, 2 * GDN_HEADS, GDN_CHUNK), lambda bb, tt: (bb, tt, 0, 0)),
                  pl.BlockSpec((1, LANES), lambda bb, tt: (0, 0))],
        out_specs=spec,
        out_shape=jax.ShapeDtypeStruct((b, t, hk), BF16),
        scratch_shapes=[pltpu.VMEM((GDN_HEADS, GDN_HEAD_DIM, GDN_HEAD_DIM), F32)],
        compiler_params=_params(("arbitrary", "arbitrary")),
        name="gdn_rec",
    )(q, k, v, z, gcb, gct4, norm_w.reshape(1, LANES).astype(F32))


GRP_LANE0 = MOE_EXPERTS


def _route(logits):
    lane = lax.broadcasted_iota(jnp.int32, logits.shape, 1)
    lanef = lane.astype(F32)
    far = float(LANES)
    is_grp = (lane >= GRP_LANE0) & (lane < GRP_LANE0 + MOE_GROUPS)
    lg = jnp.where(is_grp, logits, NEG)
    eg = jnp.exp(lg - jnp.max(lg, axis=-1, keepdims=True))
    pg = eg / jnp.sum(eg, axis=-1, keepdims=True)
    gp = jnp.max(pg, axis=-1, keepdims=True)
    gidx = jnp.min(jnp.where(is_grp & (pg == gp), lanef, far), axis=-1, keepdims=True) - float(GRP_LANE0)
    in_grp = (lane < MOE_EXPERTS) & (jnp.floor(lanef * (1.0 / MOE_EPG)) == gidx)
    le = jnp.where(in_grp, logits, NEG)
    ee = jnp.exp(le - jnp.max(le, axis=-1, keepdims=True))
    pe = ee / jnp.sum(ee, axis=-1, keepdims=True)
    p1 = jnp.max(jnp.where(in_grp, pe, -1.0), axis=-1, keepdims=True)
    i1 = jnp.min(jnp.where(in_grp & (pe == p1), lanef, far), axis=-1, keepdims=True)
    rest = in_grp & (lanef != i1)
    p2 = jnp.max(jnp.where(rest, pe, -1.0), axis=-1, keepdims=True)
    i2 = jnp.min(jnp.where(rest & (pe == p2), lanef, far), axis=-1, keepdims=True)
    scale = gp / (p1 + p2)
    return jnp.where(lanef == i1, p1 * scale, jnp.where(lanef == i2, p2 * scale, 0.0)), gidx


MOE_SUB = 128


def _mix_moe_ln_kernel(o_ref, wo_ref, res_ref, mg_ref, mb_ref, wrh_ref, wrl_ref, br_ref, ltri_ref, wgu_ref, wd_ref,
                       g_ref, b_ref, out_ref, h_ref, xs_ref, gs_ref, ys_ref, *, tm, slots):
    hid_w = MOE_EPG * MOE_HIDDEN
    lane = lax.broadcasted_iota(jnp.int32, (tm, LANES), 1)
    shift = MOE_SUB.bit_length() - 1

    h = _layer_norm(DEEPNORM_ALPHA * res_ref[...] + _dot(o_ref[...], wo_ref[...]), mg_ref[...], mb_ref[...])
    h_ref[...] = h
    xb, xlo = _split2(h)
    wh = wrh_ref[...]
    logits = _dot(xb, wh) + (_dot(xlo, wh) + _dot(xb, wrl_ref[...])) + br_ref[...]
    gate, gidx = _route(logits)
    onehot = jnp.where(lane.astype(F32) == gidx, 1.0, 0.0)
    pos = _dot(ltri_ref[...], onehot.astype(BF16))
    cnt = jnp.sum(onehot, axis=0, keepdims=True).astype(jnp.int32)
    offs, n_subs, off = [], [], 0
    for g in range(MOE_GROUPS):
        n_sub = lax.shift_right_logical(cnt[0, g] + (MOE_SUB - 1), shift)
        offs.append(off)
        n_subs.append(n_sub)
        off = off + lax.shift_left(n_sub, shift)
    off_row = jnp.zeros((1, LANES), jnp.int32)
    for g in range(1, MOE_GROUPS):
        off_row = jnp.where(lane[0:1, :] == g, offs[g], off_row)
    dest = jnp.sum(onehot * (pos + off_row.astype(F32)), axis=-1, keepdims=True)
    hi = jnp.floor(dest * (1.0 / 32.0))
    parts = jnp.where(lane == 0, hi, jnp.where(lane == 1, dest - 32.0 * hi, 0.0)).astype(BF16)
    pick = (lax.broadcasted_iota(jnp.int32, (8, LANES), 0) == lax.broadcasted_iota(jnp.int32, (8, LANES), 1))
    rows = _dot_nt(pick.astype(BF16), parts)
    dest_row = 32.0 * rows[0:1, :] + rows[1:2, :]
    slot = lax.broadcasted_iota(jnp.int32, (slots, tm), 0).astype(F32)
    perm = jnp.where(slot == dest_row, 1.0, 0.0).astype(BF16)
    xs_ref[...] = _dot(perm, xb).astype(BF16)
    gs_ref[...] = _dot(perm, gate.astype(BF16))
    ys_ref[...] = jnp.zeros_like(ys_ref)

    for g in range(MOE_GROUPS):
        def sub_tile(j, carry, g=g):
            r0 = pl.multiple_of(offs[g] + j * MOE_SUB, MOE_SUB)
            xsub = xs_ref[pl.ds(r0, MOE_SUB), :]
            gsub = gs_ref[pl.ds(r0, MOE_SUB), :]
            hids = []
            for e in range(MOE_EPG):
                ge = gsub[:, g * MOE_EPG + e:g * MOE_EPG + e + 1]
                hg = _dot(xsub, wgu_ref[g, :, e * MOE_HIDDEN:(e + 1) * MOE_HIDDEN])
                hu = _dot(xsub, wgu_ref[g, :, hid_w + e * MOE_HIDDEN:hid_w + (e + 1) * MOE_HIDDEN])
                hids.append((_silu(hg) * hu * ge).astype(BF16))
            ys_ref[pl.ds(r0, MOE_SUB), :] = _dot(jnp.concatenate(hids, axis=1), wd_ref[g]).astype(BF16)
            return carry

        lax.fori_loop(0, n_subs[g], sub_tile, 0)

    slot_l = lax.broadcasted_iota(jnp.int32, (tm, slots), 1).astype(F32)
    unperm = jnp.where(slot_l == dest, 1.0, 0.0).astype(BF16)
    y = _dot(unperm, ys_ref[...])
    out_ref[...] = _layer_norm(DEEPNORM_ALPHA * h_ref[...] + y, g_ref[...], b_ref[...])


def mix_moe_ln(o, w_o, res, mix_g, mix_b, w_grp, b_grp, w_rt, b_rt, w_gate, w_up, w_down, ln_g, ln_b, tm=512):
    n, d = res.shape
    kdim = o.shape[1]
    slots = tm + MOE_GROUPS * MOE_SUB
    wr = jnp.zeros((d, LANES), F32).at[:, :MOE_EXPERTS].set(w_rt).at[:, GRP_LANE0:GRP_LANE0 + MOE_GROUPS].set(w_grp)
    br = jnp.zeros((1, LANES), F32).at[0, :MOE_EXPERTS].set(b_rt).at[0, GRP_LANE0:GRP_LANE0 + MOE_GROUPS].set(b_grp)
    wrh = wr.astype(BF16)
    wrl = (wr - wrh.astype(F32)).astype(BF16)
    hid_w = MOE_EPG * MOE_HIDDEN
    by_group = lambda w: w.reshape(MOE_GROUPS, MOE_EPG, d, MOE_HIDDEN).transpose(0, 2, 1, 3).reshape(MOE_GROUPS, d, hid_w)
    wgu = jnp.concatenate([by_group(w_gate), by_group(w_up)], axis=-1).astype(BF16)
    wd = w_down.reshape(MOE_GROUPS, hid_w, d).astype(BF16)
    r = jnp.arange(tm)
    ltri = (r[:, None] > r[None, :]).astype(BF16)
    row = lambda i: (i, 0)
    once = pl.Buffered(1)
    const2 = lambda shape: pl.BlockSpec(shape, lambda i: (0, 0), pipeline_mode=once)
    const3 = lambda shape: pl.BlockSpec(shape, lambda i: (0, 0, 0), pipeline_mode=once)
    return pl.pallas_call(
        functools.partial(_mix_moe_ln_kernel, tm=tm, slots=slots),
        grid=(n // tm,),
        in_specs=[pl.BlockSpec((tm, kdim), row), const2((kdim, d)), pl.BlockSpec((tm, d), row),
                  const2((1, d)), const2((1, d)),
                  const2((d, LANES)), const2((d, LANES)), const2((1, LANES)),
                  const2((tm, tm)),
                  const3((MOE_GROUPS, d, 2 * hid_w)), const3((MOE_GROUPS, hid_w, d)),
                  const2((1, d)), const2((1, d))],
        out_specs=pl.BlockSpec((tm, d), row),
        out_shape=jax.ShapeDtypeStruct((n, d), F32),
        scratch_shapes=[pltpu.VMEM((tm, d), F32),
                        pltpu.VMEM((slots, d), BF16), pltpu.VMEM((slots, LANES), F32), pltpu.VMEM((slots, d), BF16)],
        compiler_params=_params(("arbitrary",)),
        name="mix_moe_ln",
    )(o, w_o.astype(BF16), res, mix_g.reshape(1, d), mix_b.reshape(1, d), wrh, wrl, br, ltri, wgu, wd,
      ln_g.reshape(1, d), ln_b.reshape(1, d))


KV_W = NSA_GROUPS * NSA_HEAD_DIM
GATE_ROWS = 16


def _nsa_proj_kernel(h_ref, wq_ref, wg_ref, wc_ref, wks_ref, wkw_ref, wvs_ref, wvw_ref,
                     qT_ref, gT_ref, xk_ref, xv_ref, ksa_ref, kwa_ref, vsT_ref, vwT_ref, cmp_scr, *, tm):
    t0 = pl.program_id(1) * tm
    hb = h_ref[0].astype(BF16)
    qT_ref[0] = _dot_nt(wq_ref[...], hb).astype(BF16)
    gT_ref[0] = jax.nn.sigmoid(_dot_nt(wg_ref[...], hb))
    craw = _dot(hb, wc_ref[...])
    n_cb = 2 * KV_W // LANES
    for cb in range(n_cb):
        cmp_scr[cb] = craw[:, cb * LANES:(cb + 1) * LANES]
    nrow = tm // CMP_STRIDE
    lane_r = lax.broadcasted_iota(jnp.int32, (nrow, LANES), 1)
    for j in range(CMP_STRIDE // 2):
        for cb in range(n_cb):
            a = cmp_scr[cb, pl.ds(2 * j, nrow, stride=CMP_STRIDE), :]
            bm = cmp_scr[cb, pl.ds(2 * j + 1, nrow, stride=CMP_STRIDE), :]
            pieces = (jnp.where(lane_r < NSA_HEAD_DIM, a, pltpu.roll(bm, NSA_HEAD_DIM, 1)),
                      jnp.where(lane_r < NSA_HEAD_DIM, pltpu.roll(a, NSA_HEAD_DIM, 1), bm))
            for k, val in enumerate(pieces):
                pg = 2 * cb + k
                ref = xk_ref if pg < NSA_GROUPS else xv_ref
                ref[0, pg % NSA_GROUPS, :, j * LANES:(j + 1) * LANES] = val.astype(BF16)
    vsT_ref[0] = _dot_nt(wvs_ref[...], hb).astype(BF16)
    vwT_ref[0] = _dot_nt(wvw_ref[...], hb).astype(BF16)
    ks = _dot(hb, wks_ref[...])
    kw = _dot(hb, wkw_ref[...])
    lane = lax.broadcasted_iota(jnp.int32, (tm, LANES), 1)
    blk = (t0 + lax.broadcasted_iota(jnp.int32, (tm, LANES), 0)) // SEL_BLOCK
    onehot = jnp.where(lane - NSA_HEAD_DIM == blk, 1.0, 0.0)
    for g in range(NSA_GROUPS):
        ksg = ks[:, g * LANES:(g + 1) * LANES]
        ksa_ref[0, g] = jnp.where(lane < NSA_HEAD_DIM, ksg, onehot).astype(BF16)
        kwa_ref[0, g] = kw[:, g * LANES:(g + 1) * LANES].astype(BF16)


def _pad_group_cols(w):
    d = w.shape[0]
    w4 = w.reshape(d, NSA_GROUPS, NSA_HEAD_DIM)
    return jnp.concatenate([w4, jnp.zeros_like(w4)], axis=-1).reshape(d, NSA_GROUPS * LANES)


def nsa_proj(h, w_kv, w_in, tm=256):
    b, t, d = h.shape
    scale = NSA_HEAD_DIM ** -0.5 * LOG2E
    wq = (w_in[:, :NSA_WIDTH] * scale).T.astype(BF16)
    wgate = w_in[:, NSA_WIDTH:].reshape(d, NSA_GROUPS, NSA_HPG, N_BRANCHES)
    wgate = wgate.transpose(0, 1, 3, 2).reshape(d, NSA_GROUPS, N_BRANCHES * NSA_HPG)
    wgate = jnp.concatenate([wgate, jnp.zeros((d, NSA_GROUPS, GATE_ROWS - N_BRANCHES * NSA_HPG), F32)], axis=-1)
    wg = wgate.reshape(d, NSA_GROUPS * GATE_ROWS).T.astype(BF16)
    part = lambda p: w_kv[:, p * KV_W:(p + 1) * KV_W]
    wc = w_kv[:, :2 * KV_W].astype(BF16)
    wks = _pad_group_cols(part(2)).astype(BF16)
    wvs = part(3).T.astype(BF16)
    wkw = _pad_group_cols(part(4)).astype(BF16)
    wvw = part(5).T.astype(BF16)
    const = lambda bb, tt: (0, 0)
    tok = lambda bb, tt: (bb, tt, 0)
    tr = lambda bb, tt: (bb, 0, tt)
    g4 = lambda bb, tt: (bb, 0, tt, 0)
    ng = NSA_GROUPS * GATE_ROWS
    wide = CMP_STRIDE * NSA_HEAD_DIM
    return pl.pallas_call(
        functools.partial(_nsa_proj_kernel, tm=tm),
        grid=(b, t // tm),
        in_specs=[pl.BlockSpec((1, tm, d), tok),
                  pl.BlockSpec((NSA_WIDTH, d), const), pl.BlockSpec((ng, d), const),
                  pl.BlockSpec((d, 2 * KV_W), const),
                  pl.BlockSpec((d, NSA_GROUPS * LANES), const), pl.BlockSpec((d, NSA_GROUPS * LANES), const),
                  pl.BlockSpec((KV_W, d), const), pl.BlockSpec((KV_W, d), const)],
        out_specs=[pl.BlockSpec((1, NSA_WIDTH, tm), tr), pl.BlockSpec((1, ng, tm), tr),
                   pl.BlockSpec((1, NSA_GROUPS, tm // CMP_STRIDE, wide), g4),
                   pl.BlockSpec((1, NSA_GROUPS, tm // CMP_STRIDE, wide), g4),
                   pl.BlockSpec((1, NSA_GROUPS, tm, LANES), g4), pl.BlockSpec((1, NSA_GROUPS, tm, LANES), g4),
                   pl.BlockSpec((1, KV_W, tm), tr), pl.BlockSpec((1, KV_W, tm), tr)],
        out_shape=[jax.ShapeDtypeStruct((b, NSA_WIDTH, t), BF16), jax.ShapeDtypeStruct((b, ng, t), F32),
                   jax.ShapeDtypeStruct((b, NSA_GROUPS, t // CMP_STRIDE, wide), BF16),
                   jax.ShapeDtypeStruct((b, NSA_GROUPS, t // CMP_STRIDE, wide), BF16),
                   jax.ShapeDtypeStruct((b, NSA_GROUPS, t, LANES), BF16),
                   jax.ShapeDtypeStruct((b, NSA_GROUPS, t, LANES), BF16),
                   jax.ShapeDtypeStruct((b, KV_W, t), BF16), jax.ShapeDtypeStruct((b, KV_W, t), BF16)],
        scratch_shapes=[pltpu.VMEM((2 * KV_W // LANES, tm, LANES), F32)],
        compiler_params=_params(("arbitrary", "arbitrary")),
        name="nsa_proj",
    )(h, wq, wg, wc, wks, wkw, wvs, wvw)


N_CMP_PAD = 128


def _nsa_compress_kernel(xk_ref, xv_ref, pk_ref, pv_ref, w1k_ref, w1v_ref, w2k_ref, w2vT_ref, kc_ref, vcT_ref):
    half = CMP_STRIDE * NSA_HEAD_DIM
    pbk = _dot(pk_ref[...], w1k_ref[...])[0:1, :]
    pbv = _dot(pv_ref[...], w1v_ref[...])[0:1, :]
    for g in range(NSA_GROUPS):
        xk = xk_ref[0, g]
        hk = _dot(xk, w1k_ref[:half, :]) + pltpu.roll(_dot(xk, w1k_ref[half:, :]), N_CMP_PAD - 1, 0) + pbk
        kc_ref[0, g] = _dot(_silu(hk).astype(BF16), w2k_ref[...]).astype(BF16)
        xv = xv_ref[0, g]
        hv = _dot(xv, w1v_ref[:half, :]) + pltpu.roll(_dot(xv, w1v_ref[half:, :]), N_CMP_PAD - 1, 0) + pbv
        vcT_ref[0, g] = _dot_nt(w2vT_ref[...], _silu(hv).astype(BF16)).astype(BF16)


def nsa_compress(xk, xv, k_pos, k_w1, k_w2, v_pos, v_w1, v_w2):
    b, _, nrow, wide = xk.shape
    assert nrow == N_CMP_PAD
    pad_pos = lambda p: jnp.zeros((8, CMP_BLOCK * NSA_HEAD_DIM), F32).at[0].set(p.reshape(-1)).astype(BF16)
    spec_x = pl.BlockSpec((1, NSA_GROUPS, nrow, wide), lambda bb: (bb, 0, 0, 0))
    c2 = lambda bb: (0, 0)
    return pl.pallas_call(
        _nsa_compress_kernel,
        grid=(b,),
        in_specs=[spec_x, spec_x,
                  pl.BlockSpec((8, 2 * wide), c2), pl.BlockSpec((8, 2 * wide), c2),
                  pl.BlockSpec((2 * wide, CMP_HIDDEN), c2), pl.BlockSpec((2 * wide, CMP_HIDDEN), c2),
                  pl.BlockSpec((CMP_HIDDEN, NSA_HEAD_DIM), c2), pl.BlockSpec((NSA_HEAD_DIM, CMP_HIDDEN), c2)],
        out_specs=[pl.BlockSpec((1, NSA_GROUPS, N_CMP_PAD, NSA_HEAD_DIM), lambda bb: (bb, 0, 0, 0)),
                   pl.BlockSpec((1, NSA_GROUPS, NSA_HEAD_DIM, N_CMP_PAD), lambda bb: (bb, 0, 0, 0))],
        out_shape=[jax.ShapeDtypeStruct((b, NSA_GROUPS, N_CMP_PAD, NSA_HEAD_DIM), BF16),
                   jax.ShapeDtypeStruct((b, NSA_GROUPS, NSA_HEAD_DIM, N_CMP_PAD), BF16)],
        compiler_params=_params(("arbitrary",)),
        name="nsa_compress",
    )(xk, xv, pad_pos(k_pos), pad_pos(v_pos), k_w1.astype(BF16), v_w1.astype(BF16),
      k_w2.astype(BF16), v_w2.T.astype(BF16))


SEL_CHUNK = 256
WIN_CHUNK = 128
TB_ROWS = WINDOW + SEL_CHUNK
TBC_ROWS = 256
TBC_OFF = 120
N_SEL_BLOCKS = 32
QW = NSA_HPG * Q_BLOCK


def _t5_bucket(dist):
    n = jnp.maximum(dist, 0)
    max_exact = REL_BUCKETS // 2
    nf = jnp.maximum(n, 1).astype(F32)
    large = max_exact + (jnp.log(nf / max_exact) / math.log(REL_MAX_DIST / max_exact)
                         * (REL_BUCKETS - max_exact)).astype(jnp.int32)
    large = jnp.minimum(large, REL_BUCKETS - 1)
    return jnp.where(n < max_exact, n, large)


def _bias_tables(rel_bias):
    rel = rel_bias.astype(F32)
    far = rel[REL_BUCKETS - 1]
    ql = jnp.arange(Q_BLOCK)
    heads = jnp.arange(NSA_HEADS).reshape(NSA_GROUPS, NSA_HPG)

    def table(dist, valid):
        bucket = _t5_bucket(dist)[None, :, None, :]
        relc = ((rel - far) * LOG2E)[:, heads][:, :, None, :, None]
        bias = jnp.zeros((NSA_GROUPS, dist.shape[0], NSA_HPG, Q_BLOCK), F32)
        for b in range(REL_BUCKETS):
            bias = jnp.where(bucket == b, relc[b], bias)
        bias = jnp.where(valid[None, :, None, :], bias, NEG)
        return bias.reshape(NSA_GROUPS, dist.shape[0], QW)

    d = ql[None, :] - (jnp.arange(TB_ROWS)[:, None] - WINDOW)
    tb = table(d, (d >= 0) & (d < WINDOW))
    m = jnp.arange(TBC_ROWS)[:, None] - TBC_OFF
    dc = ql[None, :] - CMP_STRIDE * m - (CMP_BLOCK - 1)
    tbc = table(dc, dc >= 0)
    return tb, tbc


def _nsa_attn_kernel(qT_ref, gT_ref, kc_ref, vcT_ref, ksa_ref, vsT_ref, kwa_ref, vwT_ref, tb_ref, tbc_ref, ovl_ref,
                     o_ref, m_ref, l_ref, acc_ref, ot_ref, oT_ref, sc_ref):
    c = pl.program_id(1)
    ng, hpg, hd = NSA_GROUPS, NSA_HPG, NSA_HEAD_DIM
    nsel = N_SEL_BLOCKS
    n_top = min(TOP_N, nsel)
    n_win = WINDOW // WIN_CHUNK

    q4 = jnp.stack([jnp.concatenate([qT_ref[0, (g * hpg + hh) * hd:(g * hpg + hh + 1) * hd, :]
                                     for hh in range(hpg)], axis=1) for g in range(ng)])
    gates = gT_ref[0]

    def gate3(br):
        return jnp.stack([jnp.concatenate(
            [gates[g * GATE_ROWS + br * hpg + hh:g * GATE_ROWS + br * hpg + hh + 1, :] for hh in range(hpg)], axis=1)
            for g in range(ng)])

    def fresh():
        return (jnp.full((ng, 1, QW), NEG, F32), jnp.zeros((ng, 1, QW), F32), jnp.zeros((ng, hd, QW), F32))

    def load_state():
        return m_ref[...], l_ref[...], acc_ref[...]

    def store_state(state):
        m_ref[...], l_ref[...], acc_ref[...] = state

    def upd(state, s, vT):
        m_old, l_old, acc_old = state
        m_new = jnp.maximum(m_old, jnp.max(s, axis=1, keepdims=True))
        alpha = jnp.exp2(m_old - m_new)
        p = jnp.exp2(s - m_new)
        return (m_new, l_old * alpha + jnp.sum(p, axis=1, keepdims=True),
                acc_old * alpha + _bmm(vT, p.astype(BF16)))

    def result(state, br):
        _, l_fin, acc_fin = state
        return gate3(br) * (acc_fin * (1.0 / l_fin))

    start = pl.multiple_of(TBC_OFF - 8 * c, 8)
    tbc = tbc_ref[:, pl.ds(start, N_CMP_PAD), :]
    nrow = lax.broadcasted_iota(jnp.int32, tbc.shape, 1)
    tbc = jnp.where(nrow < N_CMP_PAD - 1, tbc, NEG)
    s = _bmm(kc_ref[0], q4) + tbc
    valid = tbc > 0.5 * NEG
    e = jnp.where(valid, jnp.exp2(s - jnp.max(s, axis=1, keepdims=True)), 0.0)
    l = jnp.sum(e, axis=1, keepdims=True)
    p = e * (1.0 / jnp.where(l > 0.0, l, 1.0))
    o_cmp = gate3(0) * _bmm(vcT_ref[0], p.astype(BF16))

    psum = p[:, :, 0:Q_BLOCK]
    for hh in range(1, hpg):
        psum = psum + p[:, :, hh * Q_BLOCK:(hh + 1) * Q_BLOCK]
    psum = jnp.concatenate([psum[g] for g in range(ng)], axis=1)
    ph, plo = _split2(psum)
    ovl = ovl_ref[...]
    imp = _dot(ovl, ph) + _dot(ovl, plo)
    jidx = lax.broadcasted_iota(jnp.int32, imp.shape, 0)
    qlane = lax.broadcasted_iota(jnp.int32, imp.shape, 1) & (Q_BLOCK - 1)
    jq = 2 * c + (qlane >= SEL_BLOCK).astype(jnp.int32)
    forced = (jidx == 0) | (jidx == jq) | (jidx == jq - 1)
    imp = jnp.where(forced, -NEG, jnp.where(jidx <= jq, imp, NEG))
    sub8 = lax.broadcasted_iota(jnp.int32, (8, imp.shape[1]), 0)
    rows = [imp[8 * v:8 * v + 8, :] for v in range(nsel // 8)]
    cnts = [jnp.zeros_like(r) for r in rows]
    for i in range(nsel):
        r = imp[i:i + 1, :]
        for v in range(nsel // 8):
            ge = jnp.where(r >= rows[v], 1.0, 0.0)
            gt = jnp.where(r > rows[v], 1.0, 0.0)
            if i < 8 * v:
                ahead = ge
            elif i >= 8 * v + 8:
                ahead = gt
            else:
                ahead = jnp.where(sub8 > i - 8 * v, ge, gt)
            cnts[v] = cnts[v] + ahead
    cnt = jnp.concatenate(cnts, axis=0)
    selm = jnp.where(cnt < float(n_top), 0.0, NEG).astype(BF16)
    pad = jnp.zeros((LANES - hd - nsel, QW), BF16)
    q_aug = jnp.stack([jnp.concatenate(
        [q4[g], jnp.concatenate([selm[:, g * Q_BLOCK:(g + 1) * Q_BLOCK]] * hpg, axis=1), pad], axis=0)
        for g in range(ng)])

    def sel_scores(k0):
        return _bmm(ksa_ref[0, :, pl.ds(k0, SEL_CHUNK), :], q_aug)

    def sel_values(k0):
        return vsT_ref[0, :, pl.ds(k0, SEL_CHUNK)].reshape(ng, hd, SEL_CHUNK)

    def win(first, n_chunks, tab_chunk):
        width = n_chunks * WIN_CHUNK
        k0 = pl.multiple_of((c - n_win + first) * WIN_CHUNK, WIN_CHUNK)
        s_w = _bmm(kwa_ref[0, :, pl.ds(k0, width), :], q_aug)
        if tab_chunk is not None:
            lo = (tab_chunk - first) * WIN_CHUNK
            part = s_w[:, lo:lo + WIN_CHUNK] + tb_ref[:, tab_chunk * WIN_CHUNK:(tab_chunk + 1) * WIN_CHUNK, :]
            pieces = ([s_w[:, :lo]] if lo else []) + [part] + ([s_w[:, lo + WIN_CHUNK:]] if lo + WIN_CHUNK < width else [])
            s_w = jnp.concatenate(pieces, axis=1) if len(pieces) > 1 else part
        return s_w, vwT_ref[0, :, pl.ds(k0, width)].reshape(ng, hd, width)

    def raw_scores(i):
        return sel_scores(pl.multiple_of(i * SEL_CHUNK, SEL_CHUNK))

    def values(i):
        return sel_values(pl.multiple_of(i * SEL_CHUNK, SEL_CHUNK))

    def sel_table(i):
        tstart = pl.multiple_of(WINDOW - (c * Q_BLOCK - i * SEL_CHUNK), Q_BLOCK)
        return tb_ref[:, pl.ds(tstart, SEL_CHUNK), :]

    store_state(fresh())
    n_far = jnp.maximum(c // 2 - 1, 0)

    @pl.when(c >= 2)
    def _():
        sc_ref[0] = raw_scores(0)

    def far_pair(j, carry):
        i = 2 * j
        sc_ref[1] = raw_scores(i + 1)
        store_state(upd(load_state(), sc_ref[0], values(i)))
        sc_ref[0] = raw_scores(i + 2)
        store_state(upd(load_state(), sc_ref[1], values(i + 1)))
        return carry

    lax.fori_loop(0, n_far // 2, far_pair, 0)

    @pl.when(n_far % 2 == 1)
    def _():
        sc_ref[1] = raw_scores(n_far)
        store_state(upd(load_state(), sc_ref[0], values(n_far - 1)))

    @pl.when(c >= n_win)
    def _():
        s_a = sc_ref[n_far % 2] + sel_table(c // 2 - 1)
        s_b = raw_scores(c // 2)
        w_a, wv_a = win(0, 2, 0)
        st_s = upd(load_state(), s_a, values(c // 2 - 1))
        w_b, wv_b = win(2, 2, 3)
        st_w = upd(fresh(), w_a, wv_a)
        st_s = upd(st_s, s_b + sel_table(c // 2), values(c // 2))
        w_c, wv_c = win(4, 1, 4)
        st_w = upd(st_w, w_b, wv_b)
        st_w = upd(st_w, w_c, wv_c)
        ot_ref[...] = o_cmp + result(st_s, 1) + result(st_w, 2)

    @pl.when(c < n_win)
    def _():
        @pl.when(c >= 2)
        def _():
            store_state(upd(load_state(), sc_ref[0] + sel_table(c // 2 - 1), values(c // 2 - 1)))

        ot_ref[...] = o_cmp + result(upd(load_state(), raw_scores(c // 2) + sel_table(c // 2), values(c // 2)), 1)
        store_state(fresh())
        pl.when(c == 3)(lambda: store_state(upd(load_state(), *win(1, 1, None))))
        pl.when(c >= 2)(lambda: store_state(upd(load_state(), *win(2, 2, 3))))
        pl.when(c == 1)(lambda: store_state(upd(load_state(), *win(3, 1, 3))))
        ot_ref[...] += result(upd(load_state(), *win(4, 1, 4)), 2)

    for g in range(ng):
        for hh in range(hpg):
            oT_ref[hh, g * hd:(g + 1) * hd, :] = ot_ref[g, :, hh * Q_BLOCK:(hh + 1) * Q_BLOCK]
    for hh in range(hpg):
        o_ref[0, :, hh * KV_W:(hh + 1) * KV_W] = oT_ref[hh].T.astype(BF16)


def nsa_attn(qT, gT, kc, vcT, ksa, vsT, kwa, vwT, rel_bias):
    b, _, t = qT.shape
    nq = t // Q_BLOCK
    tb, tbc = _bias_tables(rel_bias)
    nc = (t - CMP_BLOCK) // CMP_STRIDE + 1
    cs = jnp.arange(N_CMP_PAD) * CMP_STRIDE
    ss = jnp.arange(N_SEL_BLOCKS) * SEL_BLOCK
    ovl = ((cs[None, :] < ss[:, None] + SEL_BLOCK) & (cs[None, :] + CMP_BLOCK - 1 >= ss[:, None])
           & (jnp.arange(N_CMP_PAD)[None, :] < nc)).astype(BF16)
    ng = NSA_GROUPS * GATE_ROWS
    per_b3 = lambda bb, cc: (bb, 0, 0)
    per_b4 = lambda bb, cc: (bb, 0, 0, 0)
    c3 = lambda bb, cc: (0, 0, 0)
    return pl.pallas_call(
        _nsa_attn_kernel,
        grid=(b, nq),
        in_specs=[pl.BlockSpec((1, NSA_WIDTH, Q_BLOCK), lambda bb, cc: (bb, 0, cc)),
                  pl.BlockSpec((1, ng, Q_BLOCK), lambda bb, cc: (bb, 0, cc)),
                  pl.BlockSpec((1, NSA_GROUPS, N_CMP_PAD, NSA_HEAD_DIM), per_b4),
                  pl.BlockSpec((1, NSA_GROUPS, NSA_HEAD_DIM, N_CMP_PAD), per_b4),
                  pl.BlockSpec((1, NSA_GROUPS, t, LANES), per_b4),
                  pl.BlockSpec((1, KV_W, t), per_b3),
                  pl.BlockSpec((1, NSA_GROUPS, t, LANES), per_b4),
                  pl.BlockSpec((1, KV_W, t), per_b3),
                  pl.BlockSpec((NSA_GROUPS, TB_ROWS, QW), c3),
                  pl.BlockSpec((NSA_GROUPS, TBC_ROWS, QW), c3),
                  pl.BlockSpec((N_SEL_BLOCKS, N_CMP_PAD), lambda bb, cc: (0, 0))],
        out_specs=pl.BlockSpec((1, Q_BLOCK, NSA_WIDTH), lambda bb, cc: (bb, cc, 0)),
        out_shape=jax.ShapeDtypeStruct((b, t, NSA_WIDTH), BF16),
        scratch_shapes=[pltpu.VMEM((NSA_GROUPS, 1, QW), F32), pltpu.VMEM((NSA_GROUPS, 1, QW), F32),
                        pltpu.VMEM((NSA_GROUPS, NSA_HEAD_DIM, QW), F32),
                        pltpu.VMEM((NSA_GROUPS, NSA_HEAD_DIM, QW), F32),
                        pltpu.VMEM((NSA_HPG, KV_W, Q_BLOCK), F32),
                        pltpu.VMEM((2, NSA_GROUPS, SEL_CHUNK, QW), F32)],
        compiler_params=_params(("arbitrary", "arbitrary")),
        name="nsa_attn",
    )(qT, gT, kc, vcT, ksa, vsT, kwa, vwT, tb, tbc, ovl)


def kernel(x, gdn_w_in, gdn_conv_w, gdn_a_log, gdn_dt_bias, gdn_norm_w, gdn_w_o, nsa_w_kv, cmp_k_pos, cmp_k_w1,
           cmp_k_w2, cmp_v_pos, cmp_v_w1, cmp_v_w2, nsa_w_in, nsa_w_o, rel_bias, ln_mix_g, ln_mix_b, ln_ffn_g,
           ln_ffn_b, moe_w_grp, moe_b_grp, moe_w_rt, moe_b_rt, moe_w_gate, moe_w_up, moe_w_down):
    b, t, d = x.shape
    n = b * t

    def mix_ffn(o, w_o, res, layer):
        return mix_moe_ln(o, w_o, res, ln_mix_g[layer], ln_mix_b[layer], moe_w_grp[layer], moe_b_grp[layer],
                          moe_w_rt[layer], moe_b_rt[layer], moe_w_gate[layer], moe_w_up[layer], moe_w_down[layer],
                          ln_ffn_g[layer], ln_ffn_b[layer])

    q, k, v, z, gcb, gcbT = gdn_inproj(x, gdn_w_in[0], gdn_conv_w[0], gdn_a_log[0], gdn_dt_bias[0])
    o = gdn_rec(q, k, v, z, gcb, gcbT, gdn_norm_w[0])
    h = mix_ffn(o.reshape(n, GDN_WIDTH), gdn_w_o[0], x.reshape(n, d), 0)

    qT, gT, xk, xv, ksa, kwa, vsT, vwT = nsa_proj(h.reshape(b, t, d), nsa_w_kv, nsa_w_in[0])
    kc, vcT = nsa_compress(xk, xv, cmp_k_pos, cmp_k_w1, cmp_k_w2, cmp_v_pos, cmp_v_w1, cmp_v_w2)
    o = nsa_attn(qT, gT, kc, vcT, ksa, vsT, kwa, vwT, rel_bias)
    w_o = nsa_w_o[0].reshape(NSA_GROUPS, NSA_HPG, NSA_HEAD_DIM, d).transpose(1, 0, 2, 3).reshape(NSA_WIDTH, d)
    h = mix_ffn(o.reshape(n, NSA_WIDTH), w_o, h, 1)
    return h.reshape(b, t, d)
```

```python
import functools
import math

import jax
import jax.numpy as jnp
from jax import lax
from jax.experimental import pallas as pl
from jax.experimental.pallas import tpu as pltpu

F32 = jnp.float32
BF16 = jnp.bfloat16

D_MODEL = 1024
DEPTH = 2
GDN_HEADS = 8
GDN_HEAD_DIM = 128
GDN_WIDTH = GDN_HEADS * GDN_HEAD_DIM
GDN_CONV = 4
GDN_CHUNK = 64
NSA_HEADS = 16
NSA_GROUPS = 4
NSA_HPG = NSA_HEADS // NSA_GROUPS
NSA_HEAD_DIM = 64
NSA_WIDTH = NSA_HEADS * NSA_HEAD_DIM
CMP_BLOCK = 32
CMP_STRIDE = 16
CMP_HIDDEN = 2 * NSA_HEAD_DIM
SEL_BLOCK = 64
TOP_N = 16
WINDOW = 512
Q_BLOCK = 128
N_BRANCHES = 3
REL_BUCKETS = 32
REL_MAX_DIST = 128
MOE_GROUPS = 4
MOE_EPG = 4
MOE_EXPERTS = MOE_GROUPS * MOE_EPG
MOE_HIDDEN = 256
DEEPNORM_ALPHA = (2 * DEPTH) ** 0.25
LN_EPS = 1e-5
NORM_EPS = 1e-6
NEG = -1e30
LOG2E = math.log2(math.e)

LANES = 128
VMEM_LIMIT = 56 * 1024 * 1024


def _dot(a, b):
    return jnp.dot(a, b, preferred_element_type=F32)


def _dot_nt(a, b):
    return lax.dot_general(a, b, (((1,), (1,)), ((), ())), preferred_element_type=F32)


def _dot_tn(a, b):
    return lax.dot_general(a, b, (((0,), (0,)), ((), ())), preferred_element_type=F32)


def _split2(x):
    hi = x.astype(BF16)
    lo = (x - hi.astype(F32)).astype(BF16)
    return hi, lo


def _silu(x):
    return x * jax.nn.sigmoid(x)


def _params(sem):
    return pltpu.CompilerParams(dimension_semantics=sem, vmem_limit_bytes=VMEM_LIMIT)


def _layer_norm(x, g, b):
    mu = jnp.mean(x, axis=-1, keepdims=True)
    xc = x - mu
    var = jnp.mean(xc * xc, axis=-1, keepdims=True)
    return xc * lax.rsqrt(var + LN_EPS) * g + b


def _gdn_inproj_kernel(x_ref, w_ref, wabh_ref, wabl_ref, cw_ref, alog_ref, dtb_ref, ltri_ref,
                       q_ref, k_ref, v_ref, z_ref, gcb_ref, gcbT_ref, carry_ref, *, tm):
    @pl.when(pl.program_id(1) == 0)
    def _():
        carry_ref[...] = jnp.zeros_like(carry_ref)

    x = x_ref[0]
    xb, xlo = _split2(x)

    wh = wabh_ref[...]
    ab = _dot(xb, wh) + (_dot(xlo, wh) + _dot(xb, wabl_ref[...]))
    lane = lax.broadcasted_iota(jnp.int32, ab.shape, 1)
    sp_in = ab + dtb_ref[...]
    softplus = jnp.maximum(sp_in, 0.0) + jnp.log1p(jnp.exp(-jnp.abs(sp_in)))
    g = jnp.where(lane < GDN_HEADS, -jnp.exp(alog_ref[...]) * softplus, 0.0)
    beta = jax.nn.sigmoid(ab)
    g1 = g.astype(BF16)
    r1 = g - g1.astype(F32)
    g2 = r1.astype(BF16)
    g3 = (r1 - g2.astype(F32)).astype(BF16)
    ltri = ltri_ref[...]
    gc = _dot(ltri, g1) + (_dot(ltri, g2) + _dot(ltri, g3))
    gcb = jnp.where(lane < GDN_HEADS, gc, jnp.where(lane < 2 * GDN_HEADS, beta, 0.0))
    gcb_ref[0] = gcb
    gcbT_ref[0] = gcb.T[:2 * GDN_HEADS, :]

    row8 = lax.broadcasted_iota(jnp.int32, (8, 256), 0)
    outs = (q_ref, k_ref, v_ref)
    for s in range(3):
        for cc in range(4):
            col = s * GDN_WIDTH + cc * 256
            y = _dot(xb, w_ref[:, col:col + 256])
            prev = carry_ref[s * 4 + cc]
            carry_ref[s * 4 + cc] = y[tm - 8:, :]
            cw = cw_ref[:, col:col + 256]
            acc = y * cw[3:4, :]
            for kk in range(1, GDN_CONV):
                ry = pltpu.roll(y, kk, 0)
                rp = pltpu.roll(prev, kk, 0)
                head = jnp.where(row8 < kk, rp, ry[:8, :])
                shifted = jnp.concatenate([head, ry[8:, :]], axis=0)
                acc = acc + shifted * cw[3 - kk:4 - kk, :]
            a = _silu(acc)
            if s < 2:
                halves = []
                for hh in range(2):
                    ah = a[:, hh * LANES:(hh + 1) * LANES]
                    ss = jnp.sum(ah * ah, axis=-1, keepdims=True)
                    scale = lax.rsqrt(ss + NORM_EPS)
                    if s == 0:
                        scale = scale * (GDN_HEAD_DIM ** -0.5)
                    halves.append(ah * scale)
                a = jnp.concatenate(halves, axis=1)
            outs[s][0, :, cc * 256:(cc + 1) * 256] = a.astype(BF16)
    for cc in range(4):
        col = 3 * GDN_WIDTH + cc * 256
        z_ref[0, :, cc * 256:(cc + 1) * 256] = _dot(xb, w_ref[:, col:col + 256]).astype(BF16)


def gdn_inproj(x, w_in, conv_w, a_log, dt_bias, tm=256):
    b, t, d = x.shape
    hk = GDN_WIDTH
    w_main = w_in[:, :4 * hk].astype(BF16)
    w_ab = jnp.zeros((d, LANES), F32).at[:, :2 * GDN_HEADS].set(w_in[:, 4 * hk:])
    wabh = w_ab.astype(BF16)
    wabl = (w_ab - wabh.astype(F32)).astype(BF16)
    alog = jnp.zeros((1, LANES), F32).at[0, :GDN_HEADS].set(a_log)
    dtb = jnp.zeros((1, LANES), F32).at[0, :GDN_HEADS].set(dt_bias)
    r = jnp.arange(tm)
    ltri = ((r[:, None] // GDN_CHUNK == r[None, :] // GDN_CHUNK) & (r[:, None] >= r[None, :])).astype(BF16)
    tok = lambda bb, tt: (bb, tt, 0)
    const2 = lambda bb, tt: (0, 0)
    act = jax.ShapeDtypeStruct((b, t, hk), BF16)
    return pl.pallas_call(
        functools.partial(_gdn_inproj_kernel, tm=tm),
        grid=(b, t // tm),
        in_specs=[
            pl.BlockSpec((1, tm, d), tok),
            pl.BlockSpec((d, 4 * hk), const2),
            pl.BlockSpec((d, LANES), const2),
            pl.BlockSpec((d, LANES), const2),
            pl.BlockSpec((GDN_CONV, 3 * hk), const2),
            pl.BlockSpec((1, LANES), const2),
            pl.BlockSpec((1, LANES), const2),
            pl.BlockSpec((tm, tm), const2),
        ],
        out_specs=[
            pl.BlockSpec((1, tm, hk), tok),
            pl.BlockSpec((1, tm, hk), tok),
            pl.BlockSpec((1, tm, hk), tok),
            pl.BlockSpec((1, tm, hk), tok),
            pl.BlockSpec((1, tm, LANES), tok),
            pl.BlockSpec((1, 2 * GDN_HEADS, tm), lambda bb, tt: (bb, 0, tt)),
        ],
        out_shape=[act, act, act, act,
                   jax.ShapeDtypeStruct((b, t, LANES), F32),
                   jax.ShapeDtypeStruct((b, 2 * GDN_HEADS, t), F32)],
        scratch_shapes=[pltpu.VMEM((12, 8, 256), F32)],
        compiler_params=_params(("arbitrary", "arbitrary")),
        name="gdn_inproj",
    )(x, w_main, wabh, wabl, conv_w, alog, dtb, ltri)


def _bmm(a, b):
    return lax.dot_general(a, b, (((2,), (1,)), ((0,), (0,))), preferred_element_type=F32)


def _bmm_nt(a, b):
    return lax.dot_general(a, b, (((2,), (2,)), ((0,), (0,))), preferred_element_type=F32)


def _gdn_rec_kernel(q_ref, k_ref, v_ref, z_ref, gcb_ref, gcT_ref, nw_ref, o_ref, s_ref, *, nc):
    c = GDN_CHUNK
    nh = GDN_HEADS

    @pl.when(pl.program_id(1) == 0)
    def _():
        s_ref[...] = jnp.zeros_like(s_ref)

    row = lax.broadcasted_iota(jnp.int32, (nc, c, c), 1)
    col = lax.broadcasted_iota(jnp.int32, (nc, c, c), 2)
    eye = (row == col).astype(F32)
    nw = nw_ref[...]
    gcb = gcb_ref[0]
    gct = gcT_ref[0]

    u_h, wq_h, qkkd_h, dlast_h = [], [], [], []
    for h in range(nh):
        sl = slice(h * LANES, (h + 1) * LANES)
        q = q_ref[0, :, sl].reshape(nc, c, LANES)
        k = k_ref[0, :, sl].reshape(nc, c, LANES)
        v = v_ref[0, :, sl].astype(F32).reshape(nc, c, LANES)
        gc = gcb[:, h:h + 1].reshape(nc, c, 1)
        beta = gcb[:, nh + h:nh + h + 1].reshape(nc, c, 1)
        gc_row = gct[:, h:h + 1, :]
        g_last = gc[:, c - 1:c, :]
        decay = jnp.exp(jnp.where(row >= col, gc - gc_row, NEG))
        eg = jnp.exp(gc)
        kf = k.astype(F32)
        kb = kf * beta
        aq = _bmm_nt(jnp.concatenate([kb.astype(BF16), q], axis=1), k)
        a = jnp.where(row > col, aq[:, :c] * decay, 0.0)
        qk = aq[:, c:] * decay
        qpow = -a
        usum = eye + qpow
        qb = qpow.astype(BF16)
        qpow = _bmm(qb, qb)
        for lvl in range(5):
            qb = qpow.astype(BF16)
            if lvl < 4:
                prod = _bmm(jnp.concatenate([usum.astype(BF16), qb], axis=1), qb)
                usum = usum + prod[:, :c]
                qpow = prod[:, c:]
            else:
                usum = usum + _bmm(usum.astype(BF16), qb)
        rhs = jnp.concatenate([v * beta, kb * eg], axis=2).astype(BF16)
        uw = _bmm(usum.astype(BF16), rhs)
        u_h.append(uw[:, :, :LANES])
        wq_h.append(jnp.concatenate([uw[:, :, LANES:], q.astype(F32) * eg], axis=1).astype(BF16))
        k_dec = kf * jnp.exp(g_last - gc)
        k_dec_t = jnp.swapaxes(k_dec, 1, 2)
        qkkd_h.append(jnp.concatenate([qk, k_dec_t], axis=1).astype(BF16))
        dlast_h.append(jnp.exp(g_last))

    for i in range(nc):
        s_old = s_ref[...]
        lhs = jnp.stack([wq_h[h][i] for h in range(nh)])
        r = _bmm(lhs, s_old.astype(BF16))
        u = jnp.stack([u_h[h][i] for h in range(nh)])
        v_new = (u - r[:, :c]).astype(BF16)
        r2 = _bmm(jnp.stack([qkkd_h[h][i] for h in range(nh)]), v_new)
        o = r[:, c:] + r2[:, :c]
        dl = jnp.stack([dlast_h[h][i] for h in range(nh)])
        s_ref[...] = s_old * dl + r2[:, c:]
        ms = jnp.mean(o * o, axis=-1, keepdims=True)
        on = o * lax.rsqrt(ms + NORM_EPS) * nw
        for h in range(nh):
            z = z_ref[0, i * c:(i + 1) * c, h * LANES:(h + 1) * LANES].astype(F32)
            o_ref[0, i * c:(i + 1) * c, h * LANES:(h + 1) * LANES] = (on[h] * _silu(z)).astype(BF16)


def gdn_rec(q, k, v, z, gcb, gcbT, norm_w, nc=16):
    b, t, hk = q.shape
    n_chunks = t // GDN_CHUNK
    tb = nc * GDN_CHUNK
    gct4 = gcbT.reshape(b, 2 * GDN_HEADS, n_chunks, GDN_CHUNK).transpose(0, 2, 1, 3)
    spec = pl.BlockSpec((1, tb, hk), lambda bb, tt: (bb, tt, 0))
    return pl.pallas_call(
        functools.partial(_gdn_rec_kernel, nc=nc),
        grid=(b, t // tb),
        in_specs=[spec, spec, spec, spec,
                  pl.BlockSpec((1, tb, LANES), lambda bb, tt: (bb, tt, 0)),
                  pl.BlockSpec((1, nc, 2 * GDN_HEADS, GDN_CHUNK), lambda bb, tt: (bb, tt, 0, 0)),
                  pl.BlockSpec((1, LANES), lambda bb, tt: (0, 0))],
        out_specs=spec,
        out_shape=jax.ShapeDtypeStruct((b, t, hk), BF16),
        scratch_shapes=[pltpu.VMEM((GDN_HEADS, GDN_HEAD_DIM, GDN_HEAD_DIM), F32)],
        compiler_params=_params(("arbitrary", "arbitrary")),
        name="gdn_rec",
    )(q, k, v, z, gcb, gct4, norm_w.reshape(1, LANES).astype(F32))


GRP_LANE0 = MOE_EXPERTS


def _route_t(lt):
    tm = lt.shape[1]
    far = float(LANES)
    row8 = lax.broadcasted_iota(jnp.int32, (8, tm), 0)
    row8f = row8.astype(F32)
    is_grp = row8 < MOE_GROUPS
    lg = jnp.where(is_grp, lt[GRP_LANE0:GRP_LANE0 + 8, :], NEG)
    eg = jnp.exp(lg - jnp.max(lg, axis=0, keepdims=True))
    pg = eg / jnp.sum(eg, axis=0, keepdims=True)
    gp = jnp.max(pg, axis=0, keepdims=True)
    gidx = jnp.min(jnp.where(is_grp & (pg == gp), row8f, far), axis=0, keepdims=True)
    rowf = lax.broadcasted_iota(jnp.int32, (MOE_EXPERTS, tm), 0).astype(F32)
    in_grp = jnp.floor(rowf * (1.0 / MOE_EPG)) == gidx
    le = jnp.where(in_grp, lt[:MOE_EXPERTS, :], NEG)
    ee = jnp.exp(le - jnp.max(le, axis=0, keepdims=True))
    pe = ee / jnp.sum(ee, axis=0, keepdims=True)
    p1 = jnp.max(jnp.where(in_grp, pe, -1.0), axis=0, keepdims=True)
    i1 = jnp.min(jnp.where(in_grp & (pe == p1), rowf, far), axis=0, keepdims=True)
    rest = in_grp & (rowf != i1)
    p2 = jnp.max(jnp.where(rest, pe, -1.0), axis=0, keepdims=True)
    i2 = jnp.min(jnp.where(rest & (pe == p2), rowf, far), axis=0, keepdims=True)
    scale = gp / (p1 + p2)
    return jnp.where(rowf == i1, p1 * scale, jnp.where(rowf == i2, p2 * scale, 0.0)), gidx


MOE_SUB = 128


def _mix_moe_ln_kernel(o_ref, wo_ref, res_ref, mg_ref, mb_ref, wrh_ref, wrl_ref, br_ref, utri_ref, wgu_ref, wd_ref,
                       g_ref, b_ref, out_ref, h_ref, xs_ref, gs_ref, ys_ref, *, tm, slots):
    hid_w = MOE_EPG * MOE_HIDDEN
    shift = MOE_SUB.bit_length() - 1

    h = _layer_norm(DEEPNORM_ALPHA * res_ref[...] + _dot(o_ref[...], wo_ref[...]), mg_ref[...], mb_ref[...])
    h_ref[...] = h
    xb, xlo = _split2(h)
    wh = wrh_ref[...]
    lt = _dot_nt(wh, xb) + (_dot_nt(wh, xlo) + _dot_nt(wrl_ref[...], xb)) + br_ref[...]
    gate, gidx = _route_t(lt)
    row8 = lax.broadcasted_iota(jnp.int32, (8, tm), 0)
    onehot = jnp.where((row8.astype(F32) == gidx) & (row8 < MOE_GROUPS), 1.0, 0.0)
    pos = _dot(onehot.astype(BF16), utri_ref[...])
    cnt = jnp.sum(onehot, axis=1, keepdims=True).astype(jnp.int32)
    offs, n_subs, off = [], [], 0
    for g in range(MOE_GROUPS):
        n_sub = lax.shift_right_logical(cnt[g, 0] + (MOE_SUB - 1), shift)
        offs.append(off)
        n_subs.append(n_sub)
        off = off + lax.shift_left(n_sub, shift)
    off_col = jnp.zeros((8, 1), jnp.int32)
    for g in range(1, MOE_GROUPS):
        off_col = jnp.where(row8[:, 0:1] == g, offs[g], off_col)
    dest_row = jnp.sum(onehot * (pos + off_col.astype(F32)), axis=0, keepdims=True)
    slot = lax.broadcasted_iota(jnp.int32, (slots, tm), 0).astype(F32)
    perm = jnp.where(slot == dest_row, 1.0, 0.0).astype(BF16)
    xs_ref[...] = _dot(perm, xb).astype(BF16)
    hi = jnp.floor(dest_row * (1.0 / 32.0))
    extra = jnp.where(row8 == 0, hi, jnp.where(row8 == 1, dest_row - 32.0 * hi, 0.0))
    per_tok = jnp.concatenate([gate, extra, jnp.zeros((LANES - MOE_EXPERTS - 8, tm), F32)], axis=0)
    gs_ref[...] = _dot_nt(per_tok.astype(BF16), perm).T
    tok = per_tok.T
    dest = 32.0 * tok[:, MOE_EXPERTS:MOE_EXPERTS + 1] + tok[:, MOE_EXPERTS + 1:MOE_EXPERTS + 2]
    ys_ref[...] = jnp.zeros_like(ys_ref)

    for g in range(MOE_GROUPS):
        def sub_tile(j, carry, g=g):
            r0 = pl.multiple_of(offs[g] + j * MOE_SUB, MOE_SUB)
            xsub = xs_ref[pl.ds(r0, MOE_SUB), :]
            gsub = gs_ref[pl.ds(r0, MOE_SUB), :]
            hids = []
            for e in range(MOE_EPG):
                ge = gsub[:, g * MOE_EPG + e:g * MOE_EPG + e + 1]
                hg = _dot(xsub, wgu_ref[g, :, e * MOE_HIDDEN:(e + 1) * MOE_HIDDEN])
                hu = _dot(xsub, wgu_ref[g, :, hid_w + e * MOE_HIDDEN:hid_w + (e + 1) * MOE_HIDDEN])
                hids.append((_silu(hg) * hu * ge).astype(BF16))
            ys_ref[pl.ds(r0, MOE_SUB), :] = _dot(jnp.concatenate(hids, axis=1), wd_ref[g]).astype(BF16)
            return carry

        lax.fori_loop(0, n_subs[g], sub_tile, 0)

    slot_l = lax.broadcasted_iota(jnp.int32, (tm, slots), 1).astype(F32)
    unperm = jnp.where(slot_l == dest, 1.0, 0.0).astype(BF16)
    y = _dot(unperm, ys_ref[...])
    out_ref[...] = _layer_norm(DEEPNORM_ALPHA * h_ref[...] + y, g_ref[...], b_ref[...])


def mix_moe_ln(o, w_o, res, mix_g, mix_b, w_grp, b_grp, w_rt, b_rt, w_gate, w_up, w_down, ln_g, ln_b, tm=512):
    n, d = res.shape
    kdim = o.shape[1]
    slots = tm + MOE_GROUPS * MOE_SUB
    wr = jnp.zeros((LANES, d), F32).at[:MOE_EXPERTS].set(w_rt.T).at[GRP_LANE0:GRP_LANE0 + MOE_GROUPS].set(w_grp.T)
    br = jnp.zeros((LANES, 1), F32).at[:MOE_EXPERTS, 0].set(b_rt).at[GRP_LANE0:GRP_LANE0 + MOE_GROUPS, 0].set(b_grp)
    wrh = wr.astype(BF16)
    wrl = (wr - wrh.astype(F32)).astype(BF16)
    hid_w = MOE_EPG * MOE_HIDDEN
    by_group = lambda w: w.reshape(MOE_GROUPS, MOE_EPG, d, MOE_HIDDEN).transpose(0, 2, 1, 3).reshape(MOE_GROUPS, d, hid_w)
    wgu = jnp.concatenate([by_group(w_gate), by_group(w_up)], axis=-1).astype(BF16)
    wd = w_down.reshape(MOE_GROUPS, hid_w, d).astype(BF16)
    r = jnp.arange(tm)
    utri = (r[:, None] < r[None, :]).astype(BF16)
    row = lambda i: (i, 0)
    once = pl.Buffered(1)
    const2 = lambda shape: pl.BlockSpec(shape, lambda i: (0, 0), pipeline_mode=once)
    const3 = lambda shape: pl.BlockSpec(shape, lambda i: (0, 0, 0), pipeline_mode=once)
    return pl.pallas_call(
        functools.partial(_mix_moe_ln_kernel, tm=tm, slots=slots),
        grid=(n // tm,),
        in_specs=[pl.BlockSpec((tm, kdim), row), const2((kdim, d)), pl.BlockSpec((tm, d), row),
                  const2((1, d)), const2((1, d)),
                  const2((LANES, d)), const2((LANES, d)), const2((LANES, 1)),
                  const2((tm, tm)),
                  const3((MOE_GROUPS, d, 2 * hid_w)), const3((MOE_GROUPS, hid_w, d)),
                  const2((1, d)), const2((1, d))],
        out_specs=pl.BlockSpec((tm, d), row),
        out_shape=jax.ShapeDtypeStruct((n, d), F32),
        scratch_shapes=[pltpu.VMEM((tm, d), F32),
                        pltpu.VMEM((slots, d), BF16), pltpu.VMEM((slots, LANES), F32), pltpu.VMEM((slots, d), BF16)],
        compiler_params=_params(("arbitrary",)),
        name="mix_moe_ln",
    )(o, w_o.astype(BF16), res, mix_g.reshape(1, d), mix_b.reshape(1, d), wrh, wrl, br, utri, wgu, wd,
      ln_g.reshape(1, d), ln_b.reshape(1, d))


KV_W = NSA_GROUPS * NSA_HEAD_DIM
GATE_ROWS = 16


def _nsa_proj_kernel(h_ref, wq_ref, wg_ref, wc_ref, wks_ref, wkw_ref, wvs_ref, wvw_ref,
                     qT_ref, gT_ref, xk_ref, xv_ref, ksa_ref, kwa_ref, vsT_ref, vwT_ref, cmp_scr, *, tm):
    t0 = pl.program_id(1) * tm
    hb = h_ref[0].astype(BF16)
    qT_ref[0] = _dot_nt(wq_ref[...], hb).astype(BF16)
    gT_ref[0] = jax.nn.sigmoid(_dot_nt(wg_ref[...], hb))
    craw = _dot(hb, wc_ref[...])
    n_cb = 2 * KV_W // LANES
    for cb in range(n_cb):
        cmp_scr[cb] = craw[:, cb * LANES:(cb + 1) * LANES]
    nrow = tm // CMP_STRIDE
    lane_r = lax.broadcasted_iota(jnp.int32, (nrow, LANES), 1)
    for j in range(CMP_STRIDE // 2):
        for cb in range(n_cb):
            a = cmp_scr[cb, pl.ds(2 * j, nrow, stride=CMP_STRIDE), :]
            bm = cmp_scr[cb, pl.ds(2 * j + 1, nrow, stride=CMP_STRIDE), :]
            pieces = (jnp.where(lane_r < NSA_HEAD_DIM, a, pltpu.roll(bm, NSA_HEAD_DIM, 1)),
                      jnp.where(lane_r < NSA_HEAD_DIM, pltpu.roll(a, NSA_HEAD_DIM, 1), bm))
            for k, val in enumerate(pieces):
                pg = 2 * cb + k
                ref = xk_ref if pg < NSA_GROUPS else xv_ref
                ref[0, pg % NSA_GROUPS, :, j * LANES:(j + 1) * LANES] = val.astype(BF16)
    vsT_ref[0] = _dot_nt(wvs_ref[...], hb).astype(BF16)
    vwT_ref[0] = _dot_nt(wvw_ref[...], hb).astype(BF16)
    ks = _dot(hb, wks_ref[...])
    kw = _dot(hb, wkw_ref[...])
    lane = lax.broadcasted_iota(jnp.int32, (tm, LANES), 1)
    blk = (t0 + lax.broadcasted_iota(jnp.int32, (tm, LANES), 0)) // SEL_BLOCK
    onehot = jnp.where(lane - NSA_HEAD_DIM == blk, 1.0, 0.0)
    for g in range(NSA_GROUPS):
        ksg = ks[:, g * LANES:(g + 1) * LANES]
        ksa_ref[0, g] = jnp.where(lane < NSA_HEAD_DIM, ksg, onehot).astype(BF16)
        kwa_ref[0, g] = kw[:, g * LANES:(g + 1) * LANES].astype(BF16)


def _pad_group_cols(w):
    d = w.shape[0]
    w4 = w.reshape(d, NSA_GROUPS, NSA_HEAD_DIM)
    return jnp.concatenate([w4, jnp.zeros_like(w4)], axis=-1).reshape(d, NSA_GROUPS * LANES)


def nsa_proj(h, w_kv, w_in, tm=256):
    b, t, d = h.shape
    scale = NSA_HEAD_DIM ** -0.5 * LOG2E
    wq = (w_in[:, :NSA_WIDTH] * scale).T.astype(BF16)
    wgate = w_in[:, NSA_WIDTH:].reshape(d, NSA_GROUPS, NSA_HPG, N_BRANCHES)
    wgate = wgate.transpose(0, 1, 3, 2).reshape(d, NSA_GROUPS, N_BRANCHES * NSA_HPG)
    wgate = jnp.concatenate([wgate, jnp.zeros((d, NSA_GROUPS, GATE_ROWS - N_BRANCHES * NSA_HPG), F32)], axis=-1)
    wg = wgate.reshape(d, NSA_GROUPS * GATE_ROWS).T.astype(BF16)
    part = lambda p: w_kv[:, p * KV_W:(p + 1) * KV_W]
    wc = w_kv[:, :2 * KV_W].astype(BF16)
    wks = _pad_group_cols(part(2)).astype(BF16)
    wvs = part(3).T.astype(BF16)
    wkw = _pad_group_cols(part(4)).astype(BF16)
    wvw = part(5).T.astype(BF16)
    const = lambda bb, tt: (0, 0)
    tok = lambda bb, tt: (bb, tt, 0)
    tr = lambda bb, tt: (bb, 0, tt)
    g4 = lambda bb, tt: (bb, 0, tt, 0)
    ng = NSA_GROUPS * GATE_ROWS
    wide = CMP_STRIDE * NSA_HEAD_DIM
    return pl.pallas_call(
        functools.partial(_nsa_proj_kernel, tm=tm),
        grid=(b, t // tm),
        in_specs=[pl.BlockSpec((1, tm, d), tok),
                  pl.BlockSpec((NSA_WIDTH, d), const), pl.BlockSpec((ng, d), const),
                  pl.BlockSpec((d, 2 * KV_W), const),
                  pl.BlockSpec((d, NSA_GROUPS * LANES), const), pl.BlockSpec((d, NSA_GROUPS * LANES), const),
                  pl.BlockSpec((KV_W, d), const), pl.BlockSpec((KV_W, d), const)],
        out_specs=[pl.BlockSpec((1, NSA_WIDTH, tm), tr), pl.BlockSpec((1, ng, tm), tr),
                   pl.BlockSpec((1, NSA_GROUPS, tm // CMP_STRIDE, wide), g4),
                   pl.BlockSpec((1, NSA_GROUPS, tm // CMP_STRIDE, wide), g4),
                   pl.BlockSpec((1, NSA_GROUPS, tm, LANES), g4), pl.BlockSpec((1, NSA_GROUPS, tm, LANES), g4),
                   pl.BlockSpec((1, KV_W, tm), tr), pl.BlockSpec((1, KV_W, tm), tr)],
        out_shape=[jax.ShapeDtypeStruct((b, NSA_WIDTH, t), BF16), jax.ShapeDtypeStruct((b, ng, t), F32),
                   jax.ShapeDtypeStruct((b, NSA_GROUPS, t // CMP_STRIDE, wide), BF16),
                   jax.ShapeDtypeStruct((b, NSA_GROUPS, t // CMP_STRIDE, wide), BF16),
                   jax.ShapeDtypeStruct((b, NSA_GROUPS, t, LANES), BF16),
                   jax.ShapeDtypeStruct((b, NSA_GROUPS, t, LANES), BF16),
                   jax.ShapeDtypeStruct((b, KV_W, t), BF16), jax.ShapeDtypeStruct((b, KV_W, t), BF16)],
        scratch_shapes=[pltpu.VMEM((2 * KV_W // LANES, tm, LANES), F32)],
        compiler_params=_params(("arbitrary", "arbitrary")),
        name="nsa_proj",
    )(h, wq, wg, wc, wks, wkw, wvs, wvw)


N_CMP_PAD = 128


def _nsa_compress_kernel(xk_ref, xv_ref, pk_ref, pv_ref, w1k_ref, w1v_ref, w2k_ref, w2vT_ref, kc_ref, vcT_ref):
    half = CMP_STRIDE * NSA_HEAD_DIM
    pbk = _dot(pk_ref[...], w1k_ref[...])[0:1, :]
    pbv = _dot(pv_ref[...], w1v_ref[...])[0:1, :]
    for g in range(NSA_GROUPS):
        xk = xk_ref[0, g]
        hk = _dot(xk, w1k_ref[:half, :]) + pltpu.roll(_dot(xk, w1k_ref[half:, :]), N_CMP_PAD - 1, 0) + pbk
        kc_ref[0, g] = _dot(_silu(hk).astype(BF16), w2k_ref[...]).astype(BF16)
        xv = xv_ref[0, g]
        hv = _dot(xv, w1v_ref[:half, :]) + pltpu.roll(_dot(xv, w1v_ref[half:, :]), N_CMP_PAD - 1, 0) + pbv
        vcT_ref[0, g] = _dot_nt(w2vT_ref[...], _silu(hv).astype(BF16)).astype(BF16)


def nsa_compress(xk, xv, k_pos, k_w1, k_w2, v_pos, v_w1, v_w2):
    b, _, nrow, wide = xk.shape
    assert nrow == N_CMP_PAD
    pad_pos = lambda p: jnp.zeros((8, CMP_BLOCK * NSA_HEAD_DIM), F32).at[0].set(p.reshape(-1)).astype(BF16)
    spec_x = pl.BlockSpec((1, NSA_GROUPS, nrow, wide), lambda bb: (bb, 0, 0, 0))
    c2 = lambda bb: (0, 0)
    return pl.pallas_call(
        _nsa_compress_kernel,
        grid=(b,),
        in_specs=[spec_x, spec_x,
                  pl.BlockSpec((8, 2 * wide), c2), pl.BlockSpec((8, 2 * wide), c2),
                  pl.BlockSpec((2 * wide, CMP_HIDDEN), c2), pl.BlockSpec((2 * wide, CMP_HIDDEN), c2),
                  pl.BlockSpec((CMP_HIDDEN, NSA_HEAD_DIM), c2), pl.BlockSpec((NSA_HEAD_DIM, CMP_HIDDEN), c2)],
        out_specs=[pl.BlockSpec((1, NSA_GROUPS, N_CMP_PAD, NSA_HEAD_DIM), lambda bb: (bb, 0, 0, 0)),
                   pl.BlockSpec((1, NSA_GROUPS, NSA_HEAD_DIM, N_CMP_PAD), lambda bb: (bb, 0, 0, 0))],
        out_shape=[jax.ShapeDtypeStruct((b, NSA_GROUPS, N_CMP_PAD, NSA_HEAD_DIM), BF16),
                   jax.ShapeDtypeStruct((b, NSA_GROUPS, NSA_HEAD_DIM, N_CMP_PAD), BF16)],
        compiler_params=_params(("arbitrary",)),
        name="nsa_compress",
    )(xk, xv, pad_pos(k_pos), pad_pos(v_pos), k_w1.astype(BF16), v_w1.astype(BF16),
      k_w2.astype(BF16), v_w2.T.astype(BF16))


SEL_CHUNK = 256
WIN_CHUNK = 128
TB_ROWS = WINDOW + SEL_CHUNK
TBC_ROWS = 256
TBC_OFF = 120
N_SEL_BLOCKS = 32
QW = NSA_HPG * Q_BLOCK


def _t5_bucket(dist):
    n = jnp.maximum(dist, 0)
    max_exact = REL_BUCKETS // 2
    nf = jnp.maximum(n, 1).astype(F32)
    large = max_exact + (jnp.log(nf / max_exact) / math.log(REL_MAX_DIST / max_exact)
                         * (REL_BUCKETS - max_exact)).astype(jnp.int32)
    large = jnp.minimum(large, REL_BUCKETS - 1)
    return jnp.where(n < max_exact, n, large)


def _bias_tables(rel_bias):
    rel = rel_bias.astype(F32)
    far = rel[REL_BUCKETS - 1]
    ql = jnp.arange(Q_BLOCK)
    heads = jnp.arange(NSA_HEADS).reshape(NSA_GROUPS, NSA_HPG)

    def table(dist, valid):
        bucket = _t5_bucket(dist)[None, :, None, :]
        relc = ((rel - far) * LOG2E)[:, heads][:, :, None, :, None]
        bias = jnp.zeros((NSA_GROUPS, dist.shape[0], NSA_HPG, Q_BLOCK), F32)
        for b in range(REL_BUCKETS):
            bias = jnp.where(bucket == b, relc[b], bias)
        bias = jnp.where(valid[None, :, None, :], bias, NEG)
        return bias.reshape(NSA_GROUPS, dist.shape[0], QW)

    d = ql[None, :] - (jnp.arange(TB_ROWS)[:, None] - WINDOW)
    tb = table(d, (d >= 0) & (d < WINDOW))
    m = jnp.arange(TBC_ROWS)[:, None] - TBC_OFF
    dc = ql[None, :] - CMP_STRIDE * m - (CMP_BLOCK - 1)
    tbc = table(dc, dc >= 0)
    return tb, tbc


def _nsa_attn_kernel(qT_ref, gT_ref, kc_ref, vcT_ref, ksa_ref, vsT_ref, kwa_ref, vwT_ref, tb_ref, tbc_ref, ovl_ref,
                     o_ref, m_ref, l_ref, acc_ref, ot_ref, oT_ref, sc_ref, selm_ref):
    c = pl.program_id(1)
    ng, hpg, hd = NSA_GROUPS, NSA_HPG, NSA_HEAD_DIM
    nsel = N_SEL_BLOCKS
    n_top = min(TOP_N, nsel)
    n_win = WINDOW // WIN_CHUNK

    q4 = jnp.stack([jnp.concatenate([qT_ref[0, (g * hpg + hh) * hd:(g * hpg + hh + 1) * hd, :]
                                     for hh in range(hpg)], axis=1) for g in range(ng)])
    gates = gT_ref[0]

    def gate3(br):
        return jnp.stack([jnp.concatenate(
            [gates[g * GATE_ROWS + br * hpg + hh:g * GATE_ROWS + br * hpg + hh + 1, :] for hh in range(hpg)], axis=1)
            for g in range(ng)])

    def fresh():
        return (jnp.full((ng, 1, QW), NEG, F32), jnp.zeros((ng, 1, QW), F32), jnp.zeros((ng, hd, QW), F32))

    def load_state():
        return m_ref[...], l_ref[...], acc_ref[...]

    def store_state(state):
        m_ref[...], l_ref[...], acc_ref[...] = state

    def upd(state, s, vT):
        m_old, l_old, acc_old = state
        m_new = jnp.maximum(m_old, jnp.max(s, axis=1, keepdims=True))
        alpha = jnp.exp2(m_old - m_new)
        p = jnp.exp2(s - m_new)
        return (m_new, l_old * alpha + jnp.sum(p, axis=1, keepdims=True),
                acc_old * alpha + _bmm(vT, p.astype(BF16)))

    def result(state, br):
        _, l_fin, acc_fin = state
        return gate3(br) * (acc_fin * (1.0 / l_fin))

    start = pl.multiple_of(TBC_OFF - 8 * c, 8)
    tbc = tbc_ref[:, pl.ds(start, N_CMP_PAD), :]
    nrow = lax.broadcasted_iota(jnp.int32, tbc.shape, 1)
    tbc = jnp.where(nrow < N_CMP_PAD - 1, tbc, NEG)
    s = _bmm(kc_ref[0], q4) + tbc
    valid = tbc > 0.5 * NEG
    e = jnp.where(valid, jnp.exp2(s - jnp.max(s, axis=1, keepdims=True)), 0.0)
    l = jnp.sum(e, axis=1, keepdims=True)
    p = e * (1.0 / jnp.where(l > 0.0, l, 1.0))
    o_cmp = gate3(0) * _bmm(vcT_ref[0], p.astype(BF16))

    psum = p[:, :, 0:Q_BLOCK]
    for hh in range(1, hpg):
        psum = psum + p[:, :, hh * Q_BLOCK:(hh + 1) * Q_BLOCK]
    psum = jnp.concatenate([psum[g] for g in range(ng)], axis=1)
    ph, plo = _split2(psum)
    ovl = ovl_ref[...]
    imp = _dot(ovl, ph) + _dot(ovl, plo)
    jidx = lax.broadcasted_iota(jnp.int32, imp.shape, 0)
    qlane = lax.broadcasted_iota(jnp.int32, imp.shape, 1) & (Q_BLOCK - 1)
    jq = 2 * c + (qlane >= SEL_BLOCK).astype(jnp.int32)
    forced = (jidx == 0) | (jidx == jq) | (jidx == jq - 1)
    imp = jnp.where(forced, -NEG, jnp.where(jidx <= jq, imp, NEG))
    selm_ref[...] = jnp.zeros_like(selm_ref)

    @pl.when(2 * c + 2 > n_top)
    def _():
        sub8 = lax.broadcasted_iota(jnp.int32, (8, imp.shape[1]), 0)
        rows = [imp[8 * v:8 * v + 8, :] for v in range(nsel // 8)]
        cnts = [jnp.zeros_like(r) for r in rows]
        for i in range(nsel):
            r = imp[i:i + 1, :]
            for v in range(nsel // 8):
                ge = jnp.where(r >= rows[v], 1.0, 0.0)
                gt = jnp.where(r > rows[v], 1.0, 0.0)
                if i < 8 * v:
                    ahead = ge
                elif i >= 8 * v + 8:
                    ahead = gt
                else:
                    ahead = jnp.where(sub8 > i - 8 * v, ge, gt)
                cnts[v] = cnts[v] + ahead
        cnt = jnp.concatenate(cnts, axis=0)
        selm_ref[...] = jnp.where(cnt < float(n_top), 0.0, NEG)

    selm = selm_ref[...].astype(BF16)
    pad = jnp.zeros((LANES - hd - nsel, QW), BF16)
    q_aug = jnp.stack([jnp.concatenate(
        [q4[g], jnp.concatenate([selm[:, g * Q_BLOCK:(g + 1) * Q_BLOCK]] * hpg, axis=1), pad], axis=0)
        for g in range(ng)])

    def sel_scores(k0):
        return _bmm(ksa_ref[0, :, pl.ds(k0, SEL_CHUNK), :], q_aug)

    def sel_values(k0):
        return vsT_ref[0, :, pl.ds(k0, SEL_CHUNK)].reshape(ng, hd, SEL_CHUNK)

    def win(first, n_chunks, tab_chunk):
        width = n_chunks * WIN_CHUNK
        k0 = pl.multiple_of((c - n_win + first) * WIN_CHUNK, WIN_CHUNK)
        s_w = _bmm(kwa_ref[0, :, pl.ds(k0, width), :], q_aug)
        if tab_chunk is not None:
            lo = (tab_chunk - first) * WIN_CHUNK
            part = s_w[:, lo:lo + WIN_CHUNK] + tb_ref[:, tab_chunk * WIN_CHUNK:(tab_chunk + 1) * WIN_CHUNK, :]
            pieces = ([s_w[:, :lo]] if lo else []) + [part] + ([s_w[:, lo + WIN_CHUNK:]] if lo + WIN_CHUNK < width else [])
            s_w = jnp.concatenate(pieces, axis=1) if len(pieces) > 1 else part
        return s_w, vwT_ref[0, :, pl.ds(k0, width)].reshape(ng, hd, width)

    def raw_scores(i):
        return sel_scores(pl.multiple_of(i * SEL_CHUNK, SEL_CHUNK))

    def values(i):
        return sel_values(pl.multiple_of(i * SEL_CHUNK, SEL_CHUNK))

    def sel_table(i):
        tstart = pl.multiple_of(WINDOW - (c * Q_BLOCK - i * SEL_CHUNK), Q_BLOCK)
        return tb_ref[:, pl.ds(tstart, SEL_CHUNK), :]

    store_state(fresh())
    n_far = jnp.maximum(c // 2 - 1, 0)

    @pl.when(c >= 2)
    def _():
        sc_ref[0] = raw_scores(0)

    def far_pair(j, carry):
        i = 2 * j
        sc_ref[1] = raw_scores(i + 1)
        store_state(upd(load_state(), sc_ref[0], values(i)))
        sc_ref[0] = raw_scores(i + 2)
        store_state(upd(load_state(), sc_ref[1], values(i + 1)))
        return carry

    lax.fori_loop(0, n_far // 2, far_pair, 0)

    @pl.when(n_far % 2 == 1)
    def _():
        sc_ref[1] = raw_scores(n_far)
        store_state(upd(load_state(), sc_ref[0], values(n_far - 1)))

    @pl.when(c >= n_win)
    def _():
        s_a = sc_ref[n_far % 2] + sel_table(c // 2 - 1)
        s_b = raw_scores(c // 2)
        w_a, wv_a = win(0, 2, 0)
        st_s = upd(load_state(), s_a, values(c // 2 - 1))
        w_b, wv_b = win(2, 2, 3)
        st_w = upd(fresh(), w_a, wv_a)
        st_s = upd(st_s, s_b + sel_table(c // 2), values(c // 2))
        w_c, wv_c = win(4, 1, 4)
        st_w = upd(st_w, w_b, wv_b)
        st_w = upd(st_w, w_c, wv_c)
        ot_ref[...] = o_cmp + result(st_s, 1) + result(st_w, 2)

    @pl.when(c < n_win)
    def _():
        @pl.when(c >= 2)
        def _():
            store_state(upd(load_state(), sc_ref[0] + sel_table(c // 2 - 1), values(c // 2 - 1)))

        ot_ref[...] = o_cmp + result(upd(load_state(), raw_scores(c // 2) + sel_table(c // 2), values(c // 2)), 1)
        store_state(fresh())
        pl.when(c == 3)(lambda: store_state(upd(load_state(), *win(1, 1, None))))
        pl.when(c >= 2)(lambda: store_state(upd(load_state(), *win(2, 2, 3))))
        pl.when(c == 1)(lambda: store_state(upd(load_state(), *win(3, 1, 3))))
        ot_ref[...] += result(upd(load_state(), *win(4, 1, 4)), 2)

    for g in range(ng):
        for hh in range(hpg):
            oT_ref[hh, g * hd:(g + 1) * hd, :] = ot_ref[g, :, hh * Q_BLOCK:(hh + 1) * Q_BLOCK]
    for hh in range(hpg):
        o_ref[0, :, hh * KV_W:(hh + 1) * KV_W] = oT_ref[hh].T.astype(BF16)


def nsa_attn(qT, gT, kc, vcT, ksa, vsT, kwa, vwT, rel_bias):
    b, _, t = qT.shape
    nq = t // Q_BLOCK
    tb, tbc = _bias_tables(rel_bias)
    nc = (t - CMP_BLOCK) // CMP_STRIDE + 1
    cs = jnp.arange(N_CMP_PAD) * CMP_STRIDE
    ss = jnp.arange(N_SEL_BLOCKS) * SEL_BLOCK
    ovl = ((cs[None, :] < ss[:, None] + SEL_BLOCK) & (cs[None, :] + CMP_BLOCK - 1 >= ss[:, None])
           & (jnp.arange(N_CMP_PAD)[None, :] < nc)).astype(BF16)
    ng = NSA_GROUPS * GATE_ROWS
    per_b3 = lambda bb, cc: (bb, 0, 0)
    per_b4 = lambda bb, cc: (bb, 0, 0, 0)
    c3 = lambda bb, cc: (0, 0, 0)
    return pl.pallas_call(
        _nsa_attn_kernel,
        grid=(b, nq),
        in_specs=[pl.BlockSpec((1, NSA_WIDTH, Q_BLOCK), lambda bb, cc: (bb, 0, cc)),
                  pl.BlockSpec((1, ng, Q_BLOCK), lambda bb, cc: (bb, 0, cc)),
                  pl.BlockSpec((1, NSA_GROUPS, N_CMP_PAD, NSA_HEAD_DIM), per_b4),
                  pl.BlockSpec((1, NSA_GROUPS, NSA_HEAD_DIM, N_CMP_PAD), per_b4),
                  pl.BlockSpec((1, NSA_GROUPS, t, LANES), per_b4),
                  pl.BlockSpec((1, KV_W, t), per_b3),
                  pl.BlockSpec((1, NSA_GROUPS, t, LANES), per_b4),
                  pl.BlockSpec((1, KV_W, t), per_b3),
                  pl.BlockSpec((NSA_GROUPS, TB_ROWS, QW), c3),
                  pl.BlockSpec((NSA_GROUPS, TBC_ROWS, QW), c3),
                  pl.BlockSpec((N_SEL_BLOCKS, N_CMP_PAD), lambda bb, cc: (0, 0))],
        out_specs=pl.BlockSpec((1, Q_BLOCK, NSA_WIDTH), lambda bb, cc: (bb, cc, 0)),
        out_shape=jax.ShapeDtypeStruct((b, t, NSA_WIDTH), BF16),
        scratch_shapes=[pltpu.VMEM((NSA_GROUPS, 1, QW), F32), pltpu.VMEM((NSA_GROUPS, 1, QW), F32),
                        pltpu.VMEM((NSA_GROUPS, NSA_HEAD_DIM, QW), F32),
                        pltpu.VMEM((NSA_GROUPS, NSA_HEAD_DIM, QW), F32),
                        pltpu.VMEM((NSA_HPG, KV_W, Q_BLOCK), F32),
                        pltpu.VMEM((2, NSA_GROUPS, SEL_CHUNK, QW), F32),
                        pltpu.VMEM((N_SEL_BLOCKS, NSA_GROUPS * Q_BLOCK), F32)],
        compiler_params=_params(("arbitrary", "arbitrary")),
        name="nsa_attn",
    )(qT, gT, kc, vcT, ksa, vsT, kwa, vwT, tb, tbc, ovl)


def kernel(x, gdn_w_in, gdn_conv_w, gdn_a_log, gdn_dt_bias, gdn_norm_w, gdn_w_o, nsa_w_kv, cmp_k_pos, cmp_k_w1,
           cmp_k_w2, cmp_v_pos, cmp_v_w1, cmp_v_w2, nsa_w_in, nsa_w_o, rel_bias, ln_mix_g, ln_mix_b, ln_ffn_g,
           ln_ffn_b, moe_w_grp, moe_b_grp, moe_w_rt, moe_b_rt, moe_w_gate, moe_w_up, moe_w_down):
    b, t, d = x.shape
    n = b * t

    def mix_ffn(o, w_o, res, layer):
        return mix_moe_ln(o, w_o, res, ln_mix_g[layer], ln_mix_b[layer], moe_w_grp[layer], moe_b_grp[layer],
                          moe_w_rt[layer], moe_b_rt[layer], moe_w_gate[layer], moe_w_up[layer], moe_w_down[layer],
                          ln_ffn_g[layer], ln_ffn_b[layer])

    q, k, v, z, gcb, gcbT = gdn_inproj(x, gdn_w_in[0], gdn_conv_w[0], gdn_a_log[0], gdn_dt_bias[0])
    o = gdn_rec(q, k, v, z, gcb, gcbT, gdn_norm_w[0])
    h = mix_ffn(o.reshape(n, GDN_WIDTH), gdn_w_o[0], x.reshape(n, d), 0)

    qT, gT, xk, xv, ksa, kwa, vsT, vwT = nsa_proj(h.reshape(b, t, d), nsa_w_kv, nsa_w_in[0])
    kc, vcT = nsa_compress(xk, xv, cmp_k_pos, cmp_k_w1, cmp_k_w2, cmp_v_pos, cmp_v_w1, cmp_v_w2)
    o = nsa_attn(qT, gT, kc, vcT, ksa, vsT, kwa, vwT, rel_bias)
    w_o = nsa_w_o[0].reshape(NSA_GROUPS, NSA_HPG, NSA_HEAD_DIM, d).transpose(1, 0, 2, 3).reshape(NSA_WIDTH, d)
    h = mix_ffn(o.reshape(n, NSA_WIDTH), w_o, h, 1)
    return h.reshape(b, t, d)
```

```python
import functools
import math

import jax
import jax.numpy as jnp
from jax import lax
from jax.experimental import pallas as pl
from jax.experimental.pallas import tpu as pltpu

F32 = jnp.float32
BF16 = jnp.bfloat16

DEPTH = 2
GDN_HEADS = 8
GDN_HEAD_DIM = 128
GDN_WIDTH = GDN_HEADS * GDN_HEAD_DIM
GDN_CONV = 4
GDN_CHUNK = 64
NSA_HEADS = 16
NSA_GROUPS = 4
NSA_HPG = NSA_HEADS // NSA_GROUPS
NSA_HEAD_DIM = 64
NSA_WIDTH = NSA_HEADS * NSA_HEAD_DIM
CMP_BLOCK = 32
CMP_STRIDE = 16
CMP_HIDDEN = 2 * NSA_HEAD_DIM
SEL_BLOCK = 64
TOP_N = 16
WINDOW = 512
Q_BLOCK = 128
N_BRANCHES = 3
REL_BUCKETS = 32
REL_MAX_DIST = 128
MOE_GROUPS = 4
MOE_EPG = 4
MOE_EXPERTS = MOE_GROUPS * MOE_EPG
MOE_HIDDEN = 256
DEEPNORM_ALPHA = (2 * DEPTH) ** 0.25
LN_EPS = 1e-5
NORM_EPS = 1e-6
NEG = -1e30
LOG2E = math.log2(math.e)

LANES = 128
VMEM_LIMIT = 56 * 1024 * 1024


def _dot(a, b):
    return jnp.dot(a, b, preferred_element_type=F32)


def _dot_nt(a, b):
    return lax.dot_general(a, b, (((1,), (1,)), ((), ())), preferred_element_type=F32)


def _split2(x):
    hi = x.astype(BF16)
    lo = (x - hi.astype(F32)).astype(BF16)
    return hi, lo


def _silu(x):
    return x * jax.nn.sigmoid(x)


def _params(sem):
    return pltpu.CompilerParams(dimension_semantics=sem, vmem_limit_bytes=VMEM_LIMIT)


def _layer_norm(x, g, b):
    mu = jnp.mean(x, axis=-1, keepdims=True)
    xc = x - mu
    var = jnp.mean(xc * xc, axis=-1, keepdims=True)
    return xc * lax.rsqrt(var + LN_EPS) * g + b


def _gdn_inproj_kernel(x_ref, w_ref, wabh_ref, wabl_ref, cw_ref, alog_ref, dtb_ref, ltri_ref,
                       q_ref, k_ref, v_ref, z_ref, gcb_ref, gcbT_ref, carry_ref, *, tm):
    @pl.when(pl.program_id(1) == 0)
    def _():
        carry_ref[...] = jnp.zeros_like(carry_ref)

    x = x_ref[0]
    xb, xlo = _split2(x)

    wh = wabh_ref[...]
    ab = _dot(xb, wh) + (_dot(xlo, wh) + _dot(xb, wabl_ref[...]))
    lane = lax.broadcasted_iota(jnp.int32, ab.shape, 1)
    sp_in = ab + dtb_ref[...]
    softplus = jnp.maximum(sp_in, 0.0) + jnp.log1p(jnp.exp(-jnp.abs(sp_in)))
    g = jnp.where(lane < GDN_HEADS, -jnp.exp(alog_ref[...]) * softplus, 0.0)
    beta = jax.nn.sigmoid(ab)
    g1 = g.astype(BF16)
    r1 = g - g1.astype(F32)
    g2 = r1.astype(BF16)
    g3 = (r1 - g2.astype(F32)).astype(BF16)
    ltri = ltri_ref[...]
    gc = _dot(ltri, g1) + (_dot(ltri, g2) + _dot(ltri, g3))
    gcb = jnp.where(lane < GDN_HEADS, gc, jnp.where(lane < 2 * GDN_HEADS, beta, 0.0))
    gcb_ref[0] = gcb
    gcbT_ref[0] = gcb.T[:2 * GDN_HEADS, :]

    row8 = lax.broadcasted_iota(jnp.int32, (8, 256), 0)
    outs = (q_ref, k_ref, v_ref)
    for s in range(3):
        for cc in range(4):
            col = s * GDN_WIDTH + cc * 256
            y = _dot(xb, w_ref[:, col:col + 256])
            prev = carry_ref[s * 4 + cc]
            carry_ref[s * 4 + cc] = y[tm - 8:, :]
            cw = cw_ref[:, col:col + 256]
            acc = y * cw[3:4, :]
            for kk in range(1, GDN_CONV):
                ry = pltpu.roll(y, kk, 0)
                rp = pltpu.roll(prev, kk, 0)
                head = jnp.where(row8 < kk, rp, ry[:8, :])
                shifted = jnp.concatenate([head, ry[8:, :]], axis=0)
                acc = acc + shifted * cw[3 - kk:4 - kk, :]
            a = _silu(acc)
            if s < 2:
                halves = []
                for hh in range(2):
                    ah = a[:, hh * LANES:(hh + 1) * LANES]
                    ss = jnp.sum(ah * ah, axis=-1, keepdims=True)
                    scale = lax.rsqrt(ss + NORM_EPS)
                    if s == 0:
                        scale = scale * (GDN_HEAD_DIM ** -0.5)
                    halves.append(ah * scale)
                a = jnp.concatenate(halves, axis=1)
            outs[s][0, :, cc * 256:(cc + 1) * 256] = a.astype(BF16)
    for cc in range(4):
        col = 3 * GDN_WIDTH + cc * 256
        z_ref[0, :, cc * 256:(cc + 1) * 256] = _dot(xb, w_ref[:, col:col + 256]).astype(BF16)


def gdn_inproj(x, w_in, conv_w, a_log, dt_bias, tm=256):
    b, t, d = x.shape
    hk = GDN_WIDTH
    w_main = w_in[:, :4 * hk].astype(BF16)
    w_ab = jnp.zeros((d, LANES), F32).at[:, :2 * GDN_HEADS].set(w_in[:, 4 * hk:])
    wabh = w_ab.astype(BF16)
    wabl = (w_ab - wabh.astype(F32)).astype(BF16)
    alog = jnp.zeros((1, LANES), F32).at[0, :GDN_HEADS].set(a_log)
    dtb = jnp.zeros((1, LANES), F32).at[0, :GDN_HEADS].set(dt_bias)
    r = jnp.arange(tm)
    ltri = ((r[:, None] // GDN_CHUNK == r[None, :] // GDN_CHUNK) & (r[:, None] >= r[None, :])).astype(BF16)
    tok = lambda bb, tt: (bb, tt, 0)
    const2 = lambda bb, tt: (0, 0)
    act = jax.ShapeDtypeStruct((b, t, hk), BF16)
    return pl.pallas_call(
        functools.partial(_gdn_inproj_kernel, tm=tm),
        grid=(b, t // tm),
        in_specs=[
            pl.BlockSpec((1, tm, d), tok),
            pl.BlockSpec((d, 4 * hk), const2),
            pl.BlockSpec((d, LANES), const2),
            pl.BlockSpec((d, LANES), const2),
            pl.BlockSpec((GDN_CONV, 3 * hk), const2),
            pl.BlockSpec((1, LANES), const2),
            pl.BlockSpec((1, LANES), const2),
            pl.BlockSpec((tm, tm), const2),
        ],
        out_specs=[
            pl.BlockSpec((1, tm, hk), tok),
            pl.BlockSpec((1, tm, hk), tok),
            pl.BlockSpec((1, tm, hk), tok),
            pl.BlockSpec((1, tm, hk), tok),
            pl.BlockSpec((1, tm, LANES), tok),
            pl.BlockSpec((1, 2 * GDN_HEADS, tm), lambda bb, tt: (bb, 0, tt)),
        ],
        out_shape=[act, act, act, act,
                   jax.ShapeDtypeStruct((b, t, LANES), F32),
                   jax.ShapeDtypeStruct((b, 2 * GDN_HEADS, t), F32)],
        scratch_shapes=[pltpu.VMEM((12, 8, 256), F32)],
        compiler_params=_params(("arbitrary", "arbitrary")),
        name="gdn_inproj",
    )(x, w_main, wabh, wabl, conv_w, alog, dtb, ltri)


def _bmm(a, b):
    return lax.dot_general(a, b, (((2,), (1,)), ((0,), (0,))), preferred_element_type=F32)


def _bmm_nt(a, b):
    return lax.dot_general(a, b, (((2,), (2,)), ((0,), (0,))), preferred_element_type=F32)


def _gdn_rec_kernel(q_ref, k_ref, v_ref, z_ref, gcb_ref, gcT_ref, nw_ref, o_ref, s_ref, *, nc):
    c = GDN_CHUNK
    nh = GDN_HEADS

    @pl.when(pl.program_id(1) == 0)
    def _():
        s_ref[...] = jnp.zeros_like(s_ref)

    row = lax.broadcasted_iota(jnp.int32, (nc, c, c), 1)
    col = lax.broadcasted_iota(jnp.int32, (nc, c, c), 2)
    eye = (row == col).astype(F32)
    nw = nw_ref[...]
    gcb = gcb_ref[0]
    gct = gcT_ref[0]

    u_h, wq_h, qkkd_h, dlast_h = [], [], [], []
    for h in range(nh):
        sl = slice(h * LANES, (h + 1) * LANES)
        q = q_ref[0, :, sl].reshape(nc, c, LANES)
        k = k_ref[0, :, sl].reshape(nc, c, LANES)
        v = v_ref[0, :, sl].astype(F32).reshape(nc, c, LANES)
        gc = gcb[:, h:h + 1].reshape(nc, c, 1)
        beta = gcb[:, nh + h:nh + h + 1].reshape(nc, c, 1)
        gc_row = gct[:, h:h + 1, :]
        g_last = gc[:, c - 1:c, :]
        decay = jnp.exp(jnp.where(row >= col, gc - gc_row, NEG))
        eg = jnp.exp(gc)
        kf = k.astype(F32)
        kb = kf * beta
        aq = _bmm_nt(jnp.concatenate([kb.astype(BF16), q], axis=1), k)
        a = jnp.where(row > col, aq[:, :c] * decay, 0.0)
        qk = aq[:, c:] * decay
        qpow = -a
        usum = eye + qpow
        qb = qpow.astype(BF16)
        qpow = _bmm(qb, qb)
        for lvl in range(5):
            qb = qpow.astype(BF16)
            if lvl < 4:
                prod = _bmm(jnp.concatenate([usum.astype(BF16), qb], axis=1), qb)
                usum = usum + prod[:, :c]
                qpow = prod[:, c:]
            else:
                usum = usum + _bmm(usum.astype(BF16), qb)
        rhs = jnp.concatenate([v * beta, kb * eg], axis=2).astype(BF16)
        uw = _bmm(usum.astype(BF16), rhs)
        u_h.append(uw[:, :, :LANES])
        wq_h.append(jnp.concatenate([uw[:, :, LANES:], q.astype(F32) * eg], axis=1).astype(BF16))
        k_dec = kf * jnp.exp(g_last - gc)
        k_dec_t = jnp.swapaxes(k_dec, 1, 2)
        qkkd_h.append(jnp.concatenate([qk, k_dec_t], axis=1).astype(BF16))
        dlast_h.append(jnp.exp(g_last))

    for i in range(nc):
        s_old = s_ref[...]
        lhs = jnp.stack([wq_h[h][i] for h in range(nh)])
        r = _bmm(lhs, s_old.astype(BF16))
        u = jnp.stack([u_h[h][i] for h in range(nh)])
        v_new = (u - r[:, :c]).astype(BF16)
        r2 = _bmm(jnp.stack([qkkd_h[h][i] for h in range(nh)]), v_new)
        o = r[:, c:] + r2[:, :c]
        dl = jnp.stack([dlast_h[h][i] for h in range(nh)])
        s_ref[...] = s_old * dl + r2[:, c:]
        ms = jnp.mean(o * o, axis=-1, keepdims=True)
        on = o * lax.rsqrt(ms + NORM_EPS) * nw
        for h in range(nh):
            z = z_ref[0, i * c:(i + 1) * c, h * LANES:(h + 1) * LANES].astype(F32)
            o_ref[0, i * c:(i + 1) * c, h * LANES:(h + 1) * LANES] = (on[h] * _silu(z)).astype(BF16)


def gdn_rec(q, k, v, z, gcb, gcbT, norm_w, nc=16):
    b, t, hk = q.shape
    n_chunks = t // GDN_CHUNK
    tb = nc * GDN_CHUNK
    gct4 = gcbT.reshape(b, 2 * GDN_HEADS, n_chunks, GDN_CHUNK).transpose(0, 2, 1, 3)
    spec = pl.BlockSpec((1, tb, hk), lambda bb, tt: (bb, tt, 0))
    return pl.pallas_call(
        functools.partial(_gdn_rec_kernel, nc=nc),
        grid=(b, t // tb),
        in_specs=[spec, spec, spec, spec,
                  pl.BlockSpec((1, tb, LANES), lambda bb, tt: (bb, tt, 0)),
                  pl.BlockSpec((1, nc, 2 * GDN_HEADS, GDN_CHUNK), lambda bb, tt: (bb, tt, 0, 0)),
                  pl.BlockSpec((1, LANES), lambda bb, tt: (0, 0))],
        out_specs=spec,
        out_shape=jax.ShapeDtypeStruct((b, t, hk), BF16),
        scratch_shapes=[pltpu.VMEM((GDN_HEADS, GDN_HEAD_DIM, GDN_HEAD_DIM), F32)],
        compiler_params=_params(("arbitrary", "arbitrary")),
        name="gdn_rec",
    )(q, k, v, z, gcb, gct4, norm_w.reshape(1, LANES).astype(F32))


GRP_LANE0 = MOE_EXPERTS


def _route_t(lt):
    tm = lt.shape[1]
    far = float(LANES)
    row8 = lax.broadcasted_iota(jnp.int32, (8, tm), 0)
    row8f = row8.astype(F32)
    is_grp = row8 < MOE_GROUPS
    lg = jnp.where(is_grp, lt[GRP_LANE0:GRP_LANE0 + 8, :], NEG)
    eg = jnp.exp(lg - jnp.max(lg, axis=0, keepdims=True))
    pg = eg / jnp.sum(eg, axis=0, keepdims=True)
    gp = jnp.max(pg, axis=0, keepdims=True)
    gidx = jnp.min(jnp.where(is_grp & (pg == gp), row8f, far), axis=0, keepdims=True)
    rowf = lax.broadcasted_iota(jnp.int32, (MOE_EXPERTS, tm), 0).astype(F32)
    in_grp = jnp.floor(rowf * (1.0 / MOE_EPG)) == gidx
    le = jnp.where(in_grp, lt[:MOE_EXPERTS, :], NEG)
    ee = jnp.exp(le - jnp.max(le, axis=0, keepdims=True))
    pe = ee / jnp.sum(ee, axis=0, keepdims=True)
    p1 = jnp.max(jnp.where(in_grp, pe, -1.0), axis=0, keepdims=True)
    i1 = jnp.min(jnp.where(in_grp & (pe == p1), rowf, far), axis=0, keepdims=True)
    rest = in_grp & (rowf != i1)
    p2 = jnp.max(jnp.where(rest, pe, -1.0), axis=0, keepdims=True)
    i2 = jnp.min(jnp.where(rest & (pe == p2), rowf, far), axis=0, keepdims=True)
    scale = gp / (p1 + p2)
    return jnp.where(rowf == i1, p1 * scale, jnp.where(rowf == i2, p2 * scale, 0.0)), gidx


MOE_SUB = 128


def _mix_moe_ln_kernel(o_ref, wo_ref, res_ref, mg_ref, mb_ref, wrh_ref, wrl_ref, br_ref, utri_ref, wgu_ref, wd_ref,
                       g_ref, b_ref, out_ref, h_ref, xs_ref, gs_ref, ys_ref, *, tm, slots):
    hid_w = MOE_EPG * MOE_HIDDEN
    shift = MOE_SUB.bit_length() - 1

    h = _layer_norm(DEEPNORM_ALPHA * res_ref[...] + _dot(o_ref[...], wo_ref[...]), mg_ref[...], mb_ref[...])
    h_ref[...] = h
    xb, xlo = _split2(h)
    wh = wrh_ref[...]
    lt = _dot_nt(wh, xb) + (_dot_nt(wh, xlo) + _dot_nt(wrl_ref[...], xb)) + br_ref[...]
    gate, gidx = _route_t(lt)
    row8 = lax.broadcasted_iota(jnp.int32, (8, tm), 0)
    onehot = jnp.where((row8.astype(F32) == gidx) & (row8 < MOE_GROUPS), 1.0, 0.0)
    pos = _dot(onehot.astype(BF16), utri_ref[...])
    cnt = jnp.sum(onehot, axis=1, keepdims=True).astype(jnp.int32)
    offs, n_subs, off = [], [], 0
    for g in range(MOE_GROUPS):
        n_sub = lax.shift_right_logical(cnt[g, 0] + (MOE_SUB - 1), shift)
        offs.append(off)
        n_subs.append(n_sub)
        off = off + lax.shift_left(n_sub, shift)
    off_col = jnp.zeros((8, 1), jnp.int32)
    for g in range(1, MOE_GROUPS):
        off_col = jnp.where(row8[:, 0:1] == g, offs[g], off_col)
    dest_row = jnp.sum(onehot * (pos + off_col.astype(F32)), axis=0, keepdims=True)
    slot = lax.broadcasted_iota(jnp.int32, (slots, tm), 0).astype(F32)
    perm = jnp.where(slot == dest_row, 1.0, 0.0).astype(BF16)
    xs_ref[...] = _dot(perm, xb).astype(BF16)
    hi = jnp.floor(dest_row * (1.0 / 32.0))
    extra = jnp.where(row8 == 0, hi, jnp.where(row8 == 1, dest_row - 32.0 * hi, 0.0))
    per_tok = jnp.concatenate([gate, extra, jnp.zeros((LANES - MOE_EXPERTS - 8, tm), F32)], axis=0)
    gs_ref[...] = _dot_nt(per_tok.astype(BF16), perm).T
    tok = per_tok.T
    dest = 32.0 * tok[:, MOE_EXPERTS:MOE_EXPERTS + 1] + tok[:, MOE_EXPERTS + 1:MOE_EXPERTS + 2]
    ys_ref[...] = jnp.zeros_like(ys_ref)

    for g in range(MOE_GROUPS):
        def sub_tile(j, carry, g=g):
            r0 = pl.multiple_of(offs[g] + j * MOE_SUB, MOE_SUB)
            xsub = xs_ref[pl.ds(r0, MOE_SUB), :]
            gsub = gs_ref[pl.ds(r0, MOE_SUB), :]
            hids = []
            for e in range(MOE_EPG):
                ge = gsub[:, g * MOE_EPG + e:g * MOE_EPG + e + 1]
                hg = _dot(xsub, wgu_ref[g, :, e * MOE_HIDDEN:(e + 1) * MOE_HIDDEN])
                hu = _dot(xsub, wgu_ref[g, :, hid_w + e * MOE_HIDDEN:hid_w + (e + 1) * MOE_HIDDEN])
                hids.append((_silu(hg) * hu * ge).astype(BF16))
            ys_ref[pl.ds(r0, MOE_SUB), :] = _dot(jnp.concatenate(hids, axis=1), wd_ref[g]).astype(BF16)
            return carry

        lax.fori_loop(0, n_subs[g], sub_tile, 0)

    slot_l = lax.broadcasted_iota(jnp.int32, (tm, slots), 1).astype(F32)
    unperm = jnp.where(slot_l == dest, 1.0, 0.0).astype(BF16)
    y = _dot(unperm, ys_ref[...])
    out_ref[...] = _layer_norm(DEEPNORM_ALPHA * h_ref[...] + y, g_ref[...], b_ref[...])


def mix_moe_ln(o, w_o, res, mix_g, mix_b, w_grp, b_grp, w_rt, b_rt, w_gate, w_up, w_down, ln_g, ln_b, tm=512):
    n, d = res.shape
    kdim = o.shape[1]
    slots = MOE_SUB * ((tm + MOE_GROUPS * (MOE_SUB - 1)) // MOE_SUB)
    wr = jnp.zeros((LANES, d), F32).at[:MOE_EXPERTS].set(w_rt.T).at[GRP_LANE0:GRP_LANE0 + MOE_GROUPS].set(w_grp.T)
    br = jnp.zeros((LANES, 1), F32).at[:MOE_EXPERTS, 0].set(b_rt).at[GRP_LANE0:GRP_LANE0 + MOE_GROUPS, 0].set(b_grp)
    wrh = wr.astype(BF16)
    wrl = (wr - wrh.astype(F32)).astype(BF16)
    hid_w = MOE_EPG * MOE_HIDDEN
    by_group = lambda w: w.reshape(MOE_GROUPS, MOE_EPG, d, MOE_HIDDEN).transpose(0, 2, 1, 3).reshape(MOE_GROUPS, d, hid_w)
    wgu = jnp.concatenate([by_group(w_gate), by_group(w_up)], axis=-1).astype(BF16)
    wd = w_down.reshape(MOE_GROUPS, hid_w, d).astype(BF16)
    r = jnp.arange(tm)
    utri = (r[:, None] < r[None, :]).astype(BF16)
    row = lambda i: (i, 0)
    once = pl.Buffered(1)
    const2 = lambda shape: pl.BlockSpec(shape, lambda i: (0, 0), pipeline_mode=once)
    const3 = lambda shape: pl.BlockSpec(shape, lambda i: (0, 0, 0), pipeline_mode=once)
    return pl.pallas_call(
        functools.partial(_mix_moe_ln_kernel, tm=tm, slots=slots),
        grid=(n // tm,),
        in_specs=[pl.BlockSpec((tm, kdim), row), const2((kdim, d)), pl.BlockSpec((tm, d), row),
                  const2((1, d)), const2((1, d)),
                  const2((LANES, d)), const2((LANES, d)), const2((LANES, 1)),
                  const2((tm, tm)),
                  const3((MOE_GROUPS, d, 2 * hid_w)), const3((MOE_GROUPS, hid_w, d)),
                  const2((1, d)), const2((1, d))],
        out_specs=pl.BlockSpec((tm, d), row),
        out_shape=jax.ShapeDtypeStruct((n, d), F32),
        scratch_shapes=[pltpu.VMEM((tm, d), F32),
                        pltpu.VMEM((slots, d), BF16), pltpu.VMEM((slots, LANES), F32), pltpu.VMEM((slots, d), BF16)],
        compiler_params=_params(("arbitrary",)),
        name="mix_moe_ln",
    )(o, w_o.astype(BF16), res, mix_g.reshape(1, d), mix_b.reshape(1, d), wrh, wrl, br, utri, wgu, wd,
      ln_g.reshape(1, d), ln_b.reshape(1, d))


KV_W = NSA_GROUPS * NSA_HEAD_DIM
GATE_ROWS = 16


def _nsa_proj_kernel(h_ref, wq_ref, wg_ref, wc_ref, wks_ref, wkw_ref, wvs_ref, wvw_ref,
                     qT_ref, gT_ref, xk_ref, xv_ref, ksa_ref, kwa_ref, vsT_ref, vwT_ref, cmp_scr, *, tm):
    t0 = pl.program_id(1) * tm
    hb = h_ref[0].astype(BF16)
    qT_ref[0] = _dot_nt(wq_ref[...], hb).astype(BF16)
    gT_ref[0] = jax.nn.sigmoid(_dot_nt(wg_ref[...], hb))
    craw = _dot(hb, wc_ref[...])
    n_cb = 2 * KV_W // LANES
    for cb in range(n_cb):
        cmp_scr[cb] = craw[:, cb * LANES:(cb + 1) * LANES]
    nrow = tm // CMP_STRIDE
    lane_r = lax.broadcasted_iota(jnp.int32, (nrow, LANES), 1)
    for j in range(CMP_STRIDE // 2):
        for cb in range(n_cb):
            a = cmp_scr[cb, pl.ds(2 * j, nrow, stride=CMP_STRIDE), :]
            bm = cmp_scr[cb, pl.ds(2 * j + 1, nrow, stride=CMP_STRIDE), :]
            pieces = (jnp.where(lane_r < NSA_HEAD_DIM, a, pltpu.roll(bm, NSA_HEAD_DIM, 1)),
                      jnp.where(lane_r < NSA_HEAD_DIM, pltpu.roll(a, NSA_HEAD_DIM, 1), bm))
            for k, val in enumerate(pieces):
                pg = 2 * cb + k
                ref = xk_ref if pg < NSA_GROUPS else xv_ref
                ref[0, pg % NSA_GROUPS, :, j * LANES:(j + 1) * LANES] = val.astype(BF16)
    vsT_ref[0] = _dot_nt(wvs_ref[...], hb).astype(BF16)
    vwT_ref[0] = _dot_nt(wvw_ref[...], hb).astype(BF16)
    ks = _dot(hb, wks_ref[...])
    kw = _dot(hb, wkw_ref[...])
    lane = lax.broadcasted_iota(jnp.int32, (tm, LANES), 1)
    blk = (t0 + lax.broadcasted_iota(jnp.int32, (tm, LANES), 0)) // SEL_BLOCK
    onehot = jnp.where(lane - NSA_HEAD_DIM == blk, 1.0, 0.0)
    for g in range(NSA_GROUPS):
        ksg = ks[:, g * LANES:(g + 1) * LANES]
        ksa_ref[0, g] = jnp.where(lane < NSA_HEAD_DIM, ksg, onehot).astype(BF16)
        kwa_ref[0, g] = kw[:, g * LANES:(g + 1) * LANES].astype(BF16)


def _pad_group_cols(w):
    d = w.shape[0]
    w4 = w.reshape(d, NSA_GROUPS, NSA_HEAD_DIM)
    return jnp.concatenate([w4, jnp.zeros_like(w4)], axis=-1).reshape(d, NSA_GROUPS * LANES)


def nsa_proj(h, w_kv, w_in, tm=512):
    b, t, d = h.shape
    scale = NSA_HEAD_DIM ** -0.5 * LOG2E
    wq = (w_in[:, :NSA_WIDTH] * scale).T.astype(BF16)
    wgate = w_in[:, NSA_WIDTH:].reshape(d, NSA_GROUPS, NSA_HPG, N_BRANCHES)
    wgate = wgate.transpose(0, 1, 3, 2).reshape(d, NSA_GROUPS, N_BRANCHES * NSA_HPG)
    wgate = jnp.concatenate([wgate, jnp.zeros((d, NSA_GROUPS, GATE_ROWS - N_BRANCHES * NSA_HPG), F32)], axis=-1)
    wg = wgate.reshape(d, NSA_GROUPS * GATE_ROWS).T.astype(BF16)
    part = lambda p: w_kv[:, p * KV_W:(p + 1) * KV_W]
    wc = w_kv[:, :2 * KV_W].astype(BF16)
    wks = _pad_group_cols(part(2)).astype(BF16)
    wvs = part(3).T.astype(BF16)
    wkw = _pad_group_cols(part(4)).astype(BF16)
    wvw = part(5).T.astype(BF16)
    const = lambda bb, tt: (0, 0)
    tok = lambda bb, tt: (bb, tt, 0)
    tr = lambda bb, tt: (bb, 0, tt)
    g4 = lambda bb, tt: (bb, 0, tt, 0)
    ng = NSA_GROUPS * GATE_ROWS
    wide = CMP_STRIDE * NSA_HEAD_DIM
    return pl.pallas_call(
        functools.partial(_nsa_proj_kernel, tm=tm),
        grid=(b, t // tm),
        in_specs=[pl.BlockSpec((1, tm, d), tok),
                  pl.BlockSpec((NSA_WIDTH, d), const), pl.BlockSpec((ng, d), const),
                  pl.BlockSpec((d, 2 * KV_W), const),
                  pl.BlockSpec((d, NSA_GROUPS * LANES), const), pl.BlockSpec((d, NSA_GROUPS * LANES), const),
                  pl.BlockSpec((KV_W, d), const), pl.BlockSpec((KV_W, d), const)],
        out_specs=[pl.BlockSpec((1, NSA_WIDTH, tm), tr), pl.BlockSpec((1, ng, tm), tr),
                   pl.BlockSpec((1, NSA_GROUPS, tm // CMP_STRIDE, wide), g4),
                   pl.BlockSpec((1, NSA_GROUPS, tm // CMP_STRIDE, wide), g4),
                   pl.BlockSpec((1, NSA_GROUPS, tm, LANES), g4), pl.BlockSpec((1, NSA_GROUPS, tm, LANES), g4),
                   pl.BlockSpec((1, KV_W, tm), tr), pl.BlockSpec((1, KV_W, tm), tr)],
        out_shape=[jax.ShapeDtypeStruct((b, NSA_WIDTH, t), BF16), jax.ShapeDtypeStruct((b, ng, t), F32),
                   jax.ShapeDtypeStruct((b, NSA_GROUPS, t // CMP_STRIDE, wide), BF16),
                   jax.ShapeDtypeStruct((b, NSA_GROUPS, t // CMP_STRIDE, wide), BF16),
                   jax.ShapeDtypeStruct((b, NSA_GROUPS, t, LANES), BF16),
                   jax.ShapeDtypeStruct((b, NSA_GROUPS, t, LANES), BF16),
                   jax.ShapeDtypeStruct((b, KV_W, t), BF16), jax.ShapeDtypeStruct((b, KV_W, t), BF16)],
        scratch_shapes=[pltpu.VMEM((2 * KV_W // LANES, tm, LANES), F32)],
        compiler_params=_params(("arbitrary", "arbitrary")),
        name="nsa_proj",
    )(h, wq, wg, wc, wks, wkw, wvs, wvw)


N_CMP_PAD = 128


def _nsa_compress_kernel(xk_ref, xv_ref, pk_ref, pv_ref, w1k_ref, w1v_ref, w2k_ref, w2vT_ref, kc_ref, vcT_ref):
    half = CMP_STRIDE * NSA_HEAD_DIM
    pbk = _dot(pk_ref[...], w1k_ref[...])[0:1, :]
    pbv = _dot(pv_ref[...], w1v_ref[...])[0:1, :]
    for g in range(NSA_GROUPS):
        xk = xk_ref[0, g]
        hk = _dot(xk, w1k_ref[:half, :]) + pltpu.roll(_dot(xk, w1k_ref[half:, :]), N_CMP_PAD - 1, 0) + pbk
        kc_ref[0, g] = _dot(_silu(hk).astype(BF16), w2k_ref[...]).astype(BF16)
        xv = xv_ref[0, g]
        hv = _dot(xv, w1v_ref[:half, :]) + pltpu.roll(_dot(xv, w1v_ref[half:, :]), N_CMP_PAD - 1, 0) + pbv
        vcT_ref[0, g] = _dot_nt(w2vT_ref[...], _silu(hv).astype(BF16)).astype(BF16)


def nsa_compress(xk, xv, k_pos, k_w1, k_w2, v_pos, v_w1, v_w2):
    b, _, nrow, wide = xk.shape
    assert nrow == N_CMP_PAD
    pad_pos = lambda p: jnp.zeros((8, CMP_BLOCK * NSA_HEAD_DIM), F32).at[0].set(p.reshape(-1)).astype(BF16)
    spec_x = pl.BlockSpec((1, NSA_GROUPS, nrow, wide), lambda bb: (bb, 0, 0, 0))
    c2 = lambda bb: (0, 0)
    return pl.pallas_call(
        _nsa_compress_kernel,
        grid=(b,),
        in_specs=[spec_x, spec_x,
                  pl.BlockSpec((8, 2 * wide), c2), pl.BlockSpec((8, 2 * wide), c2),
                  pl.BlockSpec((2 * wide, CMP_HIDDEN), c2), pl.BlockSpec((2 * wide, CMP_HIDDEN), c2),
                  pl.BlockSpec((CMP_HIDDEN, NSA_HEAD_DIM), c2), pl.BlockSpec((NSA_HEAD_DIM, CMP_HIDDEN), c2)],
        out_specs=[pl.BlockSpec((1, NSA_GROUPS, N_CMP_PAD, NSA_HEAD_DIM), lambda bb: (bb, 0, 0, 0)),
                   pl.BlockSpec((1, NSA_GROUPS, NSA_HEAD_DIM, N_CMP_PAD), lambda bb: (bb, 0, 0, 0))],
        out_shape=[jax.ShapeDtypeStruct((b, NSA_GROUPS, N_CMP_PAD, NSA_HEAD_DIM), BF16),
                   jax.ShapeDtypeStruct((b, NSA_GROUPS, NSA_HEAD_DIM, N_CMP_PAD), BF16)],
        compiler_params=_params(("arbitrary",)),
        name="nsa_compress",
    )(xk, xv, pad_pos(k_pos), pad_pos(v_pos), k_w1.astype(BF16), v_w1.astype(BF16),
      k_w2.astype(BF16), v_w2.T.astype(BF16))


SEL_CHUNK = 256
WIN_CHUNK = 128
TB_ROWS = WINDOW + SEL_CHUNK
TBC_ROWS = 256
TBC_OFF = 120
N_SEL_BLOCKS = 32
QW = NSA_HPG * Q_BLOCK


def _t5_bucket(dist):
    n = jnp.maximum(dist, 0)
    max_exact = REL_BUCKETS // 2
    nf = jnp.maximum(n, 1).astype(F32)
    large = max_exact + (jnp.log(nf / max_exact) / math.log(REL_MAX_DIST / max_exact)
                         * (REL_BUCKETS - max_exact)).astype(jnp.int32)
    large = jnp.minimum(large, REL_BUCKETS - 1)
    return jnp.where(n < max_exact, n, large)


def _bias_tables(rel_bias):
    rel = rel_bias.astype(F32)
    far = rel[REL_BUCKETS - 1]
    ql = jnp.arange(Q_BLOCK)
    heads = jnp.arange(NSA_HEADS).reshape(NSA_GROUPS, NSA_HPG)

    def table(dist, valid):
        bucket = _t5_bucket(dist)[None, :, None, :]
        relc = ((rel - far) * LOG2E)[:, heads][:, :, None, :, None]
        bias = jnp.zeros((NSA_GROUPS, dist.shape[0], NSA_HPG, Q_BLOCK), F32)
        for b in range(REL_BUCKETS):
            bias = jnp.where(bucket == b, relc[b], bias)
        bias = jnp.where(valid[None, :, None, :], bias, NEG)
        return bias.reshape(NSA_GROUPS, dist.shape[0], QW)

    d = ql[None, :] - (jnp.arange(TB_ROWS)[:, None] - WINDOW)
    tb = table(d, (d >= 0) & (d < WINDOW))
    m = jnp.arange(TBC_ROWS)[:, None] - TBC_OFF
    dc = ql[None, :] - CMP_STRIDE * m - (CMP_BLOCK - 1)
    tbc = table(dc, dc >= 0)
    return tb, tbc


def _nsa_attn_kernel(qT_ref, gT_ref, kc_ref, vcT_ref, ksa_ref, vsT_ref, kwa_ref, vwT_ref, tb_ref, tbc_ref, ovl_ref,
                     o_ref, m_ref, l_ref, acc_ref, ot_ref, oT_ref, sc_ref, selm_ref):
    c = pl.program_id(1)
    ng, hpg, hd = NSA_GROUPS, NSA_HPG, NSA_HEAD_DIM
    nsel = N_SEL_BLOCKS
    n_top = min(TOP_N, nsel)
    n_win = WINDOW // WIN_CHUNK

    q4 = jnp.stack([jnp.concatenate([qT_ref[0, (g * hpg + hh) * hd:(g * hpg + hh + 1) * hd, :]
                                     for hh in range(hpg)], axis=1) for g in range(ng)])
    gates = gT_ref[0]

    def gate3(br):
        return jnp.stack([jnp.concatenate(
            [gates[g * GATE_ROWS + br * hpg + hh:g * GATE_ROWS + br * hpg + hh + 1, :] for hh in range(hpg)], axis=1)
            for g in range(ng)])

    def fresh():
        return (jnp.full((ng, 1, QW), NEG, F32), jnp.zeros((ng, 1, QW), F32), jnp.zeros((ng, hd, QW), F32))

    def load_state():
        return m_ref[...], l_ref[...], acc_ref[...]

    def store_state(state):
        m_ref[...], l_ref[...], acc_ref[...] = state

    def upd(state, s, vT):
        m_old, l_old, acc_old = state
        m_new = jnp.maximum(m_old, jnp.max(s, axis=1, keepdims=True))
        alpha = jnp.exp2(m_old - m_new)
        p = jnp.exp2(s - m_new)
        return (m_new, l_old * alpha + jnp.sum(p, axis=1, keepdims=True),
                acc_old * alpha + _bmm(vT, p.astype(BF16)))

    def result(state, br):
        _, l_fin, acc_fin = state
        return gate3(br) * (acc_fin * (1.0 / l_fin))

    start = pl.multiple_of(TBC_OFF - 8 * c, 8)
    tbc = tbc_ref[:, pl.ds(start, N_CMP_PAD), :]
    nrow = lax.broadcasted_iota(jnp.int32, tbc.shape, 1)
    tbc = jnp.where(nrow < N_CMP_PAD - 1, tbc, NEG)
    s = _bmm(kc_ref[0], q4) + tbc
    valid = tbc > 0.5 * NEG
    e = jnp.where(valid, jnp.exp2(s - jnp.max(s, axis=1, keepdims=True)), 0.0)
    l = jnp.sum(e, axis=1, keepdims=True)
    p = e * (1.0 / jnp.where(l > 0.0, l, 1.0))
    o_cmp = gate3(0) * _bmm(vcT_ref[0], p.astype(BF16))

    psum = p[:, :, 0:Q_BLOCK]
    for hh in range(1, hpg):
        psum = psum + p[:, :, hh * Q_BLOCK:(hh + 1) * Q_BLOCK]
    psum = jnp.concatenate([psum[g] for g in range(ng)], axis=1)
    ph, plo = _split2(psum)
    ovl = ovl_ref[...]
    imp = _dot(ovl, ph) + _dot(ovl, plo)
    jidx = lax.broadcasted_iota(jnp.int32, imp.shape, 0)
    qlane = lax.broadcasted_iota(jnp.int32, imp.shape, 1) & (Q_BLOCK - 1)
    jq = 2 * c + (qlane >= SEL_BLOCK).astype(jnp.int32)
    forced = (jidx == 0) | (jidx == jq) | (jidx == jq - 1)
    imp = jnp.where(forced, -NEG, jnp.where(jidx <= jq, imp, NEG))
    selm_ref[...] = jnp.zeros_like(selm_ref)

    @pl.when(2 * c + 2 > n_top)
    def _():
        sub8 = lax.broadcasted_iota(jnp.int32, (8, imp.shape[1]), 0)
        rows = [imp[8 * v:8 * v + 8, :] for v in range(nsel // 8)]
        cnts = [jnp.zeros_like(r) for r in rows]
        for i in range(nsel):
            r = imp[i:i + 1, :]
            for v in range(nsel // 8):
                ge = jnp.where(r >= rows[v], 1.0, 0.0)
                gt = jnp.where(r > rows[v], 1.0, 0.0)
                if i < 8 * v:
                    ahead = ge
                elif i >= 8 * v + 8:
                    ahead = gt
                else:
                    ahead = jnp.where(sub8 > i - 8 * v, ge, gt)
                cnts[v] = cnts[v] + ahead
        cnt = jnp.concatenate(cnts, axis=0)
        selm_ref[...] = jnp.where(cnt < float(n_top), 0.0, NEG)

    selm = selm_ref[...].astype(BF16)
    pad = jnp.zeros((LANES - hd - nsel, QW), BF16)
    q_aug = jnp.stack([jnp.concatenate(
        [q4[g], jnp.concatenate([selm[:, g * Q_BLOCK:(g + 1) * Q_BLOCK]] * hpg, axis=1), pad], axis=0)
        for g in range(ng)])

    def sel_scores(k0):
        return _bmm(ksa_ref[0, :, pl.ds(k0, SEL_CHUNK), :], q_aug)

    def sel_values(k0):
        return vsT_ref[0, :, pl.ds(k0, SEL_CHUNK)].reshape(ng, hd, SEL_CHUNK)

    def win(first, n_chunks, tab_chunk):
        width = n_chunks * WIN_CHUNK
        k0 = pl.multiple_of((c - n_win + first) * WIN_CHUNK, WIN_CHUNK)
        s_w = _bmm(kwa_ref[0, :, pl.ds(k0, width), :], q_aug)
        if tab_chunk is not None:
            lo = (tab_chunk - first) * WIN_CHUNK
            part = s_w[:, lo:lo + WIN_CHUNK] + tb_ref[:, tab_chunk * WIN_CHUNK:(tab_chunk + 1) * WIN_CHUNK, :]
            pieces = ([s_w[:, :lo]] if lo else []) + [part] + ([s_w[:, lo + WIN_CHUNK:]] if lo + WIN_CHUNK < width else [])
            s_w = jnp.concatenate(pieces, axis=1) if len(pieces) > 1 else part
        return s_w, vwT_ref[0, :, pl.ds(k0, width)].reshape(ng, hd, width)

    def raw_scores(i):
        return sel_scores(pl.multiple_of(i * SEL_CHUNK, SEL_CHUNK))

    def values(i):
        return sel_values(pl.multiple_of(i * SEL_CHUNK, SEL_CHUNK))

    def sel_table(i):
        tstart = pl.multiple_of(WINDOW - (c * Q_BLOCK - i * SEL_CHUNK), Q_BLOCK)
        return tb_ref[:, pl.ds(tstart, SEL_CHUNK), :]

    store_state(fresh())
    n_far = jnp.maximum(c // 2 - 1, 0)

    @pl.when(c >= 2)
    def _():
        sc_ref[0] = raw_scores(0)

    def far_pair(j, carry):
        i = 2 * j
        sc_ref[1] = raw_scores(i + 1)
        store_state(upd(load_state(), sc_ref[0], values(i)))
        sc_ref[0] = raw_scores(i + 2)
        store_state(upd(load_state(), sc_ref[1], values(i + 1)))
        return carry

    lax.fori_loop(0, n_far // 2, far_pair, 0)

    @pl.when(n_far % 2 == 1)
    def _():
        sc_ref[1] = raw_scores(n_far)
        store_state(upd(load_state(), sc_ref[0], values(n_far - 1)))

    @pl.when(c >= n_win)
    def _():
        s_a = sc_ref[n_far % 2] + sel_table(c // 2 - 1)
        s_b = raw_scores(c // 2)
        w_a, wv_a = win(0, 2, 0)
        st_s = upd(load_state(), s_a, values(c // 2 - 1))
        w_b, wv_b = win(2, 2, 3)
        st_w = upd(fresh(), w_a, wv_a)
        st_s = upd(st_s, s_b + sel_table(c // 2), values(c // 2))
        w_c, wv_c = win(4, 1, 4)
        st_w = upd(st_w, w_b, wv_b)
        st_w = upd(st_w, w_c, wv_c)
        ot_ref[...] = o_cmp + result(st_s, 1) + result(st_w, 2)

    @pl.when(c < n_win)
    def _():
        @pl.when(c >= 2)
        def _():
            store_state(upd(load_state(), sc_ref[0] + sel_table(c // 2 - 1), values(c // 2 - 1)))

        ot_ref[...] = o_cmp + result(upd(load_state(), raw_scores(c // 2) + sel_table(c // 2), values(c // 2)), 1)
        store_state(fresh())
        pl.when(c == 3)(lambda: store_state(upd(load_state(), *win(1, 1, None))))
        pl.when(c >= 2)(lambda: store_state(upd(load_state(), *win(2, 2, 3))))
        pl.when(c == 1)(lambda: store_state(upd(load_state(), *win(3, 1, 3))))
        ot_ref[...] += result(upd(load_state(), *win(4, 1, 4)), 2)

    for g in range(ng):
        for hh in range(hpg):
            oT_ref[hh, g * hd:(g + 1) * hd, :] = ot_ref[g, :, hh * Q_BLOCK:(hh + 1) * Q_BLOCK]
    for hh in range(hpg):
        o_ref[0, :, hh * KV_W:(hh + 1) * KV_W] = oT_ref[hh].T.astype(BF16)


def nsa_attn(qT, gT, kc, vcT, ksa, vsT, kwa, vwT, rel_bias):
    b, _, t = qT.shape
    nq = t // Q_BLOCK
    tb, tbc = _bias_tables(rel_bias)
    nc = (t - CMP_BLOCK) // CMP_STRIDE + 1
    cs = jnp.arange(N_CMP_PAD) * CMP_STRIDE
    ss = jnp.arange(N_SEL_BLOCKS) * SEL_BLOCK
    ovl = ((cs[None, :] < ss[:, None] + SEL_BLOCK) & (cs[None, :] + CMP_BLOCK - 1 >= ss[:, None])
           & (jnp.arange(N_CMP_PAD)[None, :] < nc)).astype(BF16)
    ng = NSA_GROUPS * GATE_ROWS
    per_b3 = lambda bb, cc: (bb, 0, 0)
    per_b4 = lambda bb, cc: (bb, 0, 0, 0)
    c3 = lambda bb, cc: (0, 0, 0)
    return pl.pallas_call(
        _nsa_attn_kernel,
        grid=(b, nq),
        in_specs=[pl.BlockSpec((1, NSA_WIDTH, Q_BLOCK), lambda bb, cc: (bb, 0, cc)),
                  pl.BlockSpec((1, ng, Q_BLOCK), lambda bb, cc: (bb, 0, cc)),
                  pl.BlockSpec((1, NSA_GROUPS, N_CMP_PAD, NSA_HEAD_DIM), per_b4),
                  pl.BlockSpec((1, NSA_GROUPS, NSA_HEAD_DIM, N_CMP_PAD), per_b4),
                  pl.BlockSpec((1, NSA_GROUPS, t, LANES), per_b4),
                  pl.BlockSpec((1, KV_W, t), per_b3),
                  pl.BlockSpec((1, NSA_GROUPS, t, LANES), per_b4),
                  pl.BlockSpec((1, KV_W, t), per_b3),
                  pl.BlockSpec((NSA_GROUPS, TB_ROWS, QW), c3),
                  pl.BlockSpec((NSA_GROUPS, TBC_ROWS, QW), c3),
                  pl.BlockSpec((N_SEL_BLOCKS, N_CMP_PAD), lambda bb, cc: (0, 0))],
        out_specs=pl.BlockSpec((1, Q_BLOCK, NSA_WIDTH), lambda bb, cc: (bb, cc, 0)),
        out_shape=jax.ShapeDtypeStruct((b, t, NSA_WIDTH), BF16),
        scratch_shapes=[pltpu.VMEM((NSA_GROUPS, 1, QW), F32), pltpu.VMEM((NSA_GROUPS, 1, QW), F32),
                        pltpu.VMEM((NSA_GROUPS, NSA_HEAD_DIM, QW), F32),
                        pltpu.VMEM((NSA_GROUPS, NSA_HEAD_DIM, QW), F32),
                        pltpu.VMEM((NSA_HPG, KV_W, Q_BLOCK), F32),
                        pltpu.VMEM((2, NSA_GROUPS, SEL_CHUNK, QW), F32),
                        pltpu.VMEM((N_SEL_BLOCKS, NSA_GROUPS * Q_BLOCK), F32)],
        compiler_params=_params(("arbitrary", "arbitrary")),
        name="nsa_attn",
    )(qT, gT, kc, vcT, ksa, vsT, kwa, vwT, tb, tbc, ovl)


def kernel(x, gdn_w_in, gdn_conv_w, gdn_a_log, gdn_dt_bias, gdn_norm_w, gdn_w_o, nsa_w_kv, cmp_k_pos, cmp_k_w1,
           cmp_k_w2, cmp_v_pos, cmp_v_w1, cmp_v_w2, nsa_w_in, nsa_w_o, rel_bias, ln_mix_g, ln_mix_b, ln_ffn_g,
           ln_ffn_b, moe_w_grp, moe_b_grp, moe_w_rt, moe_b_rt, moe_w_gate, moe_w_up, moe_w_down):
    b, t, d = x.shape
    n = b * t

    def mix_ffn(o, w_o, res, layer):
        return mix_moe_ln(o, w_o, res, ln_mix_g[layer], ln_mix_b[layer], moe_w_grp[layer], moe_b_grp[layer],
                          moe_w_rt[layer], moe_b_rt[layer], moe_w_gate[layer], moe_w_up[layer], moe_w_down[layer],
                          ln_ffn_g[layer], ln_ffn_b[layer])

    q, k, v, z, gcb, gcbT = gdn_inproj(x, gdn_w_in[0], gdn_conv_w[0], gdn_a_log[0], gdn_dt_bias[0])
    o = gdn_rec(q, k, v, z, gcb, gcbT, gdn_norm_w[0])
    h = mix_ffn(o.reshape(n, GDN_WIDTH), gdn_w_o[0], x.reshape(n, d), 0)

    qT, gT, xk, xv, ksa, kwa, vsT, vwT = nsa_proj(h.reshape(b, t, d), nsa_w_kv, nsa_w_in[0])
    kc, vcT = nsa_compress(xk, xv, cmp_k_pos, cmp_k_w1, cmp_k_w2, cmp_v_pos, cmp_v_w1, cmp_v_w2)
    o = nsa_attn(qT, gT, kc, vcT, ksa, vsT, kwa, vwT, rel_bias)
    w_o = nsa_w_o[0].reshape(NSA_GROUPS, NSA_HPG, NSA_HEAD_DIM, d).transpose(1, 0, 2, 3).reshape(NSA_WIDTH, d)
    h = mix_ffn(o.reshape(n, NSA_WIDTH), w_o, h, 1)
    return h.reshape(b, t, d)
```

```python
import functools
import math

import jax
import jax.numpy as jnp
from jax import lax
from jax.experimental import pallas as pl
from jax.experimental.pallas import tpu as pltpu

F32 = jnp.float32
BF16 = jnp.bfloat16

DEPTH = 2
GDN_HEADS = 8
GDN_HEAD_DIM = 128
GDN_WIDTH = GDN_HEADS * GDN_HEAD_DIM
GDN_CONV = 4
GDN_CHUNK = 64
NSA_HEADS = 16
NSA_GROUPS = 4
NSA_HPG = NSA_HEADS // NSA_GROUPS
NSA_HEAD_DIM = 64
NSA_WIDTH = NSA_HEADS * NSA_HEAD_DIM
CMP_BLOCK = 32
CMP_STRIDE = 16
CMP_HIDDEN = 2 * NSA_HEAD_DIM
SEL_BLOCK = 64
TOP_N = 16
WINDOW = 512
Q_BLOCK = 128
N_BRANCHES = 3
REL_BUCKETS = 32
REL_MAX_DIST = 128
MOE_GROUPS = 4
MOE_EPG = 4
MOE_EXPERTS = MOE_GROUPS * MOE_EPG
MOE_HIDDEN = 256
DEEPNORM_ALPHA = (2 * DEPTH) ** 0.25
LN_EPS = 1e-5
NORM_EPS = 1e-6
NEG = -1e30
LOG2E = math.log2(math.e)

LANES = 128
VMEM_LIMIT = 56 * 1024 * 1024


def _dot(a, b):
    return jnp.dot(a, b, preferred_element_type=F32)


def _dot_nt(a, b):
    return lax.dot_general(a, b, (((1,), (1,)), ((), ())), preferred_element_type=F32)


def _split2(x):
    hi = x.astype(BF16)
    lo = (x - hi.astype(F32)).astype(BF16)
    return hi, lo


def _silu(x):
    return x * jax.nn.sigmoid(x)


def _params(sem):
    return pltpu.CompilerParams(dimension_semantics=sem, vmem_limit_bytes=VMEM_LIMIT)


def _layer_norm(x, g, b):
    mu = jnp.mean(x, axis=-1, keepdims=True)
    xc = x - mu
    var = jnp.mean(xc * xc, axis=-1, keepdims=True)
    return xc * lax.rsqrt(var + LN_EPS) * g + b


def _gdn_inproj_kernel(x_ref, w_ref, wabh_ref, wabl_ref, cw_ref, alog_ref, dtb_ref, ltri_ref,
                       q_ref, k_ref, v_ref, z_ref, gcb_ref, gcbT_ref, carry_ref, *, tm):
    @pl.when(pl.program_id(1) == 0)
    def _():
        carry_ref[...] = jnp.zeros_like(carry_ref)

    x = x_ref[0]
    xb, xlo = _split2(x)

    wh = wabh_ref[...]
    ab = _dot(xb, wh) + (_dot(xlo, wh) + _dot(xb, wabl_ref[...]))
    lane = lax.broadcasted_iota(jnp.int32, ab.shape, 1)
    sp_in = ab + dtb_ref[...]
    softplus = jnp.maximum(sp_in, 0.0) + jnp.log1p(jnp.exp(-jnp.abs(sp_in)))
    g = jnp.where(lane < GDN_HEADS, -jnp.exp(alog_ref[...]) * softplus, 0.0)
    beta = jax.nn.sigmoid(ab)
    g1 = g.astype(BF16)
    r1 = g - g1.astype(F32)
    g2 = r1.astype(BF16)
    g3 = (r1 - g2.astype(F32)).astype(BF16)
    ltri = ltri_ref[...]
    gc = _dot(ltri, g1) + (_dot(ltri, g2) + _dot(ltri, g3))
    gcb = jnp.where(lane < GDN_HEADS, gc, jnp.where(lane < 2 * GDN_HEADS, beta, 0.0))
    gcb_ref[0] = gcb
    gcbT_ref[0] = gcb.T[:2 * GDN_HEADS, :]

    row8 = lax.broadcasted_iota(jnp.int32, (8, 256), 0)
    outs = (q_ref, k_ref, v_ref)
    for s in range(3):
        for cc in range(4):
            col = s * GDN_WIDTH + cc * 256
            y = _dot(xb, w_ref[:, col:col + 256])
            prev = carry_ref[s * 4 + cc]
            carry_ref[s * 4 + cc] = y[tm - 8:, :]
            cw = cw_ref[:, col:col + 256]
            acc = y * cw[3:4, :]
            for kk in range(1, GDN_CONV):
                ry = pltpu.roll(y, kk, 0)
                rp = pltpu.roll(prev, kk, 0)
                head = jnp.where(row8 < kk, rp, ry[:8, :])
                shifted = jnp.concatenate([head, ry[8:, :]], axis=0)
                acc = acc + shifted * cw[3 - kk:4 - kk, :]
            a = _silu(acc)
            if s < 2:
                halves = []
                for hh in range(2):
                    ah = a[:, hh * LANES:(hh + 1) * LANES]
                    ss = jnp.sum(ah * ah, axis=-1, keepdims=True)
                    scale = lax.rsqrt(ss + NORM_EPS)
                    if s == 0:
                        scale = scale * (GDN_HEAD_DIM ** -0.5)
                    halves.append(ah * scale)
                a = jnp.concatenate(halves, axis=1)
            outs[s][0, :, cc * 256:(cc + 1) * 256] = a.astype(BF16)
    for cc in range(4):
        col = 3 * GDN_WIDTH + cc * 256
        z_ref[0, :, cc * 256:(cc + 1) * 256] = _dot(xb, w_ref[:, col:col + 256]).astype(BF16)


def gdn_inproj(x, w_in, conv_w, a_log, dt_bias, tm=256):
    b, t, d = x.shape
    hk = GDN_WIDTH
    w_main = w_in[:, :4 * hk].astype(BF16)
    w_ab = jnp.zeros((d, LANES), F32).at[:, :2 * GDN_HEADS].set(w_in[:, 4 * hk:])
    wabh = w_ab.astype(BF16)
    wabl = (w_ab - wabh.astype(F32)).astype(BF16)
    alog = jnp.zeros((1, LANES), F32).at[0, :GDN_HEADS].set(a_log)
    dtb = jnp.zeros((1, LANES), F32).at[0, :GDN_HEADS].set(dt_bias)
    r = jnp.arange(tm)
    ltri = ((r[:, None] // GDN_CHUNK == r[None, :] // GDN_CHUNK) & (r[:, None] >= r[None, :])).astype(BF16)
    tok = lambda bb, tt: (bb, tt, 0)
    const2 = lambda bb, tt: (0, 0)
    act = jax.ShapeDtypeStruct((b, t, hk), BF16)
    return pl.pallas_call(
        functools.partial(_gdn_inproj_kernel, tm=tm),
        grid=(b, t // tm),
        in_specs=[
            pl.BlockSpec((1, tm, d), tok),
            pl.BlockSpec((d, 4 * hk), const2),
            pl.BlockSpec((d, LANES), const2),
            pl.BlockSpec((d, LANES), const2),
            pl.BlockSpec((GDN_CONV, 3 * hk), const2),
            pl.BlockSpec((1, LANES), const2),
            pl.BlockSpec((1, LANES), const2),
            pl.BlockSpec((tm, tm), const2),
        ],
        out_specs=[
            pl.BlockSpec((1, tm, hk), tok),
            pl.BlockSpec((1, tm, hk), tok),
            pl.BlockSpec((1, tm, hk), tok),
            pl.BlockSpec((1, tm, hk), tok),
            pl.BlockSpec((1, tm, LANES), tok),
            pl.BlockSpec((1, 2 * GDN_HEADS, tm), lambda bb, tt: (bb, 0, tt)),
        ],
        out_shape=[act, act, act, act,
                   jax.ShapeDtypeStruct((b, t, LANES), F32),
                   jax.ShapeDtypeStruct((b, 2 * GDN_HEADS, t), F32)],
        scratch_shapes=[pltpu.VMEM((12, 8, 256), F32)],
        compiler_params=_params(("arbitrary", "arbitrary")),
        name="gdn_inproj",
    )(x, w_main, wabh, wabl, conv_w, alog, dtb, ltri)


def _bmm(a, b):
    return lax.dot_general(a, b, (((2,), (1,)), ((0,), (0,))), preferred_element_type=F32)


def _bmm_nt(a, b):
    return lax.dot_general(a, b, (((2,), (2,)), ((0,), (0,))), preferred_element_type=F32)


def _gdn_rec_kernel(q_ref, k_ref, v_ref, z_ref, gcb_ref, gcT_ref, nw_ref, o_ref, s_ref, *, nc):
    c = GDN_CHUNK
    nh = GDN_HEADS

    @pl.when(pl.program_id(1) == 0)
    def _():
        s_ref[...] = jnp.zeros_like(s_ref)

    row = lax.broadcasted_iota(jnp.int32, (nc, c, c), 1)
    col = lax.broadcasted_iota(jnp.int32, (nc, c, c), 2)
    eye = (row == col).astype(F32)
    nw = nw_ref[...]
    gcb = gcb_ref[0]
    gct = gcT_ref[0]

    u_h, wq_h, qkkd_h, dlast_h = [], [], [], []
    for h in range(nh):
        sl = slice(h * LANES, (h + 1) * LANES)
        q = q_ref[0, :, sl].reshape(nc, c, LANES)
        k = k_ref[0, :, sl].reshape(nc, c, LANES)
        v = v_ref[0, :, sl].astype(F32).reshape(nc, c, LANES)
        gc = gcb[:, h:h + 1].reshape(nc, c, 1)
        beta = gcb[:, nh + h:nh + h + 1].reshape(nc, c, 1)
        gc_row = gct[:, h:h + 1, :]
        g_last = gc[:, c - 1:c, :]
        decay = jnp.exp(jnp.where(row >= col, gc - gc_row, NEG))
        eg = jnp.exp(gc)
        kf = k.astype(F32)
        kb = kf * beta
        aq = _bmm_nt(jnp.concatenate([kb.astype(BF16), q], axis=1), k)
        a = jnp.where(row > col, aq[:, :c] * decay, 0.0)
        qk = aq[:, c:] * decay
        qpow = -a
        usum = eye + qpow
        qb = qpow.astype(BF16)
        qpow = _bmm(qb, qb)
        for lvl in range(5):
            qb = qpow.astype(BF16)
            if lvl < 4:
                prod = _bmm(jnp.concatenate([usum.astype(BF16), qb], axis=1), qb)
                usum = usum + prod[:, :c]
                qpow = prod[:, c:]
            else:
                usum = usum + _bmm(usum.astype(BF16), qb)
        rhs = jnp.concatenate([v * beta, kb * eg], axis=2).astype(BF16)
        uw = _bmm(usum.astype(BF16), rhs)
        u_h.append(uw[:, :, :LANES])
        wq_h.append(jnp.concatenate([uw[:, :, LANES:], q.astype(F32) * eg], axis=1).astype(BF16))
        k_dec = kf * jnp.exp(g_last - gc)
        k_dec_t = jnp.swapaxes(k_dec, 1, 2)
        qkkd_h.append(jnp.concatenate([qk, k_dec_t], axis=1).astype(BF16))
        dlast_h.append(jnp.exp(g_last))

    for i in range(nc):
        s_old = s_ref[...]
        lhs = jnp.stack([wq_h[h][i] for h in range(nh)])
        r = _bmm(lhs, s_old.astype(BF16))
        u = jnp.stack([u_h[h][i] for h in range(nh)])
        v_new = (u - r[:, :c]).astype(BF16)
        r2 = _bmm(jnp.stack([qkkd_h[h][i] for h in range(nh)]), v_new)
        o = r[:, c:] + r2[:, :c]
        dl = jnp.stack([dlast_h[h][i] for h in range(nh)])
        s_ref[...] = s_old * dl + r2[:, c:]
        ms = jnp.mean(o * o, axis=-1, keepdims=True)
        on = o * lax.rsqrt(ms + NORM_EPS) * nw
        for h in range(nh):
            z = z_ref[0, i * c:(i + 1) * c, h * LANES:(h + 1) * LANES].astype(F32)
            o_ref[0, i * c:(i + 1) * c, h * LANES:(h + 1) * LANES] = (on[h] * _silu(z)).astype(BF16)


def gdn_rec(q, k, v, z, gcb, gcbT, norm_w, nc=16):
    b, t, hk = q.shape
    n_chunks = t // GDN_CHUNK
    tb = nc * GDN_CHUNK
    gct4 = gcbT.reshape(b, 2 * GDN_HEADS, n_chunks, GDN_CHUNK).transpose(0, 2, 1, 3)
    spec = pl.BlockSpec((1, tb, hk), lambda bb, tt: (bb, tt, 0))
    return pl.pallas_call(
        functools.partial(_gdn_rec_kernel, nc=nc),
        grid=(b, t // tb),
        in_specs=[spec, spec, spec, spec,
                  pl.BlockSpec((1, tb, LANES), lambda bb, tt: (bb, tt, 0)),
                  pl.BlockSpec((1, nc, 2 * GDN_HEADS, GDN_CHUNK), lambda bb, tt: (bb, tt, 0, 0)),
                  pl.BlockSpec((1, LANES), lambda bb, tt: (0, 0))],
        out_specs=spec,
        out_shape=jax.ShapeDtypeStruct((b, t, hk), BF16),
        scratch_shapes=[pltpu.VMEM((GDN_HEADS, GDN_HEAD_DIM, GDN_HEAD_DIM), F32)],
        compiler_params=_params(("arbitrary", "arbitrary")),
        name="gdn_rec",
    )(q, k, v, z, gcb, gct4, norm_w.reshape(1, LANES).astype(F32))


GRP_LANE0 = MOE_EXPERTS


def _route_t(lt):
    tm = lt.shape[1]
    far = float(LANES)
    row8 = lax.broadcasted_iota(jnp.int32, (8, tm), 0)
    row8f = row8.astype(F32)
    is_grp = row8 < MOE_GROUPS
    lg = jnp.where(is_grp, lt[GRP_LANE0:GRP_LANE0 + 8, :], NEG)
    eg = jnp.exp(lg - jnp.max(lg, axis=0, keepdims=True))
    pg = eg / jnp.sum(eg, axis=0, keepdims=True)
    gp = jnp.max(pg, axis=0, keepdims=True)
    gidx = jnp.min(jnp.where(is_grp & (pg == gp), row8f, far), axis=0, keepdims=True)
    rowf = lax.broadcasted_iota(jnp.int32, (MOE_EXPERTS, tm), 0).astype(F32)
    in_grp = jnp.floor(rowf * (1.0 / MOE_EPG)) == gidx
    le = jnp.where(in_grp, lt[:MOE_EXPERTS, :], NEG)
    ee = jnp.exp(le - jnp.max(le, axis=0, keepdims=True))
    pe = ee / jnp.sum(ee, axis=0, keepdims=True)
    p1 = jnp.max(jnp.where(in_grp, pe, -1.0), axis=0, keepdims=True)
    i1 = jnp.min(jnp.where(in_grp & (pe == p1), rowf, far), axis=0, keepdims=True)
    rest = in_grp & (rowf != i1)
    p2 = jnp.max(jnp.where(rest, pe, -1.0), axis=0, keepdims=True)
    i2 = jnp.min(jnp.where(rest & (pe == p2), rowf, far), axis=0, keepdims=True)
    scale = gp / (p1 + p2)
    return jnp.where(rowf == i1, p1 * scale, jnp.where(rowf == i2, p2 * scale, 0.0)), gidx


MOE_SUB = 128


def _mix_moe_ln_kernel(o_ref, wo_ref, res_ref, mg_ref, mb_ref, wrh_ref, wrl_ref, br_ref, utri_ref, wgu_ref, wd_ref,
                       g_ref, b_ref, out_ref, h_ref, xs_ref, gs_ref, ys_ref, *, tm, slots):
    shift = MOE_SUB.bit_length() - 1

    h = _layer_norm(DEEPNORM_ALPHA * res_ref[...] + _dot(o_ref[...], wo_ref[...]), mg_ref[...], mb_ref[...])
    h_ref[...] = h
    xb, xlo = _split2(h)
    wh = wrh_ref[...]
    lt = _dot_nt(wh, xb) + (_dot_nt(wh, xlo) + _dot_nt(wrl_ref[...], xb)) + br_ref[...]
    gate, gidx = _route_t(lt)
    row8 = lax.broadcasted_iota(jnp.int32, (8, tm), 0)
    onehot = jnp.where((row8.astype(F32) == gidx) & (row8 < MOE_GROUPS), 1.0, 0.0)
    pos = _dot(onehot.astype(BF16), utri_ref[...])
    cnt = jnp.sum(onehot, axis=1, keepdims=True).astype(jnp.int32)
    offs, n_subs, off = [], [], 0
    for g in range(MOE_GROUPS):
        n_sub = lax.shift_right_logical(cnt[g, 0] + (MOE_SUB - 1), shift)
        offs.append(off)
        n_subs.append(n_sub)
        off = off + lax.shift_left(n_sub, shift)
    off_col = jnp.zeros((8, 1), jnp.int32)
    for g in range(1, MOE_GROUPS):
        off_col = jnp.where(row8[:, 0:1] == g, offs[g], off_col)
    dest_row = jnp.sum(onehot * (pos + off_col.astype(F32)), axis=0, keepdims=True)
    slot = lax.broadcasted_iota(jnp.int32, (slots, tm), 0).astype(F32)
    perm = jnp.where(slot == dest_row, 1.0, 0.0).astype(BF16)
    xs_ref[...] = _dot(perm, xb).astype(BF16)
    hi = jnp.floor(dest_row * (1.0 / 32.0))
    extra = jnp.where(row8 == 0, hi, jnp.where(row8 == 1, dest_row - 32.0 * hi, 0.0))
    per_tok = jnp.concatenate([gate, extra, jnp.zeros((LANES - MOE_EXPERTS - 8, tm), F32)], axis=0)
    gs_ref[...] = _dot_nt(per_tok.astype(BF16), perm).T
    tok = per_tok.T
    dest = 32.0 * tok[:, MOE_EXPERTS:MOE_EXPERTS + 1] + tok[:, MOE_EXPERTS + 1:MOE_EXPERTS + 2]
    ys_ref[...] = jnp.zeros_like(ys_ref)

    for g in range(MOE_GROUPS):
        def sub_tile(j, carry, g=g):
            r0 = pl.multiple_of(offs[g] + j * MOE_SUB, MOE_SUB)
            xsub = xs_ref[pl.ds(r0, MOE_SUB), :]
            gsub = gs_ref[pl.ds(r0, MOE_SUB), :]
            hids = []
            for e in range(MOE_EPG):
                ge = gsub[:, g * MOE_EPG + e:g * MOE_EPG + e + 1]
                hg = _dot(xsub, wgu_ref[g * MOE_EPG + e, :, :MOE_HIDDEN])
                hu = _dot(xsub, wgu_ref[g * MOE_EPG + e, :, MOE_HIDDEN:])
                hids.append((_silu(hg) * hu * ge).astype(BF16))
            ys_ref[pl.ds(r0, MOE_SUB), :] = _dot(jnp.concatenate(hids, axis=1), wd_ref[g]).astype(BF16)
            return carry

        lax.fori_loop(0, n_subs[g], sub_tile, 0)

    slot_l = lax.broadcasted_iota(jnp.int32, (tm, slots), 1).astype(F32)
    unperm = jnp.where(slot_l == dest, 1.0, 0.0).astype(BF16)
    y = _dot(unperm, ys_ref[...])
    out_ref[...] = _layer_norm(DEEPNORM_ALPHA * h_ref[...] + y, g_ref[...], b_ref[...])


def mix_moe_ln(o, w_o, res, mix_g, mix_b, w_grp, b_grp, w_rt, b_rt, w_gate, w_up, w_down, ln_g, ln_b, tm=512):
    n, d = res.shape
    kdim = o.shape[1]
    slots = MOE_SUB * ((tm + MOE_GROUPS * (MOE_SUB - 1)) // MOE_SUB)
    wr = jnp.zeros((LANES, d), F32).at[:MOE_EXPERTS].set(w_rt.T).at[GRP_LANE0:GRP_LANE0 + MOE_GROUPS].set(w_grp.T)
    br = jnp.zeros((LANES, 1), F32).at[:MOE_EXPERTS, 0].set(b_rt).at[GRP_LANE0:GRP_LANE0 + MOE_GROUPS, 0].set(b_grp)
    wrh = wr.astype(BF16)
    wrl = (wr - wrh.astype(F32)).astype(BF16)
    hid_w = MOE_EPG * MOE_HIDDEN
    wgu = jnp.concatenate([w_gate, w_up], axis=-1).astype(BF16)
    wd = w_down.reshape(MOE_GROUPS, hid_w, d).astype(BF16)
    r = jnp.arange(tm)
    utri = (r[:, None] < r[None, :]).astype(BF16)
    row = lambda i: (i, 0)
    once = pl.Buffered(1)
    const2 = lambda shape: pl.BlockSpec(shape, lambda i: (0, 0), pipeline_mode=once)
    const3 = lambda shape: pl.BlockSpec(shape, lambda i: (0, 0, 0), pipeline_mode=once)
    return pl.pallas_call(
        functools.partial(_mix_moe_ln_kernel, tm=tm, slots=slots),
        grid=(n // tm,),
        in_specs=[pl.BlockSpec((tm, kdim), row), const2((kdim, d)), pl.BlockSpec((tm, d), row),
                  const2((1, d)), const2((1, d)),
                  const2((LANES, d)), const2((LANES, d)), const2((LANES, 1)),
                  const2((tm, tm)),
                  const3((MOE_EXPERTS, d, 2 * MOE_HIDDEN)), const3((MOE_GROUPS, hid_w, d)),
                  const2((1, d)), const2((1, d))],
        out_specs=pl.BlockSpec((tm, d), row),
        out_shape=jax.ShapeDtypeStruct((n, d), F32),
        scratch_shapes=[pltpu.VMEM((tm, d), F32),
                        pltpu.VMEM((slots, d), BF16), pltpu.VMEM((slots, LANES), F32), pltpu.VMEM((slots, d), BF16)],
        compiler_params=_params(("arbitrary",)),
        name="mix_moe_ln",
    )(o, w_o.astype(BF16), res, mix_g.reshape(1, d), mix_b.reshape(1, d), wrh, wrl, br, utri, wgu, wd,
      ln_g.reshape(1, d), ln_b.reshape(1, d))


KV_W = NSA_GROUPS * NSA_HEAD_DIM
GATE_ROWS = 16


def _nsa_proj_kernel(h_ref, wq_ref, wg_ref, wc_ref, wks_ref, wkw_ref, wvs_ref, wvw_ref,
                     qT_ref, gT_ref, xk_ref, xv_ref, ksa_ref, kwa_ref, vsT_ref, vwT_ref, cmp_scr, *, tm):
    t0 = pl.program_id(1) * tm
    hb = h_ref[0].astype(BF16)
    qT_ref[0] = _dot_nt(wq_ref[...], hb).astype(BF16)
    gT_ref[0] = jax.nn.sigmoid(_dot_nt(wg_ref[...], hb))
    craw = _dot(hb, wc_ref[...])
    n_cb = 2 * KV_W // LANES
    for cb in range(n_cb):
        cmp_scr[cb] = craw[:, cb * LANES:(cb + 1) * LANES]
    nrow = tm // CMP_STRIDE
    lane_r = lax.broadcasted_iota(jnp.int32, (nrow, LANES), 1)
    for j in range(CMP_STRIDE // 2):
        for cb in range(n_cb):
            a = cmp_scr[cb, pl.ds(2 * j, nrow, stride=CMP_STRIDE), :]
            bm = cmp_scr[cb, pl.ds(2 * j + 1, nrow, stride=CMP_STRIDE), :]
            pieces = (jnp.where(lane_r < NSA_HEAD_DIM, a, pltpu.roll(bm, NSA_HEAD_DIM, 1)),
                      jnp.where(lane_r < NSA_HEAD_DIM, pltpu.roll(a, NSA_HEAD_DIM, 1), bm))
            for k, val in enumerate(pieces):
                pg = 2 * cb + k
                ref = xk_ref if pg < NSA_GROUPS else xv_ref
                ref[0, pg % NSA_GROUPS, :, j * LANES:(j + 1) * LANES] = val.astype(BF16)
    vsT_ref[0] = _dot_nt(wvs_ref[...], hb).astype(BF16)
    vwT_ref[0] = _dot_nt(wvw_ref[...], hb).astype(BF16)
    ks = _dot(hb, wks_ref[...])
    kw = _dot(hb, wkw_ref[...])
    lane = lax.broadcasted_iota(jnp.int32, (tm, LANES), 1)
    blk = (t0 + lax.broadcasted_iota(jnp.int32, (tm, LANES), 0)) // SEL_BLOCK
    onehot = jnp.where(lane - NSA_HEAD_DIM == blk, 1.0, 0.0)
    for g in range(NSA_GROUPS):
        ksg = ks[:, g * LANES:(g + 1) * LANES]
        ksa_ref[0, g] = jnp.where(lane < NSA_HEAD_DIM, ksg, onehot).astype(BF16)
        kwa_ref[0, g] = kw[:, g * LANES:(g + 1) * LANES].astype(BF16)


def _pad_group_cols(w):
    d = w.shape[0]
    w4 = w.reshape(d, NSA_GROUPS, NSA_HEAD_DIM)
    return jnp.concatenate([w4, jnp.zeros_like(w4)], axis=-1).reshape(d, NSA_GROUPS * LANES)


def nsa_proj(h, w_kv, w_in, tm=512):
    b, t, d = h.shape
    scale = NSA_HEAD_DIM ** -0.5 * LOG2E
    wq = (w_in[:, :NSA_WIDTH] * scale).T.astype(BF16)
    wgate = w_in[:, NSA_WIDTH:].reshape(d, NSA_GROUPS, NSA_HPG, N_BRANCHES)
    wgate = wgate.transpose(0, 1, 3, 2).reshape(d, NSA_GROUPS, N_BRANCHES * NSA_HPG)
    wgate = jnp.concatenate([wgate, jnp.zeros((d, NSA_GROUPS, GATE_ROWS - N_BRANCHES * NSA_HPG), F32)], axis=-1)
    wg = wgate.reshape(d, NSA_GROUPS * GATE_ROWS).T.astype(BF16)
    part = lambda p: w_kv[:, p * KV_W:(p + 1) * KV_W]
    wc = w_kv[:, :2 * KV_W].astype(BF16)
    wks = _pad_group_cols(part(2)).astype(BF16)
    wvs = part(3).T.astype(BF16)
    wkw = _pad_group_cols(part(4)).astype(BF16)
    wvw = part(5).T.astype(BF16)
    const = lambda bb, tt: (0, 0)
    tok = lambda bb, tt: (bb, tt, 0)
    tr = lambda bb, tt: (bb, 0, tt)
    g4 = lambda bb, tt: (bb, 0, tt, 0)
    ng = NSA_GROUPS * GATE_ROWS
    wide = CMP_STRIDE * NSA_HEAD_DIM
    return pl.pallas_call(
        functools.partial(_nsa_proj_kernel, tm=tm),
        grid=(b, t // tm),
        in_specs=[pl.BlockSpec((1, tm, d), tok),
                  pl.BlockSpec((NSA_WIDTH, d), const), pl.BlockSpec((ng, d), const),
                  pl.BlockSpec((d, 2 * KV_W), const),
                  pl.BlockSpec((d, NSA_GROUPS * LANES), const), pl.BlockSpec((d, NSA_GROUPS * LANES), const),
                  pl.BlockSpec((KV_W, d), const), pl.BlockSpec((KV_W, d), const)],
        out_specs=[pl.BlockSpec((1, NSA_WIDTH, tm), tr), pl.BlockSpec((1, ng, tm), tr),
                   pl.BlockSpec((1, NSA_GROUPS, tm // CMP_STRIDE, wide), g4),
                   pl.BlockSpec((1, NSA_GROUPS, tm // CMP_STRIDE, wide), g4),
                   pl.BlockSpec((1, NSA_GROUPS, tm, LANES), g4), pl.BlockSpec((1, NSA_GROUPS, tm, LANES), g4),
                   pl.BlockSpec((1, KV_W, tm), tr), pl.BlockSpec((1, KV_W, tm), tr)],
        out_shape=[jax.ShapeDtypeStruct((b, NSA_WIDTH, t), BF16), jax.ShapeDtypeStruct((b, ng, t), F32),
                   jax.ShapeDtypeStruct((b, NSA_GROUPS, t // CMP_STRIDE, wide), BF16),
                   jax.ShapeDtypeStruct((b, NSA_GROUPS, t // CMP_STRIDE, wide), BF16),
                   jax.ShapeDtypeStruct((b, NSA_GROUPS, t, LANES), BF16),
                   jax.ShapeDtypeStruct((b, NSA_GROUPS, t, LANES), BF16),
                   jax.ShapeDtypeStruct((b, KV_W, t), BF16), jax.ShapeDtypeStruct((b, KV_W, t), BF16)],
        scratch_shapes=[pltpu.VMEM((2 * KV_W // LANES, tm, LANES), F32)],
        compiler_params=_params(("arbitrary", "arbitrary")),
        name="nsa_proj",
    )(h, wq, wg, wc, wks, wkw, wvs, wvw)


N_CMP_PAD = 128


def _nsa_compress_kernel(xk_ref, xv_ref, pk_ref, pv_ref, w1k_ref, w1v_ref, w2k_ref, w2vT_ref, kc_ref, vcT_ref):
    half = CMP_STRIDE * NSA_HEAD_DIM
    pbk = _dot(pk_ref[...], w1k_ref[...])[0:1, :]
    pbv = _dot(pv_ref[...], w1v_ref[...])[0:1, :]
    for g in range(NSA_GROUPS):
        xk = xk_ref[0, g]
        hk = _dot(xk, w1k_ref[:half, :]) + pltpu.roll(_dot(xk, w1k_ref[half:, :]), N_CMP_PAD - 1, 0) + pbk
        kc_ref[0, g] = _dot(_silu(hk).astype(BF16), w2k_ref[...]).astype(BF16)
        xv = xv_ref[0, g]
        hv = _dot(xv, w1v_ref[:half, :]) + pltpu.roll(_dot(xv, w1v_ref[half:, :]), N_CMP_PAD - 1, 0) + pbv
        vcT_ref[0, g] = _dot_nt(w2vT_ref[...], _silu(hv).astype(BF16)).astype(BF16)


def nsa_compress(xk, xv, k_pos, k_w1, k_w2, v_pos, v_w1, v_w2):
    b, _, nrow, wide = xk.shape
    assert nrow == N_CMP_PAD
    pad_pos = lambda p: jnp.zeros((8, CMP_BLOCK * NSA_HEAD_DIM), F32).at[0].set(p.reshape(-1)).astype(BF16)
    spec_x = pl.BlockSpec((1, NSA_GROUPS, nrow, wide), lambda bb: (bb, 0, 0, 0))
    c2 = lambda bb: (0, 0)
    return pl.pallas_call(
        _nsa_compress_kernel,
        grid=(b,),
        in_specs=[spec_x, spec_x,
                  pl.BlockSpec((8, 2 * wide), c2), pl.BlockSpec((8, 2 * wide), c2),
                  pl.BlockSpec((2 * wide, CMP_HIDDEN), c2), pl.BlockSpec((2 * wide, CMP_HIDDEN), c2),
                  pl.BlockSpec((CMP_HIDDEN, NSA_HEAD_DIM), c2), pl.BlockSpec((NSA_HEAD_DIM, CMP_HIDDEN), c2)],
        out_specs=[pl.BlockSpec((1, NSA_GROUPS, N_CMP_PAD, NSA_HEAD_DIM), lambda bb: (bb, 0, 0, 0)),
                   pl.BlockSpec((1, NSA_GROUPS, NSA_HEAD_DIM, N_CMP_PAD), lambda bb: (bb, 0, 0, 0))],
        out_shape=[jax.ShapeDtypeStruct((b, NSA_GROUPS, N_CMP_PAD, NSA_HEAD_DIM), BF16),
                   jax.ShapeDtypeStruct((b, NSA_GROUPS, NSA_HEAD_DIM, N_CMP_PAD), BF16)],
        compiler_params=_params(("arbitrary",)),
        name="nsa_compress",
    )(xk, xv, pad_pos(k_pos), pad_pos(v_pos), k_w1.astype(BF16), v_w1.astype(BF16),
      k_w2.astype(BF16), v_w2.T.astype(BF16))


SEL_CHUNK = 256
WIN_CHUNK = 128
TB_ROWS = WINDOW + SEL_CHUNK
TBC_ROWS = 256
TBC_OFF = 120
N_SEL_BLOCKS = 32
QW = NSA_HPG * Q_BLOCK


def _t5_bucket(dist):
    n = jnp.maximum(dist, 0)
    max_exact = REL_BUCKETS // 2
    nf = jnp.maximum(n, 1).astype(F32)
    large = max_exact + (jnp.log(nf / max_exact) / math.log(REL_MAX_DIST / max_exact)
                         * (REL_BUCKETS - max_exact)).astype(jnp.int32)
    large = jnp.minimum(large, REL_BUCKETS - 1)
    return jnp.where(n < max_exact, n, large)


def _bias_tables(rel_bias):
    rel = rel_bias.astype(F32)
    far = rel[REL_BUCKETS - 1]
    ql = jnp.arange(Q_BLOCK)
    heads = jnp.arange(NSA_HEADS).reshape(NSA_GROUPS, NSA_HPG)

    def table(dist, valid):
        bucket = _t5_bucket(dist)[None, :, None, :]
        relc = ((rel - far) * LOG2E)[:, heads][:, :, None, :, None]
        bias = jnp.zeros((NSA_GROUPS, dist.shape[0], NSA_HPG, Q_BLOCK), F32)
        for b in range(REL_BUCKETS):
            bias = jnp.where(bucket == b, relc[b], bias)
        bias = jnp.where(valid[None, :, None, :], bias, NEG)
        return bias.reshape(NSA_GROUPS, dist.shape[0], QW)

    d = ql[None, :] - (jnp.arange(TB_ROWS)[:, None] - WINDOW)
    tb = table(d, (d >= 0) & (d < WINDOW))
    m = jnp.arange(TBC_ROWS)[:, None] - TBC_OFF
    dc = ql[None, :] - CMP_STRIDE * m - (CMP_BLOCK - 1)
    tbc = table(dc, dc >= 0)
    return tb, tbc


def _nsa_attn_kernel(qT_ref, gT_ref, kc_ref, vcT_ref, ksa_ref, vsT_ref, kwa_ref, vwT_ref, tb_ref, tbc_ref, ovl_ref,
                     o_ref, m_ref, l_ref, acc_ref, ot_ref, oT_ref, sc_ref, selm_ref):
    c = pl.program_id(1)
    ng, hpg, hd = NSA_GROUPS, NSA_HPG, NSA_HEAD_DIM
    nsel = N_SEL_BLOCKS
    n_top = min(TOP_N, nsel)
    n_win = WINDOW // WIN_CHUNK

    q4 = jnp.stack([jnp.concatenate([qT_ref[0, (g * hpg + hh) * hd:(g * hpg + hh + 1) * hd, :]
                                     for hh in range(hpg)], axis=1) for g in range(ng)])
    gates = gT_ref[0]

    def gate3(br):
        return jnp.stack([jnp.concatenate(
            [gates[g * GATE_ROWS + br * hpg + hh:g * GATE_ROWS + br * hpg + hh + 1, :] for hh in range(hpg)], axis=1)
            for g in range(ng)])

    def fresh():
        return (jnp.full((ng, 1, QW), NEG, F32), jnp.zeros((ng, 1, QW), F32), jnp.zeros((ng, hd, QW), F32))

    def load_state():
        return m_ref[...], l_ref[...], acc_ref[...]

    def store_state(state):
        m_ref[...], l_ref[...], acc_ref[...] = state

    def upd(state, s, vT):
        m_old, l_old, acc_old = state
        m_new = jnp.maximum(m_old, jnp.max(s, axis=1, keepdims=True))
        alpha = jnp.exp2(m_old - m_new)
        p = jnp.exp2(s - m_new)
        return (m_new, l_old * alpha + jnp.sum(p, axis=1, keepdims=True),
                acc_old * alpha + _bmm(vT, p.astype(BF16)))

    def result(state, br):
        _, l_fin, acc_fin = state
        return gate3(br) * (acc_fin * (1.0 / l_fin))

    start = pl.multiple_of(TBC_OFF - 8 * c, 8)
    tbc = tbc_ref[:, pl.ds(start, N_CMP_PAD), :]
    nrow = lax.broadcasted_iota(jnp.int32, tbc.shape, 1)
    tbc = jnp.where(nrow < N_CMP_PAD - 1, tbc, NEG)
    s = _bmm(kc_ref[0], q4) + tbc
    e = jnp.exp2(s - jnp.maximum(jnp.max(s, axis=1, keepdims=True), 0.1 * NEG))
    l = jnp.sum(e, axis=1, keepdims=True)
    p = e * (1.0 / jnp.where(l > 0.0, l, 1.0))
    o_cmp = gate3(0) * _bmm(vcT_ref[0], p.astype(BF16))

    psum = p[:, :, 0:Q_BLOCK]
    for hh in range(1, hpg):
        psum = psum + p[:, :, hh * Q_BLOCK:(hh + 1) * Q_BLOCK]
    psum = jnp.concatenate([psum[g] for g in range(ng)], axis=1)
    ph, plo = _split2(psum)
    ovl = ovl_ref[...]
    imp = _dot(ovl, ph) + _dot(ovl, plo)
    jidx = lax.broadcasted_iota(jnp.int32, imp.shape, 0)
    qlane = lax.broadcasted_iota(jnp.int32, imp.shape, 1) & (Q_BLOCK - 1)
    jq = 2 * c + (qlane >= SEL_BLOCK).astype(jnp.int32)
    forced = (jidx == 0) | (jidx == jq) | (jidx == jq - 1)
    imp = jnp.where(forced, -NEG, jnp.where(jidx <= jq, imp, NEG))
    selm_ref[...] = jnp.zeros_like(selm_ref)

    @pl.when(2 * c + 2 > n_top)
    def _():
        sub8 = lax.broadcasted_iota(jnp.int32, (8, imp.shape[1]), 0)
        rows = [imp[8 * v:8 * v + 8, :] for v in range(nsel // 8)]
        cnts = [jnp.zeros_like(r) for r in rows]
        for i in range(nsel):
            r = imp[i:i + 1, :]
            for v in range(nsel // 8):
                ge = jnp.where(r >= rows[v], 1.0, 0.0)
                gt = jnp.where(r > rows[v], 1.0, 0.0)
                if i < 8 * v:
                    ahead = ge
                elif i >= 8 * v + 8:
                    ahead = gt
                else:
                    ahead = jnp.where(sub8 > i - 8 * v, ge, gt)
                cnts[v] = cnts[v] + ahead
        cnt = jnp.concatenate(cnts, axis=0)
        selm_ref[...] = jnp.where(cnt < float(n_top), 0.0, NEG)

    selm = selm_ref[...].astype(BF16)
    pad = jnp.zeros((LANES - hd - nsel, QW), BF16)
    q_aug = jnp.stack([jnp.concatenate(
        [q4[g], jnp.concatenate([selm[:, g * Q_BLOCK:(g + 1) * Q_BLOCK]] * hpg, axis=1), pad], axis=0)
        for g in range(ng)])

    def sel_scores(k0):
        return _bmm(ksa_ref[0, :, pl.ds(k0, SEL_CHUNK), :], q_aug)

    def sel_values(k0):
        return vsT_ref[0, :, pl.ds(k0, SEL_CHUNK)].reshape(ng, hd, SEL_CHUNK)

    def win(first, n_chunks, tab_chunk):
        width = n_chunks * WIN_CHUNK
        k0 = pl.multiple_of((c - n_win + first) * WIN_CHUNK, WIN_CHUNK)
        s_w = _bmm(kwa_ref[0, :, pl.ds(k0, width), :], q_aug)
        if tab_chunk is not None:
            lo = (tab_chunk - first) * WIN_CHUNK
            part = s_w[:, lo:lo + WIN_CHUNK] + tb_ref[:, tab_chunk * WIN_CHUNK:(tab_chunk + 1) * WIN_CHUNK, :]
            pieces = ([s_w[:, :lo]] if lo else []) + [part] + ([s_w[:, lo + WIN_CHUNK:]] if lo + WIN_CHUNK < width else [])
            s_w = jnp.concatenate(pieces, axis=1) if len(pieces) > 1 else part
        return s_w, vwT_ref[0, :, pl.ds(k0, width)].reshape(ng, hd, width)

    def raw_scores(i):
        return sel_scores(pl.multiple_of(i * SEL_CHUNK, SEL_CHUNK))

    def values(i):
        return sel_values(pl.multiple_of(i * SEL_CHUNK, SEL_CHUNK))

    def sel_table(i):
        tstart = pl.multiple_of(WINDOW - (c * Q_BLOCK - i * SEL_CHUNK), Q_BLOCK)
        return tb_ref[:, pl.ds(tstart, SEL_CHUNK), :]

    store_state(fresh())
    n_far = jnp.maximum(c // 2 - 1, 0)

    @pl.when(c >= 2)
    def _():
        sc_ref[0] = raw_scores(0)

    def far_pair(j, carry):
        i = 2 * j
        sc_ref[1] = raw_scores(i + 1)
        store_state(upd(load_state(), sc_ref[0], values(i)))
        sc_ref[0] = raw_scores(i + 2)
        store_state(upd(load_state(), sc_ref[1], values(i + 1)))
        return carry

    lax.fori_loop(0, n_far // 2, far_pair, 0)

    @pl.when(n_far % 2 == 1)
    def _():
        sc_ref[1] = raw_scores(n_far)
        store_state(upd(load_state(), sc_ref[0], values(n_far - 1)))

    @pl.when(c >= n_win)
    def _():
        s_a = sc_ref[n_far % 2] + sel_table(c // 2 - 1)
        s_b = raw_scores(c // 2)
        w_a, wv_a = win(0, 2, 0)
        st_s = upd(load_state(), s_a, values(c // 2 - 1))
        w_b, wv_b = win(2, 2, 3)
        st_w = upd(fresh(), w_a, wv_a)
        st_s = upd(st_s, s_b + sel_table(c // 2), values(c // 2))
        w_c, wv_c = win(4, 1, 4)
        st_w = upd(st_w, w_b, wv_b)
        st_w = upd(st_w, w_c, wv_c)
        ot_ref[...] = o_cmp + result(st_s, 1) + result(st_w, 2)

    @pl.when(c < n_win)
    def _():
        @pl.when(c >= 2)
        def _():
            store_state(upd(load_state(), sc_ref[0] + sel_table(c // 2 - 1), values(c // 2 - 1)))

        ot_ref[...] = o_cmp + result(upd(load_state(), raw_scores(c // 2) + sel_table(c // 2), values(c // 2)), 1)
        store_state(fresh())
        pl.when(c == 3)(lambda: store_state(upd(load_state(), *win(1, 1, None))))
        pl.when(c >= 2)(lambda: store_state(upd(load_state(), *win(2, 2, 3))))
        pl.when(c == 1)(lambda: store_state(upd(load_state(), *win(3, 1, 3))))
        ot_ref[...] += result(upd(load_state(), *win(4, 1, 4)), 2)

    for g in range(ng):
        for hh in range(hpg):
            oT_ref[hh, g * hd:(g + 1) * hd, :] = ot_ref[g, :, hh * Q_BLOCK:(hh + 1) * Q_BLOCK]
    for hh in range(hpg):
        o_ref[0, :, hh * KV_W:(hh + 1) * KV_W] = oT_ref[hh].T.astype(BF16)


def nsa_attn(qT, gT, kc, vcT, ksa, vsT, kwa, vwT, rel_bias):
    b, _, t = qT.shape
    nq = t // Q_BLOCK
    tb, tbc = _bias_tables(rel_bias)
    nc = (t - CMP_BLOCK) // CMP_STRIDE + 1
    cs = jnp.arange(N_CMP_PAD) * CMP_STRIDE
    ss = jnp.arange(N_SEL_BLOCKS) * SEL_BLOCK
    ovl = ((cs[None, :] < ss[:, None] + SEL_BLOCK) & (cs[None, :] + CMP_BLOCK - 1 >= ss[:, None])
           & (jnp.arange(N_CMP_PAD)[None, :] < nc)).astype(BF16)
    ng = NSA_GROUPS * GATE_ROWS
    per_b3 = lambda bb, cc: (bb, 0, 0)
    per_b4 = lambda bb, cc: (bb, 0, 0, 0)
    c3 = lambda bb, cc: (0, 0, 0)
    return pl.pallas_call(
        _nsa_attn_kernel,
        grid=(b, nq),
        in_specs=[pl.BlockSpec((1, NSA_WIDTH, Q_BLOCK), lambda bb, cc: (bb, 0, cc)),
                  pl.BlockSpec((1, ng, Q_BLOCK), lambda bb, cc: (bb, 0, cc)),
                  pl.BlockSpec((1, NSA_GROUPS, N_CMP_PAD, NSA_HEAD_DIM), per_b4),
                  pl.BlockSpec((1, NSA_GROUPS, NSA_HEAD_DIM, N_CMP_PAD), per_b4),
                  pl.BlockSpec((1, NSA_GROUPS, t, LANES), per_b4),
                  pl.BlockSpec((1, KV_W, t), per_b3),
                  pl.BlockSpec((1, NSA_GROUPS, t, LANES), per_b4),
                  pl.BlockSpec((1, KV_W, t), per_b3),
                  pl.BlockSpec((NSA_GROUPS, TB_ROWS, QW), c3),
                  pl.BlockSpec((NSA_GROUPS, TBC_ROWS, QW), c3),
                  pl.BlockSpec((N_SEL_BLOCKS, N_CMP_PAD), lambda bb, cc: (0, 0))],
        out_specs=pl.BlockSpec((1, Q_BLOCK, NSA_WIDTH), lambda bb, cc: (bb, cc, 0)),
        out_shape=jax.ShapeDtypeStruct((b, t, NSA_WIDTH), BF16),
        scratch_shapes=[pltpu.VMEM((NSA_GROUPS, 1, QW), F32), pltpu.VMEM((NSA_GROUPS, 1, QW), F32),
                        pltpu.VMEM((NSA_GROUPS, NSA_HEAD_DIM, QW), F32),
                        pltpu.VMEM((NSA_GROUPS, NSA_HEAD_DIM, QW), F32),
                        pltpu.VMEM((NSA_HPG, KV_W, Q_BLOCK), F32),
                        pltpu.VMEM((2, NSA_GROUPS, SEL_CHUNK, QW), F32),
                        pltpu.VMEM((N_SEL_BLOCKS, NSA_GROUPS * Q_BLOCK), F32)],
        compiler_params=_params(("arbitrary", "arbitrary")),
        name="nsa_attn",
    )(qT, gT, kc, vcT, ksa, vsT, kwa, vwT, tb, tbc, ovl)


def kernel(x, gdn_w_in, gdn_conv_w, gdn_a_log, gdn_dt_bias, gdn_norm_w, gdn_w_o, nsa_w_kv, cmp_k_pos, cmp_k_w1,
           cmp_k_w2, cmp_v_pos, cmp_v_w1, cmp_v_w2, nsa_w_in, nsa_w_o, rel_bias, ln_mix_g, ln_mix_b, ln_ffn_g,
           ln_ffn_b, moe_w_grp, moe_b_grp, moe_w_rt, moe_b_rt, moe_w_gate, moe_w_up, moe_w_down):
    b, t, d = x.shape
    n = b * t

    def mix_ffn(o, w_o, res, layer):
        return mix_moe_ln(o, w_o, res, ln_mix_g[layer], ln_mix_b[layer], moe_w_grp[layer], moe_b_grp[layer],
                          moe_w_rt[layer], moe_b_rt[layer], moe_w_gate[layer], moe_w_up[layer], moe_w_down[layer],
                          ln_ffn_g[layer], ln_ffn_b[layer])

    q, k, v, z, gcb, gcbT = gdn_inproj(x, gdn_w_in[0], gdn_conv_w[0], gdn_a_log[0], gdn_dt_bias[0])
    o = gdn_rec(q, k, v, z, gcb, gcbT, gdn_norm_w[0])
    h = mix_ffn(o.reshape(n, GDN_WIDTH), gdn_w_o[0], x.reshape(n, d), 0)

    qT, gT, xk, xv, ksa, kwa, vsT, vwT = nsa_proj(h.reshape(b, t, d), nsa_w_kv, nsa_w_in[0])
    kc, vcT = nsa_compress(xk, xv, cmp_k_pos, cmp_k_w1, cmp_k_w2, cmp_v_pos, cmp_v_w1, cmp_v_w2)
    o = nsa_attn(qT, gT, kc, vcT, ksa, vsT, kwa, vwT, rel_bias)
    w_o = nsa_w_o[0].reshape(NSA_GROUPS, NSA_HPG, NSA_HEAD_DIM, d).transpose(1, 0, 2, 3).reshape(NSA_WIDTH, d)
    h = mix_ffn(o.reshape(n, NSA_WIDTH), w_o, h, 1)
    return h.reshape(b, t, d)
```
